```python
import jax, jax.numpy as jnp
from jax import lax
import numpy as np


D_MODEL = 1024
BATCH = 8
SEQ = 8192
DEPTH = 2

GRID_W = 64
CTX_LEN = 256
HEAD_DIM = 64
A_Q_HEADS = 6
A_KV_HEADS = 2
B_HEADS = 6
C_GROUPS = 4
C_WIDTH = C_GROUPS * HEAD_DIM
A_Q_W = A_Q_HEADS * HEAD_DIM
A_KV_W = A_KV_HEADS * HEAD_DIM
B_W = B_HEADS * HEAD_DIM
Q_W = A_Q_W + B_W
KV_W = 2 * A_KV_W + 2 * B_W
MIX_W = A_Q_W + B_W + C_WIDTH
PROJ_W = Q_W + KV_W + 3 * C_WIDTH
PROJ_SPLIT = (A_Q_W, Q_W, Q_W + KV_W, Q_W + KV_W + C_WIDTH, Q_W + KV_W + 2 * C_WIDTH)
KV_SPLIT = (A_KV_W, 2 * A_KV_W, 2 * A_KV_W + B_W)
Q_BLOCK = 128
WIN_R = 8
WIN_C = 16
ROPE_FREQS = HEAD_DIM // 4
ROPE_THETA = 10000.0
CONV_W = 3
FFN_DIM = 2816
N_MOD = 9
EPS = 1e-6

kernel_name = 'hybrid_parallel_heads_diffusion_block'


def rms_norm(x, g):
    xf = x.astype(jnp.float32)
    y = xf * lax.rsqrt(jnp.mean(xf * xf, axis=-1, keepdims=True) + EPS)
    return y.astype(x.dtype) * g


def heads(t, n):
    return t.reshape(t.shape[:-1] + (n, HEAD_DIM))


def axial_rope_tables(seq, dtype):
    pos = jnp.arange(seq, dtype=jnp.int32)
    row = (pos // GRID_W).astype(jnp.float32)
    col = (pos % GRID_W).astype(jnp.float32)
    freq = 1.0 / (ROPE_THETA ** (jnp.arange(ROPE_FREQS, dtype=jnp.float32) / ROPE_FREQS))
    ang = jnp.stack([row[:, None] * freq, col[:, None] * freq], axis=1)
    return jnp.cos(ang).astype(dtype), jnp.sin(ang).astype(dtype)


def apply_axial_rope(x, cos, sin):
    xr = x.reshape(x.shape[:-1] + (2, 2, ROPE_FREQS))
    x1, x2 = xr[..., 0, :], xr[..., 1, :]
    cs, sn = cos[None, :, None], sin[None, :, None]
    return jnp.stack([x1 * cs - x2 * sn, x2 * cs + x1 * sn], axis=-2).reshape(x.shape)


def gqa_attention(q, k, v):
    bsz, s_len, hq, dh = q.shape
    hkv = k.shape[2]
    grp = hq // hkv
    nb = s_len // Q_BLOCK
    qb = q.reshape(bsz, nb, Q_BLOCK, hkv, grp, dh).transpose(1, 0, 2, 3, 4, 5)
    scale = dh ** -0.5

    def block(qi):
        s = jnp.einsum('bqhgd,bkhd->bhgqk', qi, k).astype(jnp.float32) * scale
        p = jax.nn.softmax(s, axis=-1).astype(v.dtype)
        return jnp.einsum('bhgqk,bkhd->bqhgd', p, v)

    o = lax.map(block, qb)
    return o.transpose(1, 0, 2, 3, 4, 5).reshape(bsz, s_len, hq * dh)


def neighbourhood_attention(q, k, v, k_ctx, v_ctx, rpb, rows):
    bsz, s_len, nh, dh = q.shape
    wr = min(WIN_R, rows)
    n_win = wr * WIN_C
    scale = dh ** -0.5
    qg = q.reshape(bsz, rows, GRID_W, nh, dh)
    kg = k.reshape(bsz, rows, GRID_W, nh, dh)
    vg = v.reshape(bsz, rows, GRID_W, nh, dh)
    col = jnp.arange(GRID_W)
    cs = jnp.clip(col - WIN_C // 2, 0, GRID_W - WIN_C)
    col_idx = cs[:, None] + jnp.arange(WIN_C)[None, :]
    dc_idx = col_idx - col[:, None] + (WIN_C - 1)

    def row_block(r):
        rs = jnp.clip(r - wr // 2, 0, rows - wr)
        qr = lax.dynamic_index_in_dim(qg, r, axis=1, keepdims=False)
        kw = lax.dynamic_slice_in_dim(kg, rs, wr, axis=1)[:, :, col_idx]
        vw = lax.dynamic_slice_in_dim(vg, rs, wr, axis=1)[:, :, col_idx]
        dr_idx = rs + jnp.arange(wr) - r + (WIN_R - 1)
        bias = rpb[:, dr_idx][:, :, dc_idx].transpose(0, 2, 1, 3)
        s_win = jnp.einsum('bqhd,brqjhd->bhqrj', qr, kw).astype(jnp.float32) * scale + bias.astype(jnp.float32)[None]
        s_ctx = jnp.einsum('bqhd,bkhd->bhqk', qr, k_ctx).astype(jnp.float32) * scale
        s = jnp.concatenate([s_win.reshape(bsz, nh, GRID_W, n_win), s_ctx], axis=-1)
        p = jax.nn.softmax(s, axis=-1).astype(v.dtype)
        p_win = p[..., :n_win].reshape(bsz, nh, GRID_W, wr, WIN_C)
        return (jnp.einsum('bhqrj,brqjhd->bqhd', p_win, vw)
                + jnp.einsum('bhqk,bkhd->bqhd', p[..., n_win:], v_ctx))

    o = lax.map(row_block, jnp.arange(rows, dtype=jnp.int32))
    return o.transpose(1, 0, 2, 3, 4).reshape(bsz, s_len, nh * dh)


def short_conv_mixer(xi, gate_b, gate_c, w):
    z = gate_c * xi
    y = lax.conv_general_dilated(z, w[:, None, :], window_strides=(1,),
                                 padding=((CONV_W // 2, CONV_W // 2),),
                                 dimension_numbers=('NWC', 'WIO', 'NWC'),
                                 feature_group_count=z.shape[-1])
    return gate_b * y


def swiglu(u, wi, wo):
    g, up = jnp.split(u @ wi, 2, axis=-1)
    return (jax.nn.silu(g) * up) @ wo


def split_kv(pkv, k_gain):
    ka, va, kb, vb = jnp.split(pkv, KV_SPLIT, axis=-1)
    return (rms_norm(heads(ka, A_KV_HEADS), k_gain), heads(va, A_KV_HEADS),
            heads(kb, B_HEADS), heads(vb, B_HEADS))


def context_mix(p, qk_g, conv_w):
    qa, qb, pkv, cx, cb, cc = jnp.split(p, PROJ_SPLIT, axis=-1)
    ka, va, kb, vb = split_kv(pkv, qk_g[1])
    o_a = gqa_attention(rms_norm(heads(qa, A_Q_HEADS), qk_g[0]), ka, va)
    o_b = gqa_attention(heads(qb, B_HEADS), kb, vb)
    o_c = short_conv_mixer(cx, cb, cc, conv_w)
    return jnp.concatenate([o_a, o_b, o_c], axis=-1), (ka, va, kb, vb)


def latent_mix(p, kv_ctx, cos, sin, qk_g, rpb, conv_w, rows):
    qa, qb, pkv, cx, cb, cc = jnp.split(p, PROJ_SPLIT, axis=-1)
    ka, va, kb, vb = split_kv(pkv, qk_g[1])
    ka_c, va_c, kb_c, vb_c = kv_ctx
    qa = apply_axial_rope(rms_norm(heads(qa, A_Q_HEADS), qk_g[0]), cos, sin)
    ka = apply_axial_rope(ka, cos, sin)
    o_a = gqa_attention(qa, jnp.concatenate([ka, ka_c], axis=1), jnp.concatenate([va, va_c], axis=1))
    o_b = neighbourhood_attention(heads(qb, B_HEADS), kb, vb, kb_c, vb_c, rpb, rows)
    o_c = short_conv_mixer(cx, cb, cc, conv_w)
    return jnp.concatenate([o_a, o_b, o_c], axis=-1)


def hybrid_layer(h, hc, c, c_ctx, w_ada, b_ada, g, w_in, qk_g, rpb, conv_w, w_o, ffn_wi, ffn_wo,
                 cos, sin, rows, last):
    m = (jax.nn.silu(c) @ w_ada + b_ada).reshape(c.shape[0], 1, N_MOD, D_MODEL)
    mc = (jax.nn.silu(c_ctx) @ w_ada + b_ada).reshape(1, 1, N_MOD, D_MODEL)

    def pre(t, mm, i, gi):
        return rms_norm(t, g[gi]) * (1 + mm[:, :, i + 1]) + mm[:, :, i]

    def post(t, y, mm, i, gi, wres):
        return t + wres * mm[:, :, i + 2] * rms_norm(y, g[gi])

    h = post(h, swiglu(pre(h, m, 0, 0), ffn_wi[0], ffn_wo[0]), m, 0, 1, 0.5)
    hc = post(hc, swiglu(pre(hc, mc, 0, 0), ffn_wi[0], ffn_wo[0]), mc, 0, 1, 0.5)
    u = pre(h, m, 3, 2)
    uc = pre(hc, mc, 3, 2)
    if last:
        kv_c = split_kv(uc @ w_in[:, Q_W:Q_W + KV_W], qk_g[1])
    else:
        oc, kv_c = context_mix(uc @ w_in, qk_g, conv_w)
        hc = post(hc, oc @ w_o, mc, 3, 3, 1.0)
    o = latent_mix(u @ w_in, kv_c, cos, sin, qk_g, rpb, conv_w, rows)
    h = post(h, o @ w_o, m, 3, 3, 1.0)
    h = post(h, swiglu(pre(h, m, 6, 4), ffn_wi[1], ffn_wo[1]), m, 6, 5, 0.5)
    if not last:
        hc = post(hc, swiglu(pre(hc, mc, 6, 4), ffn_wi[1], ffn_wo[1]), mc, 6, 5, 0.5)
    return h, hc


def setup_inputs(seed: int = 0) -> dict:
    key = jax.random.key(seed)
    ks = jax.random.split(key, 14)
    nrm = jax.random.normal
    f32 = jnp.float32
    return {
        'x': nrm(ks[0], (BATCH, SEQ, D_MODEL), f32),
        'c': nrm(ks[1], (BATCH, D_MODEL), f32),
        'ctx': nrm(ks[2], (BATCH, CTX_LEN, D_MODEL), f32),
        'c_ctx': nrm(ks[3], (D_MODEL,), f32),
        'w_ada': nrm(ks[4], (DEPTH, D_MODEL, N_MOD * D_MODEL), f32) * (0.5 * D_MODEL ** -0.5),
        'b_ada': 0.01 * nrm(ks[5], (DEPTH, N_MOD * D_MODEL), f32),
        'norm_g': 1.0 + 0.05 * nrm(ks[6], (DEPTH, 6, D_MODEL), f32),
        'w_in': nrm(ks[7], (DEPTH, D_MODEL, PROJ_W), f32) * D_MODEL ** -0.5,
        'qk_g': 1.0 + 0.05 * nrm(ks[8], (DEPTH, 2, HEAD_DIM), f32),
        'rpb': 0.1 * nrm(ks[9], (DEPTH, B_HEADS, 2 * WIN_R - 1, 2 * WIN_C - 1), f32),
        'conv_w': nrm(ks[10], (DEPTH, CONV_W, C_WIDTH), f32) * CONV_W ** -0.5,
        'w_o': nrm(ks[11], (DEPTH, MIX_W, D_MODEL), f32) * MIX_W ** -0.5,
        'ffn_wi': nrm(ks[12], (DEPTH, 2, D_MODEL, 2 * FFN_DIM), f32) * D_MODEL ** -0.5,
        'ffn_wo': nrm(ks[13], (DEPTH, 2, FFN_DIM, D_MODEL), f32) * FFN_DIM ** -0.5,
    }


def reference(x, c, ctx, c_ctx, w_ada, b_ada, norm_g, w_in, qk_g, rpb, conv_w, w_o, ffn_wi, ffn_wo):
    seq = x.shape[1]
    rows = seq // GRID_W
    cos, sin = axial_rope_tables(seq, x.dtype)
    h, hc = x, ctx
    for i in range(DEPTH):
        h, hc = hybrid_layer(h, hc, c, c_ctx, w_ada[i], b_ada[i], norm_g[i], w_in[i], qk_g[i], rpb[i],
                             conv_w[i], w_o[i], ffn_wi[i], ffn_wo[i], cos, sin, rows, i == DEPTH - 1)
    return h
```

```python
import functools

import numpy as np
import jax
import jax.numpy as jnp
from jax import lax
from jax.experimental import pallas as pl
from jax.experimental.pallas import tpu as pltpu

F32 = jnp.float32
BF16 = jnp.bfloat16

HEAD_DIM = 64
GRID_W = 64
A_Q_HEADS = 6
A_KV_HEADS = 2
B_HEADS = 6
C_WIDTH = 256
A_Q_W = A_Q_HEADS * HEAD_DIM
A_KV_W = A_KV_HEADS * HEAD_DIM
B_W = B_HEADS * HEAD_DIM
WIN_R = 8
WIN_C = 16
ROPE_FREQS = HEAD_DIM // 4
ROPE_THETA = 10000.0
N_MOD = 9
EPS = 1e-6
NEG = -1e30

V7X_LANES = 128
V7X_SCOPED_VMEM_BYTES = 60000 * 1024

NBR_QROWS = 4
NBR_KROWS = NBR_QROWS + WIN_R


def _vmem_limit(estimate_bytes):
    return int(min(max(estimate_bytes, 16 * 1024 * 1024), V7X_SCOPED_VMEM_BYTES))


def _params(n_axes, vmem_bytes):
    return pltpu.CompilerParams(
        dimension_semantics=("arbitrary",) * n_axes,
        vmem_limit_bytes=_vmem_limit(vmem_bytes),
    )


def _rms(x, g):
    ms = jnp.mean(x * x, axis=-1, keepdims=True)
    return x * lax.rsqrt(ms + EPS) * g


def _resident(block_shape, index_map):
    return pl.BlockSpec(block_shape, index_map, pipeline_mode=pl.Buffered(1))


def _ada_kernel(c_ref, w_ref, b_ref, o_ref):
    c = c_ref[...]
    sc = c * jax.nn.sigmoid(c)
    o_ref[...] = jnp.dot(sc, w_ref[...], preferred_element_type=F32,
                         precision=lax.Precision.HIGHEST) + b_ref[...]


def _ada(c_all, w_ada, b_ada):
    depth, d, n = w_ada.shape
    rows = c_all.shape[0]
    tn = d
    return pl.pallas_call(
        _ada_kernel,
        grid=(depth, n // tn),
        in_specs=[
            pl.BlockSpec((rows, d), lambda l, j: (0, 0)),
            pl.BlockSpec((None, d, tn), lambda l, j: (l, 0, j)),
            pl.BlockSpec((None, 1, tn), lambda l, j: (l, 0, j)),
        ],
        out_specs=pl.BlockSpec((None, rows, tn), lambda l, j: (l, 0, j)),
        out_shape=jax.ShapeDtypeStruct((depth, rows, n), F32),
        compiler_params=_params(2, 4 * d * tn * 4),
        name="ada_mod",
    )(c_all, w_ada, b_ada.reshape(depth, 1, n))


def _ffn_kernel(h_ref, mod_ref, g_ref, wi_ref, wo_ref, o_ref, *, i0, gi, ffn_dim):
    h = h_ref[...]
    shift = mod_ref[i0:i0 + 1, :]
    scale = mod_ref[i0 + 1:i0 + 2, :]
    gate = mod_ref[i0 + 2:i0 + 3, :]
    u = _rms(h, g_ref[gi:gi + 1, :]) * (1.0 + scale) + shift
    hid = jnp.dot(u.astype(BF16), wi_ref[...], preferred_element_type=F32)
    gt = hid[:, :ffn_dim]
    up = hid[:, ffn_dim:]
    act = (gt * jax.nn.sigmoid(gt) * up).astype(BF16)
    y = jnp.dot(act, wo_ref[...], preferred_element_type=F32)
    o_ref[...] = h + 0.5 * gate * _rms(y, g_ref[gi + 1:gi + 2, :])


def _ffn(h, mods, norm_g, wi, wo, *, layer, which, mod_row, tm):
    bsz, s, d = h.shape
    ffn_dim = wo.shape[2]
    i0 = 6 * which
    gi = 4 * which
    if mod_row is None:
        mod_map = lambda b, i: (layer, b, 0, 0)
    else:
        mod_map = lambda b, i: (layer, mod_row, 0, 0)
    vmem = (wi.shape[2] * wi.shape[3] + wo.shape[2] * wo.shape[3]) * 2 + tm * d * 4 * 6 + tm * ffn_dim * 16
    return pl.pallas_call(
        functools.partial(_ffn_kernel, i0=i0, gi=gi, ffn_dim=ffn_dim),
        grid=(bsz, s // tm),
        in_specs=[
            pl.BlockSpec((None, tm, d), lambda b, i: (b, i, 0)),
            pl.BlockSpec((None, None, N_MOD, d), mod_map),
            _resident((None, norm_g.shape[1], d), lambda b, i: (layer, 0, 0)),
            _resident((None, None, d, 2 * ffn_dim), lambda b, i: (layer, which, 0, 0)),
            _resident((None, None, ffn_dim, d), lambda b, i: (layer, which, 0, 0)),
        ],
        out_specs=pl.BlockSpec((None, tm, d), lambda b, i: (b, i, 0)),
        out_shape=jax.ShapeDtypeStruct(h.shape, F32),
        compiler_params=_params(2, vmem),
        name="ffn",
    )(h, mods, norm_g, wi, wo)


def _swap_rope_partners(x):
    lane = lax.broadcasted_iota(jnp.int32, x.shape, 1)
    first = (lane % (2 * ROPE_FREQS)) < ROPE_FREQS
    return jnp.where(first,
                     pltpu.roll(x, V7X_LANES - ROPE_FREQS, axis=1),
                     pltpu.roll(x, ROPE_FREQS, axis=1))


def _proj_kernel(*refs, rope, tm, n_tiles):
    if rope:
        (h_ref, hp_ref, hn_ref, mod_ref, g_ref, w_ref, qkg_ref, cw_ref, gm_ref, cos_ref, sin_ref,
         qa_ref, qb_ref, ka_ref, va_ref, kb_ref, vb_ref, oc_ref) = refs
    else:
        (h_ref, hp_ref, hn_ref, mod_ref, g_ref, w_ref, qkg_ref, cw_ref, gm_ref,
         qa_ref, qb_ref, ka_ref, va_ref, kb_ref, vb_ref, oc_ref) = refs
    i = pl.program_id(1)
    shift = mod_ref[3:4, :]
    scale = mod_ref[4:5, :]
    g2 = g_ref[2:3, :]

    def pre(x):
        return (_rms(x, g2) * (1.0 + scale) + shift).astype(BF16)

    p = jnp.dot(pre(h_ref[...]), w_ref[...], preferred_element_type=F32)
    o_qa, o_qb = 0, A_Q_W
    o_ka = o_qb + B_W
    o_va = o_ka + A_KV_W
    o_kb = o_va + A_KV_W
    o_vb = o_kb + B_W
    o_cx = o_vb + B_W
    o_cb = o_cx + C_WIDTH
    o_cc = o_cb + C_WIDTH

    xq = jnp.concatenate([p[:, o_qa:o_qa + A_Q_W], p[:, o_ka:o_ka + A_KV_W]], axis=1)
    sq = (xq * xq).astype(BF16)
    half = (A_Q_W + A_KV_W) // 2
    ms = jnp.concatenate(
        [jnp.dot(sq[:, :half], gm_ref[...], preferred_element_type=F32),
         jnp.dot(sq[:, half:], gm_ref[...], preferred_element_type=F32)], axis=1)
    xn = xq * lax.rsqrt(ms + EPS) * qkg_ref[...]
    if rope:
        cos = cos_ref[...]
        sin = sin_ref[...]
        cols = []
        for j in range((A_Q_W + A_KV_W) // V7X_LANES):
            xc = xn[:, j * V7X_LANES:(j + 1) * V7X_LANES]
            cols.append(xc * cos + _swap_rope_partners(xc) * sin)
        xn = jnp.concatenate(cols, axis=1)
    q_scale = HEAD_DIM ** -0.5
    qa_ref[...] = (xn[:, :A_Q_W] * q_scale).astype(BF16)
    ka_ref[...] = xn[:, A_Q_W:].astype(BF16)
    qb_ref[...] = (p[:, o_qb:o_qb + B_W] * q_scale).astype(BF16)
    va_ref[...] = p[:, o_va:o_va + A_KV_W].astype(BF16)
    kb_ref[...] = p[:, o_kb:o_kb + B_W].astype(BF16)
    vb_ref[...] = p[:, o_vb:o_vb + B_W].astype(BF16)

    z = p[:, o_cc:o_cc + C_WIDTH] * p[:, o_cx:o_cx + C_WIDTH]

    def halo(ref):
        ub = pre(ref[...])
        return (jnp.dot(ub, w_ref[:, o_cc:o_cc + C_WIDTH], preferred_element_type=F32)
                * jnp.dot(ub, w_ref[:, o_cx:o_cx + C_WIDTH], preferred_element_type=F32))

    halo_rows = hp_ref.shape[0]
    z_before = jnp.where(i > 0, halo(hp_ref)[halo_rows - 1:halo_rows, :], 0.0)
    z_after = jnp.where(i < n_tiles - 1, halo(hn_ref)[0:1, :], 0.0)
    row = lax.broadcasted_iota(jnp.int32, z.shape, 0)
    z_m1 = jnp.where(row == 0, z_before, pltpu.roll(z, 1, axis=0))
    z_p1 = jnp.where(row == tm - 1, z_after, pltpu.roll(z, tm - 1, axis=0))
    y = cw_ref[0:1, :] * z_m1 + cw_ref[1:2, :] * z + cw_ref[2:3, :] * z_p1
    oc_ref[...] = (p[:, o_cb:o_cb + C_WIDTH] * y).astype(BF16)


def _proj(h, mods, norm_g, w_in, qk_gain, conv_w, gmat, rope_tabs, *, layer, mod_row, tm):
    bsz, s, d = h.shape
    n_tiles = s // tm
    halo_rows = 8
    hb = tm // halo_rows
    n_hblk = s // halo_rows
    rope = rope_tabs is not None
    if mod_row is None:
        mod_map = lambda b, i: (layer, b, 0, 0)
    else:
        mod_map = lambda b, i: (layer, mod_row, 0, 0)
    pw = w_in.shape[2]
    in_specs = [
        pl.BlockSpec((None, tm, d), lambda b, i: (b, i, 0)),
        pl.BlockSpec((None, halo_rows, d), lambda b, i: (b, jnp.maximum(i * hb - 1, 0), 0)),
        pl.BlockSpec((None, halo_rows, d), lambda b, i: (b, jnp.minimum((i + 1) * hb, n_hblk - 1), 0)),
        pl.BlockSpec((None, None, N_MOD, d), mod_map),
        _resident((None, norm_g.shape[1], d), lambda b, i: (layer, 0, 0)),
        _resident((None, d, pw), lambda b, i: (layer, 0, 0)),
        _resident((None, 1, A_Q_W + A_KV_W), lambda b, i: (layer, 0, 0)),
        _resident((None, conv_w.shape[1], C_WIDTH), lambda b, i: (layer, 0, 0)),
        _resident(gmat.shape, lambda b, i: (0, 0)),
    ]
    args = [h, h, h, mods, norm_g, w_in, qk_gain, conv_w, gmat]
    if rope:
        in_specs += [pl.BlockSpec((tm, V7X_LANES), lambda b, i: (i, 0))] * 2
        args += list(rope_tabs)
    widths = (A_Q_W, B_W, A_KV_W, A_KV_W, B_W, B_W, C_WIDTH)
    out_specs = [pl.BlockSpec((None, tm, w), lambda b, i: (b, i, 0)) for w in widths]
    out_shape = [jax.ShapeDtypeStruct((bsz, s, w), BF16) for w in widths]
    vmem = d * pw * 2 + tm * d * 4 * 4 + tm * pw * 12
    return pl.pallas_call(
        functools.partial(_proj_kernel, rope=rope, tm=tm, n_tiles=n_tiles),
        grid=(bsz, n_tiles),
        in_specs=in_specs,
        out_specs=out_specs,
        out_shape=out_shape,
        compiler_params=_params(2, vmem),
        name="mix_proj",
    )(*args)


def _gqa_kernel(*refs, hq, hkv, tq, tk, n_chunks, extra_len):
    if extra_len:
        q_ref, k_ref, v_ref, kx_ref, vx_ref, o_ref, qs_sc, m_sc, l_sc, acc_sc, on_sc = refs
    else:
        q_ref, k_ref, v_ref, o_ref, qs_sc, m_sc, l_sc, acc_sc, on_sc = refs
    grp = hq // hkv
    lane = lax.broadcasted_iota(jnp.int32, (tq, V7X_LANES), 1)
    low_half = lane < HEAD_DIM
    nt_dims = (((1,), (1,)), ((), ()))

    for pc in range(hkv // 2):
        heads = [h for h in range(hq) if (h // grp) // 2 == pc]
        rows = len(heads) * tq
        for j, h in enumerate(heads):
            xc = q_ref[:, (h // 2) * V7X_LANES:(h // 2 + 1) * V7X_LANES].astype(F32)
            dst_low = (h // grp) % 2 == 0
            if (h % 2 == 0) != dst_low:
                xc = pltpu.roll(xc, HEAD_DIM, axis=1)
            keep = low_half if dst_low else jnp.logical_not(low_half)
            qs_sc[j * tq:(j + 1) * tq, :] = jnp.where(keep, xc, 0.0).astype(BF16)
        m_sc[0:rows, :] = jnp.full((rows, V7X_LANES), NEG, F32)
        l_sc[0:rows, :] = jnp.zeros((rows, V7X_LANES), F32)
        acc_sc[0:rows, :] = jnp.zeros((rows, V7X_LANES), F32)

        def step(kc, vc, width):
            s = lax.dot_general(qs_sc[0:rows, :], kc, nt_dims, preferred_element_type=F32)
            m_prev = m_sc[0:rows, :]
            m_next = jnp.maximum(m_prev, jnp.max(s, axis=1, keepdims=True))
            alpha = jnp.exp(m_prev - m_next)
            p = jnp.exp(s - jnp.concatenate([m_next] * (width // V7X_LANES), axis=1))
            l_sc[0:rows, :] = alpha * l_sc[0:rows, :] + jnp.sum(p, axis=1, keepdims=True)
            acc_sc[0:rows, :] = alpha * acc_sc[0:rows, :] + jnp.dot(
                p.astype(BF16), vc, preferred_element_type=F32)
            m_sc[0:rows, :] = m_next

        col = slice(pc * V7X_LANES, (pc + 1) * V7X_LANES)

        def body(c, carry):
            start = pl.multiple_of(c * tk, tk)
            step(k_ref[pl.ds(start, tk), col], v_ref[pl.ds(start, tk), col], tk)
            return carry

        lax.fori_loop(0, n_chunks, body, 0)
        if extra_len:
            step(kx_ref[:, col], vx_ref[:, col], extra_len)

        on = acc_sc[0:rows, :] / l_sc[0:rows, :]
        for j, h in enumerate(heads):
            on_sc[h * tq:(h + 1) * tq, :] = on[j * tq:(j + 1) * tq, :]

    for oc in range(hq // 2):
        pieces = []
        for e in range(2):
            h = 2 * oc + e
            piece = on_sc[h * tq:(h + 1) * tq, :]
            src_low = (h // grp) % 2 == 0
            if src_low != (e == 0):
                piece = pltpu.roll(piece, HEAD_DIM, axis=1)
            pieces.append(piece)
        o_ref[:, oc * V7X_LANES:(oc + 1) * V7X_LANES] = jnp.where(low_half, pieces[0], pieces[1]).astype(BF16)


def _gqa(q, k, v, extra, *, hq, hkv, tq, tk):
    bsz, s, qw = q.shape
    t = k.shape[1]
    kw = k.shape[2]
    n_chunks = t // tk
    grp = hq // hkv
    max_heads = max(sum(1 for h in range(hq) if (h // grp) // 2 == pc) for pc in range(hkv // 2))
    rows = max_heads * tq
    in_specs = [
        pl.BlockSpec((None, tq, qw), lambda b, i: (b, i, 0)),
        pl.BlockSpec((None, t, kw), lambda b, i: (b, 0, 0)),
        pl.BlockSpec((None, t, kw), lambda b, i: (b, 0, 0)),
    ]
    args = [q, k, v]
    extra_len = 0
    if extra is not None:
        extra_len = extra[0].shape[1]
        in_specs += [pl.BlockSpec((None, extra_len, kw), lambda b, i: (b, 0, 0))] * 2
        args += list(extra)
    vmem = 4 * t * kw * 2 + rows * max(tk, extra_len) * 16 + rows * V7X_LANES * 24 + 8 * tq * qw * 2
    return pl.pallas_call(
        functools.partial(_gqa_kernel, hq=hq, hkv=hkv, tq=tq, tk=tk, n_chunks=n_chunks, extra_len=extra_len),
        grid=(bsz, s // tq),
        in_specs=in_specs,
        out_specs=pl.BlockSpec((None, tq, qw), lambda b, i: (b, i, 0)),
        out_shape=jax.ShapeDtypeStruct(q.shape, BF16),
        scratch_shapes=[
            pltpu.VMEM((rows, V7X_LANES), BF16),
            pltpu.VMEM((rows, V7X_LANES), F32),
            pltpu.VMEM((rows, V7X_LANES), F32),
            pltpu.VMEM((rows, V7X_LANES), F32),
            pltpu.VMEM((hq * tq, V7X_LANES), F32),
        ],
        compiler_params=_params(2, vmem),
        name="gqa_attn",
    )(*args)


def _nbr_kernel(q_ref, k_ref, v_ref, kx_ref, vx_ref, bias_ref, o_ref, *, n_rblocks, rows_total):
    rb = pl.program_id(2)
    tq = q_ref.shape[0]
    n_win = NBR_KROWS * GRID_W
    krow0 = jnp.clip(rb * NBR_QROWS - WIN_R // 2, 0, rows_total - NBR_KROWS)
    start = pl.multiple_of(krow0 * GRID_W, NBR_QROWS * GRID_W)
    kw = k_ref[pl.ds(start, n_win), :]
    vw = v_ref[pl.ds(start, n_win), :]
    kx = kx_ref[...]
    vx = vx_ref[...]
    q = q_ref[...]
    lane = lax.broadcasted_iota(jnp.int32, (tq, V7X_LANES), 1)
    low_half = lane < HEAD_DIM
    nt_dims = (((1,), (1,)), ((), ()))
    outs = []
    for e in range(2):
        keep = low_half if e == 0 else jnp.logical_not(low_half)
        qe = jnp.where(keep, q, jnp.zeros_like(q))
        s_win = lax.dot_general(qe, kw, nt_dims, preferred_element_type=F32) + bias_ref[e]
        s_ctx = lax.dot_general(qe, kx, nt_dims, preferred_element_type=F32)
        m = jnp.maximum(jnp.max(s_win, axis=1, keepdims=True), jnp.max(s_ctx, axis=1, keepdims=True))
        p_win = jnp.exp(s_win - m)
        p_ctx = jnp.exp(s_ctx - m)
        l = jnp.sum(p_win, axis=1, keepdims=True) + jnp.sum(p_ctx, axis=1, keepdims=True)
        o = (jnp.dot(p_win.astype(BF16), vw, preferred_element_type=F32)
             + jnp.dot(p_ctx.astype(BF16), vx, preferred_element_type=F32))
        outs.append(o / l)
    o_ref[...] = jnp.where(low_half, outs[0], outs[1]).astype(BF16)


def _nbr(q, k, v, kx, vx, bias):
    bsz, s, w = q.shape
    rows_total = s // GRID_W
    n_rblocks = rows_total // NBR_QROWS
    tq = NBR_QROWS * GRID_W
    n_win = NBR_KROWS * GRID_W
    lx = kx.shape[1]
    n_pairs = w // V7X_LANES

    def bias_map(b, pr, rb):
        kind = jnp.where(rb == 0, 0, jnp.where(rb == n_rblocks - 1, 2, 1))
        return (kind, pr, 0, 0)

    vmem = 4 * s * V7X_LANES * 2 + 2 * 2 * tq * n_win * 4 + tq * (n_win + lx) * 24
    return pl.pallas_call(
        functools.partial(_nbr_kernel, n_rblocks=n_rblocks, rows_total=rows_total),
        grid=(bsz, n_pairs, n_rblocks),
        in_specs=[
            pl.BlockSpec((None, tq, V7X_LANES), lambda b, pr, rb: (b, rb, pr)),
            pl.BlockSpec((None, s, V7X_LANES), lambda b, pr, rb: (b, 0, pr)),
            pl.BlockSpec((None, s, V7X_LANES), lambda b, pr, rb: (b, 0, pr)),
            pl.BlockSpec((None, lx, V7X_LANES), lambda b, pr, rb: (b, 0, pr)),
            pl.BlockSpec((None, lx, V7X_LANES), lambda b, pr, rb: (b, 0, pr)),
            pl.BlockSpec((None, 2, tq, n_win), bias_map),
        ],
        out_specs=pl.BlockSpec((None, tq, V7X_LANES), lambda b, pr, rb: (b, rb, pr)),
        out_shape=jax.ShapeDtypeStruct(q.shape, BF16),
        compiler_params=_params(3, vmem),
        name="nbr_attn",
    )(q, k, v, kx, vx, bias)


def _nbr_bias_tables(rpb, rows_total):
    assert rows_total % NBR_QROWS == 0 and rows_total >= NBR_KROWS + NBR_QROWS
    wr = min(WIN_R, rows_total)
    kinds = [(0, 0), (2 * NBR_QROWS, 2 * NBR_QROWS - WIN_R // 2),
             (rows_total - NBR_QROWS, rows_total - NBR_KROWS)]
    qi = np.arange(NBR_QROWS)[:, None, None, None]
    qc = np.arange(GRID_W)[None, :, None, None]
    kj = np.arange(NBR_KROWS)[None, None, :, None]
    kc = np.arange(GRID_W)[None, None, None, :]
    dr_all, dc_all, ok_all = [], [], []
    for r0, k0 in kinds:
        r = r0 + qi
        rs = np.clip(r - wr // 2, 0, rows_total - wr)
        cs = np.clip(qc - WIN_C // 2, 0, GRID_W - WIN_C)
        kr = k0 + kj
        ok = (kr >= rs) & (kr < rs + wr) & (kc >= cs) & (kc < cs + WIN_C)
        dr = np.broadcast_to(kr - r + (WIN_R - 1), ok.shape)
        dc = np.broadcast_to(kc - qc + (WIN_C - 1), ok.shape)
        n_q, n_k = NBR_QROWS * GRID_W, NBR_KROWS * GRID_W
        ok_all.append(ok.reshape(n_q, n_k))
        dr_all.append(np.where(ok, dr, 0).reshape(n_q, n_k))
        dc_all.append(np.where(ok, dc, 0).reshape(n_q, n_k))
    dr_idx = np.stack(dr_all)
    dc_idx = np.stack(dc_all)
    ok = np.stack(ok_all)
    tab = rpb[:, dr_idx, dc_idx]
    tab = jnp.where(ok[None], tab, NEG)
    return jnp.transpose(tab, (1, 0, 2, 3)).astype(F32)


def _oproj_kernel(h_ref, oa_ref, ob_ref, oc_ref, mod_ref, g_ref, w_ref, o_ref):
    wa = oa_ref.shape[1]
    wb = ob_ref.shape[1]
    y = (jnp.dot(oa_ref[...], w_ref[0:wa, :], preferred_element_type=F32)
         + jnp.dot(ob_ref[...], w_ref[wa:wa + wb, :], preferred_element_type=F32)
         + jnp.dot(oc_ref[...], w_ref[wa + wb:, :], preferred_element_type=F32))
    o_ref[...] = h_ref[...] + mod_ref[5:6, :] * _rms(y, g_ref[3:4, :])


def _oproj(h, oa, ob, oc, mods, norm_g, w_o, *, layer, mod_row, tm):
    bsz, s, d = h.shape
    if mod_row is None:
        mod_map = lambda b, i: (layer, b, 0, 0)
    else:
        mod_map = lambda b, i: (layer, mod_row, 0, 0)
    tile = lambda w: pl.BlockSpec((None, tm, w), lambda b, i: (b, i, 0))
    vmem = w_o.shape[1] * d * 2 + tm * d * 4 * 8
    return pl.pallas_call(
        _oproj_kernel,
        grid=(bsz, s // tm),
        in_specs=[
            tile(d), tile(oa.shape[2]), tile(ob.shape[2]), tile(oc.shape[2]),
            pl.BlockSpec((None, None, N_MOD, d), mod_map),
            _resident((None, norm_g.shape[1], d), lambda b, i: (layer, 0, 0)),
            _resident((None, w_o.shape[1], d), lambda b, i: (layer, 0, 0)),
        ],
        out_specs=tile(d),
        out_shape=jax.ShapeDtypeStruct(h.shape, F32),
        compiler_params=_params(2, vmem),
        name="mix_out",
    )(h, oa, ob, oc, mods, norm_g, w_o)


def _rope_tables(seq):
    pos = jnp.arange(seq, dtype=jnp.int32)
    row = (pos // GRID_W).astype(F32)
    col = (pos % GRID_W).astype(F32)
    freq = 1.0 / (ROPE_THETA ** (jnp.arange(ROPE_FREQS, dtype=F32) / ROPE_FREQS))
    ar = row[:, None] * freq
    ac = col[:, None] * freq
    cos = jnp.concatenate([jnp.cos(ar), jnp.cos(ar), jnp.cos(ac), jnp.cos(ac)], axis=1)
    sin = jnp.concatenate([-jnp.sin(ar), jnp.sin(ar), -jnp.sin(ac), jnp.sin(ac)], axis=1)
    reps = V7X_LANES // HEAD_DIM
    return jnp.tile(cos, (1, reps)), jnp.tile(sin, (1, reps))


def _tile_rows(n, target):
    t = min(n, target)
    assert n % t == 0
    return t


def kernel(x, c, ctx, c_ctx, w_ada, b_ada, norm_g, w_in, qk_g, rpb, conv_w, w_o, ffn_wi, ffn_wo):
    bsz, seq, d = x.shape
    ctx_len = ctx.shape[1]
    depth = w_ada.shape[0]
    ctx_row = bsz

    mod_rows = -(-(bsz + 1) // 8) * 8
    c_all = jnp.zeros((mod_rows, d), F32).at[:bsz].set(c).at[ctx_row].set(c_ctx)
    mods = _ada(c_all, w_ada, b_ada).reshape(depth, mod_rows, N_MOD, d)

    w_in_b = w_in.astype(BF16)
    w_o_b = w_o.astype(BF16)
    wi_b = ffn_wi.astype(BF16)
    wo_b = ffn_wo.astype(BF16)
    qk_gain = jnp.concatenate(
        [jnp.tile(qk_g[:, 0], (1, A_Q_HEADS)), jnp.tile(qk_g[:, 1], (1, A_KV_HEADS))], axis=1
    ).reshape(depth, 1, A_Q_W + A_KV_W)
    gsz = (A_Q_W + A_KV_W) // 2
    head_of = np.arange(gsz) // HEAD_DIM
    gmat = jnp.asarray((head_of[:, None] == head_of[None, :]) / HEAD_DIM, dtype=BF16)
    rope_tabs = _rope_tables(seq)

    tm_ffn = _tile_rows(seq, 256)
    tm_proj = _tile_rows(seq, 512)
    tm_ctx = _tile_rows(ctx_len, 256)
    tq_a = _tile_rows(seq, 128)
    tk_a = _tile_rows(seq, 512)

    h, hc = x, ctx
    for layer in range(depth):
        last = layer == depth - 1
        lat = dict(layer=layer, mod_row=None)
        cx = dict(layer=layer, mod_row=ctx_row)
        h = _ffn(h, mods, norm_g, wi_b, wo_b, which=0, tm=tm_ffn, **lat)
        hc = _ffn(hc, mods, norm_g, wi_b, wo_b, which=0, tm=tm_ctx, **cx)
        qa, qb, ka, va, kb, vb, oc = _proj(h, mods, norm_g, w_in_b, qk_gain, conv_w, gmat, rope_tabs,
                                           tm=tm_proj, **lat)
        cqa, cqb, cka, cva, ckb, cvb, coc = _proj(hc, mods, norm_g, w_in_b, qk_gain, conv_w, gmat, None,
                                                  tm=tm_ctx, **cx)
        if not last:
            coa = _gqa(cqa, cka, cva, None, hq=A_Q_HEADS, hkv=A_KV_HEADS, tq=tm_ctx, tk=ctx_len)
            cob = _gqa(cqb, ckb, cvb, None, hq=B_HEADS, hkv=B_HEADS, tq=tm_ctx, tk=ctx_len)
            hc = _oproj(hc, coa, cob, coc, mods, norm_g, w_o_b, tm=tm_ctx, **cx)
        oa = _gqa(qa, ka, va, (cka, cva), hq=A_Q_HEADS, hkv=A_KV_HEADS, tq=tq_a, tk=tk_a)
        ob = _nbr(qb, kb, vb, ckb, cvb, _nbr_bias_tables(rpb[layer], seq // GRID_W))
        h = _oproj(h, oa, ob, oc, mods, norm_g, w_o_b, tm=tm_proj, **lat)
        h = _ffn(h, mods, norm_g, wi_b, wo_b, which=1, tm=tm_ffn, **lat)
        if not last:
            hc = _ffn(hc, mods, norm_g, wi_b, wo_b, which=1, tm=tm_ctx, **cx)
    return h
```

```python
import functools

import numpy as np
import jax
import jax.numpy as jnp
from jax import lax
from jax.experimental import pallas as pl
from jax.experimental.pallas import tpu as pltpu

F32 = jnp.float32
BF16 = jnp.bfloat16

HEAD_DIM = 64
GRID_W = 64
A_Q_HEADS = 6
A_KV_HEADS = 2
B_HEADS = 6
C_WIDTH = 256
A_Q_W = A_Q_HEADS * HEAD_DIM
A_KV_W = A_KV_HEADS * HEAD_DIM
B_W = B_HEADS * HEAD_DIM
WIN_R = 8
WIN_C = 16
ROPE_FREQS = HEAD_DIM // 4
ROPE_THETA = 10000.0
N_MOD = 9
EPS = 1e-6
NEG = -1e30
LOG2_E = 1.4426950408889634

V7X_LANES = 128
V7X_SCOPED_VMEM_BYTES = 60000 * 1024

NBR_QROWS = 4
NBR_KROWS = NBR_QROWS + WIN_R


def _vmem_limit(estimate_bytes):
    return int(min(max(estimate_bytes, 16 * 1024 * 1024), V7X_SCOPED_VMEM_BYTES))


def _params(n_axes, vmem_bytes):
    return pltpu.CompilerParams(
        dimension_semantics=("arbitrary",) * n_axes,
        vmem_limit_bytes=_vmem_limit(vmem_bytes),
    )


def _rms(x, g):
    ms = jnp.mean(x * x, axis=-1, keepdims=True)
    return x * lax.rsqrt(ms + EPS) * g


def _resident(block_shape, index_map):
    return pl.BlockSpec(block_shape, index_map, pipeline_mode=pl.Buffered(1))


def _ada_kernel(c_ref, w_ref, b_ref, o_ref):
    c = c_ref[...]
    sc = c * jax.nn.sigmoid(c)
    o_ref[...] = jnp.dot(sc, w_ref[...], preferred_element_type=F32,
                         precision=lax.Precision.HIGHEST) + b_ref[...]


def _ada(c_all, w_ada, b_ada):
    depth, d, n = w_ada.shape
    rows = c_all.shape[0]
    tn = d
    return pl.pallas_call(
        _ada_kernel,
        grid=(depth, n // tn),
        in_specs=[
            pl.BlockSpec((rows, d), lambda l, j: (0, 0)),
            pl.BlockSpec((None, d, tn), lambda l, j: (l, 0, j)),
            pl.BlockSpec((None, 1, tn), lambda l, j: (l, 0, j)),
        ],
        out_specs=pl.BlockSpec((None, rows, tn), lambda l, j: (l, 0, j)),
        out_shape=jax.ShapeDtypeStruct((depth, rows, n), F32),
        compiler_params=_params(2, 4 * d * tn * 4),
        name="ada_mod",
    )(c_all, w_ada, b_ada.reshape(depth, 1, n))


def _ffn_kernel(h_ref, mod_ref, g_ref, wi_ref, wo_ref, o_ref, *, i0, gi, ffn_dim):
    h = h_ref[...]
    shift = mod_ref[i0:i0 + 1, :]
    scale = mod_ref[i0 + 1:i0 + 2, :]
    gate = mod_ref[i0 + 2:i0 + 3, :]
    u = _rms(h, g_ref[gi:gi + 1, :]) * (1.0 + scale) + shift
    hid = jnp.dot(u.astype(BF16), wi_ref[...], preferred_element_type=F32)
    gt = hid[:, :ffn_dim]
    up = hid[:, ffn_dim:]
    act = (gt * jax.nn.sigmoid(gt) * up).astype(BF16)
    y = jnp.dot(act, wo_ref[...], preferred_element_type=F32)
    o_ref[...] = h + 0.5 * gate * _rms(y, g_ref[gi + 1:gi + 2, :])


def _ffn(h, mods, norm_g, wi, wo, *, layer, which, mod_row, tm):
    bsz, s, d = h.shape
    ffn_dim = wo.shape[2]
    i0 = 6 * which
    gi = 4 * which
    if mod_row is None:
        mod_map = lambda b, i: (layer, b, 0, 0)
    else:
        mod_map = lambda b, i: (layer, mod_row, 0, 0)
    vmem = (wi.shape[2] * wi.shape[3] + wo.shape[2] * wo.shape[3]) * 2 + tm * d * 4 * 6 + tm * ffn_dim * 16
    return pl.pallas_call(
        functools.partial(_ffn_kernel, i0=i0, gi=gi, ffn_dim=ffn_dim),
        grid=(bsz, s // tm),
        in_specs=[
            pl.BlockSpec((None, tm, d), lambda b, i: (b, i, 0)),
            pl.BlockSpec((None, None, N_MOD, d), mod_map),
            _resident((None, norm_g.shape[1], d), lambda b, i: (layer, 0, 0)),
            _resident((None, None, d, 2 * ffn_dim), lambda b, i: (layer, which, 0, 0)),
            _resident((None, None, ffn_dim, d), lambda b, i: (layer, which, 0, 0)),
        ],
        out_specs=pl.BlockSpec((None, tm, d), lambda b, i: (b, i, 0)),
        out_shape=jax.ShapeDtypeStruct(h.shape, F32),
        compiler_params=_params(2, vmem),
        name="ffn",
    )(h, mods, norm_g, wi, wo)


def _swap_rope_partners(x):
    lane = lax.broadcasted_iota(jnp.int32, x.shape, 1)
    first = (lane % (2 * ROPE_FREQS)) < ROPE_FREQS
    return jnp.where(first,
                     pltpu.roll(x, V7X_LANES - ROPE_FREQS, axis=1),
                     pltpu.roll(x, ROPE_FREQS, axis=1))


def _proj_kernel(*refs, rope, tm, n_tiles):
    if rope:
        (h_ref, hp_ref, hn_ref, mod_ref, g_ref, w_ref, qkg_ref, cw_ref, gm_ref, cos_ref, sin_ref,
         qa_ref, qb_ref, ka_ref, va_ref, kb_ref, vb_ref, oc_ref) = refs
    else:
        (h_ref, hp_ref, hn_ref, mod_ref, g_ref, w_ref, qkg_ref, cw_ref, gm_ref,
         qa_ref, qb_ref, ka_ref, va_ref, kb_ref, vb_ref, oc_ref) = refs
    i = pl.program_id(1)
    shift = mod_ref[3:4, :]
    scale = mod_ref[4:5, :]
    g2 = g_ref[2:3, :]

    def pre(x):
        return (_rms(x, g2) * (1.0 + scale) + shift).astype(BF16)

    p = jnp.dot(pre(h_ref[...]), w_ref[...], preferred_element_type=F32)
    o_qa, o_qb = 0, A_Q_W
    o_ka = o_qb + B_W
    o_va = o_ka + A_KV_W
    o_kb = o_va + A_KV_W
    o_vb = o_kb + B_W
    o_cx = o_vb + B_W
    o_cb = o_cx + C_WIDTH
    o_cc = o_cb + C_WIDTH

    xq = jnp.concatenate([p[:, o_qa:o_qa + A_Q_W], p[:, o_ka:o_ka + A_KV_W]], axis=1)
    sq = (xq * xq).astype(BF16)
    half = (A_Q_W + A_KV_W) // 2
    ms = jnp.concatenate(
        [jnp.dot(sq[:, :half], gm_ref[...], preferred_element_type=F32),
         jnp.dot(sq[:, half:], gm_ref[...], preferred_element_type=F32)], axis=1)
    xn = xq * lax.rsqrt(ms + EPS) * qkg_ref[...]
    if rope:
        cos = cos_ref[...]
        sin = sin_ref[...]
        cols = []
        for j in range((A_Q_W + A_KV_W) // V7X_LANES):
            xc = xn[:, j * V7X_LANES:(j + 1) * V7X_LANES]
            cols.append(xc * cos + _swap_rope_partners(xc) * sin)
        xn = jnp.concatenate(cols, axis=1)
    q_scale = HEAD_DIM ** -0.5 * LOG2_E
    qa_ref[...] = (xn[:, :A_Q_W] * q_scale).astype(BF16)
    ka_ref[...] = xn[:, A_Q_W:].astype(BF16)
    qb_ref[...] = (p[:, o_qb:o_qb + B_W] * q_scale).astype(BF16)
    va_ref[...] = p[:, o_va:o_va + A_KV_W].astype(BF16)
    kb_ref[...] = p[:, o_kb:o_kb + B_W].astype(BF16)
    vb_ref[...] = p[:, o_vb:o_vb + B_W].astype(BF16)

    z = p[:, o_cc:o_cc + C_WIDTH] * p[:, o_cx:o_cx + C_WIDTH]

    def halo(ref):
        ub = pre(ref[...])
        return (jnp.dot(ub, w_ref[:, o_cc:o_cc + C_WIDTH], preferred_element_type=F32)
                * jnp.dot(ub, w_ref[:, o_cx:o_cx + C_WIDTH], preferred_element_type=F32))

    halo_rows = hp_ref.shape[0]
    z_before = jnp.where(i > 0, halo(hp_ref)[halo_rows - 1:halo_rows, :], 0.0)
    z_after = jnp.where(i < n_tiles - 1, halo(hn_ref)[0:1, :], 0.0)
    row = lax.broadcasted_iota(jnp.int32, z.shape, 0)
    z_m1 = jnp.where(row == 0, z_before, pltpu.roll(z, 1, axis=0))
    z_p1 = jnp.where(row == tm - 1, z_after, pltpu.roll(z, tm - 1, axis=0))
    y = cw_ref[0:1, :] * z_m1 + cw_ref[1:2, :] * z + cw_ref[2:3, :] * z_p1
    oc_ref[...] = (p[:, o_cb:o_cb + C_WIDTH] * y).astype(BF16)


def _proj(h, mods, norm_g, w_in, qk_gain, conv_w, gmat, rope_tabs, *, layer, mod_row, tm):
    bsz, s, d = h.shape
    n_tiles = s // tm
    halo_rows = 8
    hb = tm // halo_rows
    n_hblk = s // halo_rows
    rope = rope_tabs is not None
    if mod_row is None:
        mod_map = lambda b, i: (layer, b, 0, 0)
    else:
        mod_map = lambda b, i: (layer, mod_row, 0, 0)
    pw = w_in.shape[2]
    in_specs = [
        pl.BlockSpec((None, tm, d), lambda b, i: (b, i, 0)),
        pl.BlockSpec((None, halo_rows, d), lambda b, i: (b, jnp.maximum(i * hb - 1, 0), 0)),
        pl.BlockSpec((None, halo_rows, d), lambda b, i: (b, jnp.minimum((i + 1) * hb, n_hblk - 1), 0)),
        pl.BlockSpec((None, None, N_MOD, d), mod_map),
        _resident((None, norm_g.shape[1], d), lambda b, i: (layer, 0, 0)),
        _resident((None, d, pw), lambda b, i: (layer, 0, 0)),
        _resident((None, 1, A_Q_W + A_KV_W), lambda b, i: (layer, 0, 0)),
        _resident((None, conv_w.shape[1], C_WIDTH), lambda b, i: (layer, 0, 0)),
        _resident(gmat.shape, lambda b, i: (0, 0)),
    ]
    args = [h, h, h, mods, norm_g, w_in, qk_gain, conv_w, gmat]
    if rope:
        in_specs += [pl.BlockSpec((tm, V7X_LANES), lambda b, i: (i, 0))] * 2
        args += list(rope_tabs)
    widths = (A_Q_W, B_W, A_KV_W, A_KV_W, B_W, B_W, C_WIDTH)
    out_specs = [pl.BlockSpec((None, tm, w), lambda b, i: (b, i, 0)) for w in widths]
    out_shape = [jax.ShapeDtypeStruct((bsz, s, w), BF16) for w in widths]
    vmem = d * pw * 2 + tm * d * 4 * 4 + tm * pw * 12
    return pl.pallas_call(
        functools.partial(_proj_kernel, rope=rope, tm=tm, n_tiles=n_tiles),
        grid=(bsz, n_tiles),
        in_specs=in_specs,
        out_specs=out_specs,
        out_shape=out_shape,
        compiler_params=_params(2, vmem),
        name="mix_proj",
    )(*args)


def _gqa_kernel(*refs, hq, hkv, tq, tk, n_chunks, unroll, extra_len):
    if extra_len:
        q_ref, k_ref, v_ref, kx_ref, vx_ref, o_ref, qs_sc, m_sc, l_sc, acc_sc, on_sc = refs
    else:
        q_ref, k_ref, v_ref, o_ref, qs_sc, m_sc, l_sc, acc_sc, on_sc = refs
    grp = hq // hkv
    lane = lax.broadcasted_iota(jnp.int32, (tq, V7X_LANES), 1)
    low_half = lane < HEAD_DIM
    nt_dims = (((1,), (1,)), ((), ()))

    for pc in range(hkv // 2):
        heads = [h for h in range(hq) if (h // grp) // 2 == pc]
        rows = len(heads) * tq
        for j, h in enumerate(heads):
            xc = q_ref[:, (h // 2) * V7X_LANES:(h // 2 + 1) * V7X_LANES].astype(F32)
            dst_low = (h // grp) % 2 == 0
            if (h % 2 == 0) != dst_low:
                xc = pltpu.roll(xc, HEAD_DIM, axis=1)
            keep = low_half if dst_low else jnp.logical_not(low_half)
            qs_sc[j * tq:(j + 1) * tq, :] = jnp.where(keep, xc, 0.0).astype(BF16)
        m_sc[0:rows, :] = jnp.full((rows, V7X_LANES), NEG, F32)
        l_sc[0:rows, :] = jnp.zeros((rows, V7X_LANES), F32)
        acc_sc[0:rows, :] = jnp.zeros((rows, V7X_LANES), F32)

        def load_state():
            return m_sc[0:rows, :], l_sc[0:rows, :], acc_sc[0:rows, :]

        def store_state(state):
            m_sc[0:rows, :], l_sc[0:rows, :], acc_sc[0:rows, :] = state

        def step(state, kc, vc, width):
            m_prev, l_prev, acc_prev = state
            s = lax.dot_general(qs_sc[0:rows, :], kc, nt_dims, preferred_element_type=F32)
            m_next = jnp.maximum(m_prev, jnp.max(s, axis=1, keepdims=True))
            alpha = jnp.exp2(m_prev - m_next)
            p = jnp.exp2(s - jnp.concatenate([m_next] * (width // V7X_LANES), axis=1))
            l_next = alpha * l_prev + jnp.sum(p, axis=1, keepdims=True)
            acc_next = alpha * acc_prev + jnp.dot(p.astype(BF16), vc, preferred_element_type=F32)
            return m_next, l_next, acc_next

        col = slice(pc * V7X_LANES, (pc + 1) * V7X_LANES)

        def body(c, carry):
            state = load_state()
            for u in range(unroll):
                start = pl.multiple_of((c * unroll + u) * tk, tk)
                state = step(state, k_ref[pl.ds(start, tk), col], v_ref[pl.ds(start, tk), col], tk)
            store_state(state)
            return carry

        lax.fori_loop(0, n_chunks // unroll, body, 0)
        if extra_len:
            store_state(step(load_state(), kx_ref[:, col], vx_ref[:, col], extra_len))

        on = acc_sc[0:rows, :] / l_sc[0:rows, :]
        for j, h in enumerate(heads):
            on_sc[h * tq:(h + 1) * tq, :] = on[j * tq:(j + 1) * tq, :]

    for oc in range(hq // 2):
        pieces = []
        for e in range(2):
            h = 2 * oc + e
            piece = on_sc[h * tq:(h + 1) * tq, :]
            src_low = (h // grp) % 2 == 0
            if src_low != (e == 0):
                piece = pltpu.roll(piece, HEAD_DIM, axis=1)
            pieces.append(piece)
        o_ref[:, oc * V7X_LANES:(oc + 1) * V7X_LANES] = jnp.where(low_half, pieces[0], pieces[1]).astype(BF16)


def _gqa(q, k, v, extra, *, hq, hkv, tq, tk, unroll=1):
    bsz, s, qw = q.shape
    t = k.shape[1]
    kw = k.shape[2]
    n_chunks = t // tk
    assert n_chunks % unroll == 0
    grp = hq // hkv
    max_heads = max(sum(1 for h in range(hq) if (h // grp) // 2 == pc) for pc in range(hkv // 2))
    rows = max_heads * tq
    in_specs = [
        pl.BlockSpec((None, tq, qw), lambda b, i: (b, i, 0)),
        pl.BlockSpec((None, t, kw), lambda b, i: (b, 0, 0)),
        pl.BlockSpec((None, t, kw), lambda b, i: (b, 0, 0)),
    ]
    args = [q, k, v]
    extra_len = 0
    if extra is not None:
        extra_len = extra[0].shape[1]
        in_specs += [pl.BlockSpec((None, extra_len, kw), lambda b, i: (b, 0, 0))] * 2
        args += list(extra)
    vmem = 4 * t * kw * 2 + rows * max(tk, extra_len) * 16 + rows * V7X_LANES * 24 + 8 * tq * qw * 2
    return pl.pallas_call(
        functools.partial(_gqa_kernel, hq=hq, hkv=hkv, tq=tq, tk=tk, n_chunks=n_chunks, unroll=unroll,
                          extra_len=extra_len),
        grid=(bsz, s // tq),
        in_specs=in_specs,
        out_specs=pl.BlockSpec((None, tq, qw), lambda b, i: (b, i, 0)),
        out_shape=jax.ShapeDtypeStruct(q.shape, BF16),
        scratch_shapes=[
            pltpu.VMEM((rows, V7X_LANES), BF16),
            pltpu.VMEM((rows, V7X_LANES), F32),
            pltpu.VMEM((rows, V7X_LANES), F32),
            pltpu.VMEM((rows, V7X_LANES), F32),
            pltpu.VMEM((hq * tq, V7X_LANES), F32),
        ],
        compiler_params=_params(2, vmem),
        name="gqa_attn",
    )(*args)


def _nbr_kernel(q_ref, k_ref, v_ref, kx_ref, vx_ref, bias_ref, o_ref, *, n_rblocks, rows_total):
    rb = pl.program_id(2)
    tq = q_ref.shape[0]
    n_win = NBR_KROWS * GRID_W
    krow0 = jnp.clip(rb * NBR_QROWS - WIN_R // 2, 0, rows_total - NBR_KROWS)
    start = pl.multiple_of(krow0 * GRID_W, NBR_QROWS * GRID_W)
    kw = k_ref[pl.ds(start, n_win), :]
    vw = v_ref[pl.ds(start, n_win), :]
    kx = kx_ref[...]
    vx = vx_ref[...]
    q = q_ref[...]
    lane = lax.broadcasted_iota(jnp.int32, (tq, V7X_LANES), 1)
    low_half = lane < HEAD_DIM
    nt_dims = (((1,), (1,)), ((), ()))
    outs = []
    for e in range(2):
        keep = low_half if e == 0 else jnp.logical_not(low_half)
        qe = jnp.where(keep, q, jnp.zeros_like(q))
        s_win = lax.dot_general(qe, kw, nt_dims, preferred_element_type=F32) + bias_ref[e]
        s_ctx = lax.dot_general(qe, kx, nt_dims, preferred_element_type=F32)
        m = jnp.maximum(jnp.max(s_win, axis=1, keepdims=True), jnp.max(s_ctx, axis=1, keepdims=True))
        p_win = jnp.exp2(s_win - m)
        p_ctx = jnp.exp2(s_ctx - m)
        l = jnp.sum(p_win, axis=1, keepdims=True) + jnp.sum(p_ctx, axis=1, keepdims=True)
        o = (jnp.dot(p_win.astype(BF16), vw, preferred_element_type=F32)
             + jnp.dot(p_ctx.astype(BF16), vx, preferred_element_type=F32))
        outs.append(o / l)
    o_ref[...] = jnp.where(low_half, outs[0], outs[1]).astype(BF16)


def _nbr(q, k, v, kx, vx, bias):
    bsz, s, w = q.shape
    rows_total = s // GRID_W
    n_rblocks = rows_total // NBR_QROWS
    tq = NBR_QROWS * GRID_W
    n_win = NBR_KROWS * GRID_W
    lx = kx.shape[1]
    n_pairs = w // V7X_LANES

    def bias_map(b, pr, rb):
        kind = jnp.where(rb == 0, 0, jnp.where(rb == n_rblocks - 1, 2, 1))
        return (kind, pr, 0, 0)

    vmem = 4 * s * V7X_LANES * 2 + 2 * 2 * tq * n_win * 4 + tq * (n_win + lx) * 24
    return pl.pallas_call(
        functools.partial(_nbr_kernel, n_rblocks=n_rblocks, rows_total=rows_total),
        grid=(bsz, n_pairs, n_rblocks),
        in_specs=[
            pl.BlockSpec((None, tq, V7X_LANES), lambda b, pr, rb: (b, rb, pr)),
            pl.BlockSpec((None, s, V7X_LANES), lambda b, pr, rb: (b, 0, pr)),
            pl.BlockSpec((None, s, V7X_LANES), lambda b, pr, rb: (b, 0, pr)),
            pl.BlockSpec((None, lx, V7X_LANES), lambda b, pr, rb: (b, 0, pr)),
            pl.BlockSpec((None, lx, V7X_LANES), lambda b, pr, rb: (b, 0, pr)),
            pl.BlockSpec((None, 2, tq, n_win), bias_map),
        ],
        out_specs=pl.BlockSpec((None, tq, V7X_LANES), lambda b, pr, rb: (b, rb, pr)),
        out_shape=jax.ShapeDtypeStruct(q.shape, BF16),
        compiler_params=_params(3, vmem),
        name="nbr_attn",
    )(q, k, v, kx, vx, bias)


def _nbr_bias_tables(rpb, rows_total):
    assert rows_total % NBR_QROWS == 0 and rows_total >= NBR_KROWS + NBR_QROWS
    wr = min(WIN_R, rows_total)
    kinds = [(0, 0), (2 * NBR_QROWS, 2 * NBR_QROWS - WIN_R // 2),
             (rows_total - NBR_QROWS, rows_total - NBR_KROWS)]
    qi = np.arange(NBR_QROWS)[:, None, None, None]
    qc = np.arange(GRID_W)[None, :, None, None]
    kj = np.arange(NBR_KROWS)[None, None, :, None]
    kc = np.arange(GRID_W)[None, None, None, :]
    n_dr, n_dc = rpb.shape[1], rpb.shape[2]
    cs = np.clip(qc - WIN_C // 2, 0, GRID_W - WIN_C)
    col_ok = (kc >= cs) & (kc < cs + WIN_C)
    col_sel = (kc - qc + (WIN_C - 1))[..., None] == np.arange(n_dc)
    col_sel = (col_sel & col_ok[..., None])[0, :, 0].astype(np.float32)
    row_sel, ok_all = [], []
    for r0, k0 in kinds:
        r = r0 + qi
        rs = np.clip(r - wr // 2, 0, rows_total - wr)
        kr = k0 + kj
        row_ok = (kr >= rs) & (kr < rs + wr)
        sel = ((kr - r + (WIN_R - 1))[..., None] == np.arange(n_dr)) & row_ok[..., None]
        row_sel.append(sel[:, 0, :, 0].astype(np.float32))
        ok_all.append(np.broadcast_to(row_ok & col_ok, (NBR_QROWS, GRID_W, NBR_KROWS, GRID_W)))
    row_sel = jnp.asarray(np.stack(row_sel))
    ok = np.stack(ok_all).reshape(3, 1, NBR_QROWS * GRID_W, NBR_KROWS * GRID_W)
    hi = lax.Precision.HIGHEST
    rows_picked = jnp.einsum('hrd,tijr->thijd', rpb, row_sel, precision=hi)
    tab = jnp.einsum('thijd,cnd->thicjn', rows_picked, jnp.asarray(col_sel), precision=hi)
    tab = tab.reshape(3, rpb.shape[0], NBR_QROWS * GRID_W, NBR_KROWS * GRID_W)
    return jnp.where(ok, tab * LOG2_E, NEG).astype(F32)


def _oproj_kernel(h_ref, oa_ref, ob_ref, oc_ref, mod_ref, g_ref, w_ref, o_ref):
    wa = oa_ref.shape[1]
    wb = ob_ref.shape[1]
    y = (jnp.dot(oa_ref[...], w_ref[0:wa, :], preferred_element_type=F32)
         + jnp.dot(ob_ref[...], w_ref[wa:wa + wb, :], preferred_element_type=F32)
         + jnp.dot(oc_ref[...], w_ref[wa + wb:, :], preferred_element_type=F32))
    o_ref[...] = h_ref[...] + mod_ref[5:6, :] * _rms(y, g_ref[3:4, :])


def _oproj(h, oa, ob, oc, mods, norm_g, w_o, *, layer, mod_row, tm):
    bsz, s, d = h.shape
    if mod_row is None:
        mod_map = lambda b, i: (layer, b, 0, 0)
    else:
        mod_map = lambda b, i: (layer, mod_row, 0, 0)
    tile = lambda w: pl.BlockSpec((None, tm, w), lambda b, i: (b, i, 0))
    vmem = w_o.shape[1] * d * 2 + tm * d * 4 * 8
    return pl.pallas_call(
        _oproj_kernel,
        grid=(bsz, s // tm),
        in_specs=[
            tile(d), tile(oa.shape[2]), tile(ob.shape[2]), tile(oc.shape[2]),
            pl.BlockSpec((None, None, N_MOD, d), mod_map),
            _resident((None, norm_g.shape[1], d), lambda b, i: (layer, 0, 0)),
            _resident((None, w_o.shape[1], d), lambda b, i: (layer, 0, 0)),
        ],
        out_specs=tile(d),
        out_shape=jax.ShapeDtypeStruct(h.shape, F32),
        compiler_params=_params(2, vmem),
        name="mix_out",
    )(h, oa, ob, oc, mods, norm_g, w_o)


def _rope_tables(seq):
    pos = jnp.arange(seq, dtype=jnp.int32)
    row = (pos // GRID_W).astype(F32)
    col = (pos % GRID_W).astype(F32)
    freq = 1.0 / (ROPE_THETA ** (jnp.arange(ROPE_FREQS, dtype=F32) / ROPE_FREQS))
    ar = row[:, None] * freq
    ac = col[:, None] * freq
    cos = jnp.concatenate([jnp.cos(ar), jnp.cos(ar), jnp.cos(ac), jnp.cos(ac)], axis=1)
    sin = jnp.concatenate([-jnp.sin(ar), jnp.sin(ar), -jnp.sin(ac), jnp.sin(ac)], axis=1)
    reps = V7X_LANES // HEAD_DIM
    return jnp.tile(cos, (1, reps)), jnp.tile(sin, (1, reps))


def _tile_rows(n, target):
    t = min(n, target)
    assert n % t == 0
    return t


def kernel(x, c, ctx, c_ctx, w_ada, b_ada, norm_g, w_in, qk_g, rpb, conv_w, w_o, ffn_wi, ffn_wo):
    bsz, seq, d = x.shape
    ctx_len = ctx.shape[1]
    depth = w_ada.shape[0]
    ctx_row = bsz

    mod_rows = -(-(bsz + 1) // 8) * 8
    c_all = jnp.zeros((mod_rows, d), F32).at[:bsz].set(c).at[ctx_row].set(c_ctx)
    mods = _ada(c_all, w_ada, b_ada).reshape(depth, mod_rows, N_MOD, d)

    w_in_b = w_in.astype(BF16)
    w_o_b = w_o.astype(BF16)
    wi_b = ffn_wi.astype(BF16)
    wo_b = ffn_wo.astype(BF16)
    qk_gain = jnp.concatenate(
        [jnp.tile(qk_g[:, 0], (1, A_Q_HEADS)), jnp.tile(qk_g[:, 1], (1, A_KV_HEADS))], axis=1
    ).reshape(depth, 1, A_Q_W + A_KV_W)
    gsz = (A_Q_W + A_KV_W) // 2
    head_of = np.arange(gsz) // HEAD_DIM
    gmat = jnp.asarray((head_of[:, None] == head_of[None, :]) / HEAD_DIM, dtype=BF16)
    rope_tabs = _rope_tables(seq)

    tm_ffn = _tile_rows(seq, 256)
    tm_proj = _tile_rows(seq, 512)
    tm_ctx = _tile_rows(ctx_len, 256)
    tq_a = _tile_rows(seq, 128)
    tk_a = _tile_rows(seq, 512)

    h, hc = x, ctx
    for layer in range(depth):
        last = layer == depth - 1
        lat = dict(layer=layer, mod_row=None)
        cx = dict(layer=layer, mod_row=ctx_row)
        h = _ffn(h, mods, norm_g, wi_b, wo_b, which=0, tm=tm_ffn, **lat)
        hc = _ffn(hc, mods, norm_g, wi_b, wo_b, which=0, tm=tm_ctx, **cx)
        qa, qb, ka, va, kb, vb, oc = _proj(h, mods, norm_g, w_in_b, qk_gain, conv_w, gmat, rope_tabs,
                                           tm=tm_proj, **lat)
        cqa, cqb, cka, cva, ckb, cvb, coc = _proj(hc, mods, norm_g, w_in_b, qk_gain, conv_w, gmat, None,
                                                  tm=tm_ctx, **cx)
        if not last:
            coa = _gqa(cqa, cka, cva, None, hq=A_Q_HEADS, hkv=A_KV_HEADS, tq=tm_ctx, tk=ctx_len)
            cob = _gqa(cqb, ckb, cvb, None, hq=B_HEADS, hkv=B_HEADS, tq=tm_ctx, tk=ctx_len)
            hc = _oproj(hc, coa, cob, coc, mods, norm_g, w_o_b, tm=tm_ctx, **cx)
        oa = _gqa(qa, ka, va, (cka, cva), hq=A_Q_HEADS, hkv=A_KV_HEADS, tq=tq_a, tk=tk_a, unroll=2)
        ob = _nbr(qb, kb, vb, ckb, cvb, _nbr_bias_tables(rpb[layer], seq // GRID_W))
        h = _oproj(h, oa, ob, oc, mods, norm_g, w_o_b, tm=tm_proj, **lat)
        h = _ffn(h, mods, norm_g, wi_b, wo_b, which=1, tm=tm_ffn, **lat)
        if not last:
            hc = _ffn(hc, mods, norm_g, wi_b, wo_b, which=1, tm=tm_ctx, **cx)
    return h
```

```python
import functools

import numpy as np
import jax
import jax.numpy as jnp
from jax import lax
from jax.experimental import pallas as pl
from jax.experimental.pallas import tpu as pltpu

F32 = jnp.float32
BF16 = jnp.bfloat16

HEAD_DIM = 64
GRID_W = 64
A_Q_HEADS = 6
A_KV_HEADS = 2
B_HEADS = 6
C_WIDTH = 256
A_Q_W = A_Q_HEADS * HEAD_DIM
A_KV_W = A_KV_HEADS * HEAD_DIM
B_W = B_HEADS * HEAD_DIM
WIN_R = 8
WIN_C = 16
ROPE_FREQS = HEAD_DIM // 4
ROPE_THETA = 10000.0
N_MOD = 9
EPS = 1e-6
NEG = -1e30
LOG2_E = 1.4426950408889634

V7X_LANES = 128
V7X_SCOPED_VMEM_BYTES = 60000 * 1024

NBR_QROWS = 4
NBR_KROWS = NBR_QROWS + WIN_R

GQA_UNROLL = 4


def _vmem_limit(estimate_bytes):
    return int(min(max(estimate_bytes, 16 * 1024 * 1024), V7X_SCOPED_VMEM_BYTES))


def _params(n_axes, vmem_bytes):
    return pltpu.CompilerParams(
        dimension_semantics=("arbitrary",) * n_axes,
        vmem_limit_bytes=_vmem_limit(vmem_bytes),
    )


def _rms(x, g):
    ms = jnp.mean(x * x, axis=-1, keepdims=True)
    return x * lax.rsqrt(ms + EPS) * g


def _resident(block_shape, index_map):
    return pl.BlockSpec(block_shape, index_map, pipeline_mode=pl.Buffered(1))


def _ada_kernel(c_ref, w_ref, b_ref, o_ref):
    c = c_ref[...]
    sc = c * jax.nn.sigmoid(c)
    o_ref[...] = jnp.dot(sc, w_ref[...], preferred_element_type=F32,
                         precision=lax.Precision.HIGHEST) + b_ref[...]


def _ada(c_all, w_ada, b_ada):
    depth, d, n = w_ada.shape
    rows = c_all.shape[0]
    tn = d
    return pl.pallas_call(
        _ada_kernel,
        grid=(depth, n // tn),
        in_specs=[
            pl.BlockSpec((rows, d), lambda l, j: (0, 0)),
            pl.BlockSpec((None, d, tn), lambda l, j: (l, 0, j)),
            pl.BlockSpec((None, 1, tn), lambda l, j: (l, 0, j)),
        ],
        out_specs=pl.BlockSpec((None, rows, tn), lambda l, j: (l, 0, j)),
        out_shape=jax.ShapeDtypeStruct((depth, rows, n), F32),
        compiler_params=_params(2, 4 * d * tn * 4),
        name="ada_mod",
    )(c_all, w_ada, b_ada.reshape(depth, 1, n))


def _ffn_kernel(h_ref, mod_ref, g_ref, wi_ref, wo_ref, o_ref, *, i0, gi, ffn_dim):
    h = h_ref[...]
    shift = mod_ref[i0:i0 + 1, :]
    scale = mod_ref[i0 + 1:i0 + 2, :]
    gate = mod_ref[i0 + 2:i0 + 3, :]
    u = _rms(h, g_ref[gi:gi + 1, :]) * (1.0 + scale) + shift
    hid = jnp.dot(u.astype(BF16), wi_ref[...], preferred_element_type=F32)
    gt = hid[:, :ffn_dim]
    up = hid[:, ffn_dim:]
    act = (gt * jax.nn.sigmoid(gt) * up).astype(BF16)
    y = jnp.dot(act, wo_ref[...], preferred_element_type=F32)
    o_ref[...] = h + 0.5 * gate * _rms(y, g_ref[gi + 1:gi + 2, :])


def _ffn(h, mods, norm_g, wi, wo, *, layer, which, mod_row, tm):
    bsz, s, d = h.shape
    ffn_dim = wo.shape[2]
    i0 = 6 * which
    gi = 4 * which
    if mod_row is None:
        mod_map = lambda b, i: (layer, b, 0, 0)
    else:
        mod_map = lambda b, i: (layer, mod_row, 0, 0)
    vmem = (wi.shape[2] * wi.shape[3] + wo.shape[2] * wo.shape[3]) * 2 + tm * d * 4 * 6 + tm * ffn_dim * 16
    return pl.pallas_call(
        functools.partial(_ffn_kernel, i0=i0, gi=gi, ffn_dim=ffn_dim),
        grid=(bsz, s // tm),
        in_specs=[
            pl.BlockSpec((None, tm, d), lambda b, i: (b, i, 0)),
            pl.BlockSpec((None, None, N_MOD, d), mod_map),
            _resident((None, norm_g.shape[1], d), lambda b, i: (layer, 0, 0)),
            _resident((None, None, d, 2 * ffn_dim), lambda b, i: (layer, which, 0, 0)),
            _resident((None, None, ffn_dim, d), lambda b, i: (layer, which, 0, 0)),
        ],
        out_specs=pl.BlockSpec((None, tm, d), lambda b, i: (b, i, 0)),
        out_shape=jax.ShapeDtypeStruct(h.shape, F32),
        compiler_params=_params(2, vmem),
        name="ffn",
    )(h, mods, norm_g, wi, wo)


def _swap_rope_partners(x):
    lane = lax.broadcasted_iota(jnp.int32, x.shape, 1)
    first = (lane % (2 * ROPE_FREQS)) < ROPE_FREQS
    return jnp.where(first,
                     pltpu.roll(x, V7X_LANES - ROPE_FREQS, axis=1),
                     pltpu.roll(x, ROPE_FREQS, axis=1))


def _proj_kernel(*refs, rope, tm, n_tiles):
    if rope:
        (h_ref, hp_ref, hn_ref, mod_ref, g_ref, w_ref, qkg_ref, cw_ref, gm_ref, cos_ref, sin_ref,
         qa_ref, qb_ref, ka_ref, va_ref, kb_ref, vb_ref, oc_ref) = refs
    else:
        (h_ref, hp_ref, hn_ref, mod_ref, g_ref, w_ref, qkg_ref, cw_ref, gm_ref,
         qa_ref, qb_ref, ka_ref, va_ref, kb_ref, vb_ref, oc_ref) = refs
    i = pl.program_id(1)
    shift = mod_ref[3:4, :]
    scale = mod_ref[4:5, :]
    g2 = g_ref[2:3, :]

    def pre(x):
        return (_rms(x, g2) * (1.0 + scale) + shift).astype(BF16)

    p = jnp.dot(pre(h_ref[...]), w_ref[...], preferred_element_type=F32)
    o_qa, o_qb = 0, A_Q_W
    o_ka = o_qb + B_W
    o_va = o_ka + A_KV_W
    o_kb = o_va + A_KV_W
    o_vb = o_kb + B_W
    o_cx = o_vb + B_W
    o_cb = o_cx + C_WIDTH
    o_cc = o_cb + C_WIDTH

    xq = jnp.concatenate([p[:, o_qa:o_qa + A_Q_W], p[:, o_ka:o_ka + A_KV_W]], axis=1)
    sq = (xq * xq).astype(BF16)
    half = (A_Q_W + A_KV_W) // 2
    ms = jnp.concatenate(
        [jnp.dot(sq[:, :half], gm_ref[...], preferred_element_type=F32),
         jnp.dot(sq[:, half:], gm_ref[...], preferred_element_type=F32)], axis=1)
    xn = xq * lax.rsqrt(ms + EPS) * qkg_ref[...]
    if rope:
        cos = cos_ref[...]
        sin = sin_ref[...]
        cols = []
        for j in range((A_Q_W + A_KV_W) // V7X_LANES):
            xc = xn[:, j * V7X_LANES:(j + 1) * V7X_LANES]
            cols.append(xc * cos + _swap_rope_partners(xc) * sin)
        xn = jnp.concatenate(cols, axis=1)
    q_scale = HEAD_DIM ** -0.5 * LOG2_E
    qa_ref[...] = (xn[:, :A_Q_W] * q_scale).astype(BF16)
    ka_ref[...] = xn[:, A_Q_W:].astype(BF16)
    qb_ref[...] = (p[:, o_qb:o_qb + B_W] * q_scale).astype(BF16)
    va_ref[...] = p[:, o_va:o_va + A_KV_W].astype(BF16)
    kb_ref[...] = p[:, o_kb:o_kb + B_W].astype(BF16)
    vb_ref[...] = p[:, o_vb:o_vb + B_W].astype(BF16)

    z = p[:, o_cc:o_cc + C_WIDTH] * p[:, o_cx:o_cx + C_WIDTH]

    def halo(ref):
        ub = pre(ref[...])
        return (jnp.dot(ub, w_ref[:, o_cc:o_cc + C_WIDTH], preferred_element_type=F32)
                * jnp.dot(ub, w_ref[:, o_cx:o_cx + C_WIDTH], preferred_element_type=F32))

    halo_rows = hp_ref.shape[0]
    z_before = jnp.where(i > 0, halo(hp_ref)[halo_rows - 1:halo_rows, :], 0.0)
    z_after = jnp.where(i < n_tiles - 1, halo(hn_ref)[0:1, :], 0.0)
    row = lax.broadcasted_iota(jnp.int32, z.shape, 0)
    z_m1 = jnp.where(row == 0, z_before, pltpu.roll(z, 1, axis=0))
    z_p1 = jnp.where(row == tm - 1, z_after, pltpu.roll(z, tm - 1, axis=0))
    y = cw_ref[0:1, :] * z_m1 + cw_ref[1:2, :] * z + cw_ref[2:3, :] * z_p1
    oc_ref[...] = (p[:, o_cb:o_cb + C_WIDTH] * y).astype(BF16)


def _proj(h, mods, norm_g, w_in, qk_gain, conv_w, gmat, rope_tabs, *, layer, mod_row, tm):
    bsz, s, d = h.shape
    n_tiles = s // tm
    halo_rows = 8
    hb = tm // halo_rows
    n_hblk = s // halo_rows
    rope = rope_tabs is not None
    if mod_row is None:
        mod_map = lambda b, i: (layer, b, 0, 0)
    else:
        mod_map = lambda b, i: (layer, mod_row, 0, 0)
    pw = w_in.shape[2]
    in_specs = [
        pl.BlockSpec((None, tm, d), lambda b, i: (b, i, 0)),
        pl.BlockSpec((None, halo_rows, d), lambda b, i: (b, jnp.maximum(i * hb - 1, 0), 0)),
        pl.BlockSpec((None, halo_rows, d), lambda b, i: (b, jnp.minimum((i + 1) * hb, n_hblk - 1), 0)),
        pl.BlockSpec((None, None, N_MOD, d), mod_map),
        _resident((None, norm_g.shape[1], d), lambda b, i: (layer, 0, 0)),
        _resident((None, d, pw), lambda b, i: (layer, 0, 0)),
        _resident((None, 1, A_Q_W + A_KV_W), lambda b, i: (layer, 0, 0)),
        _resident((None, conv_w.shape[1], C_WIDTH), lambda b, i: (layer, 0, 0)),
        _resident(gmat.shape, lambda b, i: (0, 0)),
    ]
    args = [h, h, h, mods, norm_g, w_in, qk_gain, conv_w, gmat]
    if rope:
        in_specs += [pl.BlockSpec((tm, V7X_LANES), lambda b, i: (i, 0))] * 2
        args += list(rope_tabs)
    widths = (A_Q_W, B_W, A_KV_W, A_KV_W, B_W, B_W, C_WIDTH)
    out_specs = [pl.BlockSpec((None, tm, w), lambda b, i: (b, i, 0)) for w in widths]
    out_shape = [jax.ShapeDtypeStruct((bsz, s, w), BF16) for w in widths]
    vmem = d * pw * 2 + tm * d * 4 * 4 + tm * pw * 12
    return pl.pallas_call(
        functools.partial(_proj_kernel, rope=rope, tm=tm, n_tiles=n_tiles),
        grid=(bsz, n_tiles),
        in_specs=in_specs,
        out_specs=out_specs,
        out_shape=out_shape,
        compiler_params=_params(2, vmem),
        name="mix_proj",
    )(*args)


_NT_DIMS = (((1,), (1,)), ((), ()))

SCORE_BOUND_MAX = 40.0


def _load_q_rows(q_ref, qs_sc, heads, grp, tq):
    lane = lax.broadcasted_iota(jnp.int32, (tq, V7X_LANES), 1)
    low_half = lane < HEAD_DIM
    for j, h in enumerate(heads):
        xc = q_ref[:, (h // 2) * V7X_LANES:(h // 2 + 1) * V7X_LANES].astype(F32)
        dst_low = (h // grp) % 2 == 0
        if (h % 2 == 0) != dst_low:
            xc = pltpu.roll(xc, HEAD_DIM, axis=1)
        keep = low_half if dst_low else jnp.logical_not(low_half)
        qs_sc[j * tq:(j + 1) * tq, :] = jnp.where(keep, xc, 0.0).astype(BF16)


def _repack_heads(on_sc, o_ref, hq, grp, tq):
    lane = lax.broadcasted_iota(jnp.int32, (tq, V7X_LANES), 1)
    low_half = lane < HEAD_DIM
    for oc in range(hq // 2):
        pieces = []
        for e in range(2):
            h = 2 * oc + e
            piece = on_sc[h * tq:(h + 1) * tq, :]
            src_low = (h // grp) % 2 == 0
            if src_low != (e == 0):
                piece = pltpu.roll(piece, HEAD_DIM, axis=1)
            pieces.append(piece)
        o_ref[:, oc * V7X_LANES:(oc + 1) * V7X_LANES] = jnp.where(low_half, pieces[0], pieces[1]).astype(BF16)


def _online_softmax(qs_sc, m_sc, l_sc, acc_sc, rows, chunks, extra, *, tk, n_chunks, unroll):
    m_sc[0:rows, :] = jnp.full((rows, V7X_LANES), NEG, F32)
    l_sc[0:rows, :] = jnp.zeros((rows, V7X_LANES), F32)
    acc_sc[0:rows, :] = jnp.zeros((rows, V7X_LANES), F32)

    def load_state():
        return m_sc[0:rows, :], l_sc[0:rows, :], acc_sc[0:rows, :]

    def store_state(state):
        m_sc[0:rows, :], l_sc[0:rows, :], acc_sc[0:rows, :] = state

    def step(state, kc, vc):
        m_prev, l_prev, acc_prev = state
        s = lax.dot_general(qs_sc[0:rows, :], kc, _NT_DIMS, preferred_element_type=F32)
        m_next = jnp.maximum(m_prev, jnp.max(s, axis=1, keepdims=True))
        alpha = jnp.exp2(m_prev - m_next)
        p = jnp.exp2(s - jnp.concatenate([m_next] * (kc.shape[0] // V7X_LANES), axis=1))
        l_next = alpha * l_prev + jnp.sum(p, axis=1, keepdims=True)
        acc_next = alpha * acc_prev + jnp.dot(p.astype(BF16), vc, preferred_element_type=F32)
        return m_next, l_next, acc_next

    def body(c, carry):
        state = load_state()
        for u in range(unroll):
            state = step(state, *chunks(pl.multiple_of((c * unroll + u) * tk, tk)))
        store_state(state)
        return carry

    lax.fori_loop(0, n_chunks // unroll, body, 0)
    if extra is not None:
        store_state(step(load_state(), *extra))
    return acc_sc[0:rows, :] / l_sc[0:rows, :]


def _bounded_softmax(qs_sc, b_sc, l_sc, acc_sc, rows, chunks, extra, *, tk, n_chunks, unroll):
    l_sc[0:rows, :] = jnp.zeros((rows, V7X_LANES), F32)
    acc_sc[0:rows, :] = jnp.zeros((rows, V7X_LANES), F32)

    def step(state, kc, vc):
        l_prev, acc_prev = state
        n_cols = kc.shape[0] // V7X_LANES
        s = lax.dot_general(qs_sc[0:rows, :], kc, _NT_DIMS, preferred_element_type=F32)
        p = jnp.exp2(s - jnp.concatenate([b_sc[0:rows, :]] * n_cols, axis=1))
        l_next = l_prev
        for j in range(n_cols):
            l_next = l_next + p[:, j * V7X_LANES:(j + 1) * V7X_LANES]
        acc_next = acc_prev + jnp.dot(p.astype(BF16), vc, preferred_element_type=F32)
        return l_next, acc_next

    def body(c, carry):
        state = (l_sc[0:rows, :], acc_sc[0:rows, :])
        for u in range(unroll):
            state = step(state, *chunks(pl.multiple_of((c * unroll + u) * tk, tk)))
        l_sc[0:rows, :], acc_sc[0:rows, :] = state
        return carry

    lax.fori_loop(0, n_chunks // unroll, body, 0)
    state = (l_sc[0:rows, :], acc_sc[0:rows, :])
    if extra is not None:
        state = step(state, *extra)
    l_lanes, acc = state
    return acc / jnp.sum(l_lanes, axis=1, keepdims=True)


def _gqa_kernel(*refs, hq, hkv, tq, tk, n_chunks, unroll, extra_len):
    if extra_len:
        q_ref, k_ref, v_ref, kx_ref, vx_ref, o_ref, qs_sc, m_sc, l_sc, acc_sc, on_sc = refs
    else:
        q_ref, k_ref, v_ref, o_ref, qs_sc, m_sc, l_sc, acc_sc, on_sc = refs
    grp = hq // hkv
    for pc in range(hkv // 2):
        heads = [h for h in range(hq) if (h // grp) // 2 == pc]
        rows = len(heads) * tq
        col = slice(pc * V7X_LANES, (pc + 1) * V7X_LANES)
        _load_q_rows(q_ref, qs_sc, heads, grp, tq)
        chunks = lambda start: (k_ref[pl.ds(start, tk), col], v_ref[pl.ds(start, tk), col])
        extra = (kx_ref[:, col], vx_ref[:, col]) if extra_len else None
        on = _online_softmax(qs_sc, m_sc, l_sc, acc_sc, rows, chunks, extra,
                             tk=tk, n_chunks=n_chunks, unroll=unroll)
        for j, h in enumerate(heads):
            on_sc[h * tq:(h + 1) * tq, :] = on[j * tq:(j + 1) * tq, :]
    _repack_heads(on_sc, o_ref, hq, grp, tq)


def _gqa_bounded_kernel(q_ref, k_ref, v_ref, kx_ref, vx_ref, gs_ref, o_ref,
                        qs_sc, m_sc, l_sc, acc_sc, on_sc, b_sc, kmax_sc, *, hq, tq, tk, n_chunks, unroll):
    hkv = 2
    grp = hq // hkv
    rows = hq * tq

    def max_sq_norm(kc, mx):
        kf = kc.astype(F32)
        ss = jnp.dot((kf * kf).astype(BF16), gs_ref[...], preferred_element_type=F32)
        return jnp.maximum(mx, jnp.max(ss, axis=0, keepdims=True))

    @pl.when(pl.program_id(1) == 0)
    def _():
        def kbody(c, mx):
            return max_sq_norm(k_ref[pl.ds(pl.multiple_of(c * tk, tk), tk), :], mx)
        mx = lax.fori_loop(0, n_chunks, kbody, jnp.zeros((1, V7X_LANES), F32))
        kmax_sc[...] = jnp.broadcast_to(max_sq_norm(kx_ref[...], mx), kmax_sc.shape)

    _load_q_rows(q_ref, qs_sc, list(range(hq)), grp, tq)
    qf = qs_sc[...].astype(F32)
    q_sq = jnp.sum(qf * qf, axis=1, keepdims=True)
    row = lax.broadcasted_iota(jnp.int32, (rows, V7X_LANES), 0)
    lane = lax.broadcasted_iota(jnp.int32, (rows, V7X_LANES), 1)
    own_half = (row < rows // 2) == (lane < HEAD_DIM)
    k_sq = jnp.max(jnp.where(own_half, kmax_sc[0:1, :], 0.0), axis=1, keepdims=True)
    bound = jnp.sqrt(q_sq * k_sq) * (1.0 + 2.0 ** -5)
    b_sc[...] = jnp.broadcast_to(bound, (rows, V7X_LANES))
    bounded_ok = jnp.max(bound) <= SCORE_BOUND_MAX

    chunks = lambda start: (k_ref[pl.ds(start, tk), :], v_ref[pl.ds(start, tk), :])
    extra = (kx_ref[...], vx_ref[...])

    @pl.when(bounded_ok)
    def _():
        on_sc[...] = _bounded_softmax(qs_sc, b_sc, l_sc, acc_sc, rows, chunks, extra,
                                      tk=tk, n_chunks=n_chunks, unroll=unroll)

    @pl.when(jnp.logical_not(bounded_ok))
    def _():
        on_sc[...] = _online_softmax(qs_sc, m_sc, l_sc, acc_sc, rows, chunks, extra,
                                     tk=tk, n_chunks=n_chunks, unroll=unroll)

    _repack_heads(on_sc, o_ref, hq, grp, tq)


def _gqa(q, k, v, extra, *, hq, hkv, tq, tk, unroll=1):
    bsz, s, qw = q.shape
    t = k.shape[1]
    kw = k.shape[2]
    n_chunks = t // tk
    assert n_chunks % unroll == 0
    grp = hq // hkv
    max_heads = max(sum(1 for h in range(hq) if (h // grp) // 2 == pc) for pc in range(hkv // 2))
    rows = max_heads * tq
    in_specs = [
        pl.BlockSpec((None, tq, qw), lambda b, i: (b, i, 0)),
        pl.BlockSpec((None, t, kw), lambda b, i: (b, 0, 0)),
        pl.BlockSpec((None, t, kw), lambda b, i: (b, 0, 0)),
    ]
    args = [q, k, v]
    extra_len = 0
    if extra is not None:
        extra_len = extra[0].shape[1]
        in_specs += [pl.BlockSpec((None, extra_len, kw), lambda b, i: (b, 0, 0))] * 2
        args += list(extra)
    vmem = 4 * t * kw * 2 + rows * max(tk, extra_len) * 16 + rows * V7X_LANES * 24 + 8 * tq * qw * 2
    return pl.pallas_call(
        functools.partial(_gqa_kernel, hq=hq, hkv=hkv, tq=tq, tk=tk, n_chunks=n_chunks, unroll=unroll,
                          extra_len=extra_len),
        grid=(bsz, s // tq),
        in_specs=in_specs,
        out_specs=pl.BlockSpec((None, tq, qw), lambda b, i: (b, i, 0)),
        out_shape=jax.ShapeDtypeStruct(q.shape, BF16),
        scratch_shapes=[
            pltpu.VMEM((rows, V7X_LANES), BF16),
            pltpu.VMEM((rows, V7X_LANES), F32),
            pltpu.VMEM((rows, V7X_LANES), F32),
            pltpu.VMEM((rows, V7X_LANES), F32),
            pltpu.VMEM((hq * tq, V7X_LANES), F32),
        ],
        compiler_params=_params(2, vmem),
        name="gqa_attn",
    )(*args)


def _gqa_bounded(q, k, v, kx, vx, gsum, *, hq, tq, tk, unroll):
    bsz, s, qw = q.shape
    t, kw = k.shape[1], k.shape[2]
    lx = kx.shape[1]
    assert kw == V7X_LANES and t % (tk * unroll) == 0
    rows = hq * tq
    stat = pltpu.VMEM((rows, V7X_LANES), F32)
    vmem = 4 * (t + lx) * kw * 2 + rows * max(tk, lx) * 8 * unroll + rows * V7X_LANES * 32 + 8 * tq * qw * 2
    return pl.pallas_call(
        functools.partial(_gqa_bounded_kernel, hq=hq, tq=tq, tk=tk, n_chunks=t // tk, unroll=unroll),
        grid=(bsz, s // tq),
        in_specs=[
            pl.BlockSpec((None, tq, qw), lambda b, i: (b, i, 0)),
            pl.BlockSpec((None, t, kw), lambda b, i: (b, 0, 0)),
            pl.BlockSpec((None, t, kw), lambda b, i: (b, 0, 0)),
            pl.BlockSpec((None, lx, kw), lambda b, i: (b, 0, 0)),
            pl.BlockSpec((None, lx, kw), lambda b, i: (b, 0, 0)),
            _resident(gsum.shape, lambda b, i: (0, 0)),
        ],
        out_specs=pl.BlockSpec((None, tq, qw), lambda b, i: (b, i, 0)),
        out_shape=jax.ShapeDtypeStruct(q.shape, BF16),
        scratch_shapes=[
            pltpu.VMEM((rows, V7X_LANES), BF16), stat, stat, stat, stat, stat,
            pltpu.VMEM((8, V7X_LANES), F32),
        ],
        compiler_params=_params(2, vmem),
        name="gqa_attn_lat",
    )(q, k, v, kx, vx, gsum)


def _nbr_kernel(q_ref, k_ref, v_ref, kx_ref, vx_ref, bias_ref, o_ref, *, n_rblocks, rows_total):
    rb = pl.program_id(2)
    tq = q_ref.shape[0]
    n_win = NBR_KROWS * GRID_W
    krow0 = jnp.clip(rb * NBR_QROWS - WIN_R // 2, 0, rows_total - NBR_KROWS)
    start = pl.multiple_of(krow0 * GRID_W, NBR_QROWS * GRID_W)
    kw = k_ref[pl.ds(start, n_win), :]
    vw = v_ref[pl.ds(start, n_win), :]
    kx = kx_ref[...]
    vx = vx_ref[...]
    q = q_ref[...]
    lane = lax.broadcasted_iota(jnp.int32, (tq, V7X_LANES), 1)
    low_half = lane < HEAD_DIM
    nt_dims = (((1,), (1,)), ((), ()))
    outs = []
    for e in range(2):
        keep = low_half if e == 0 else jnp.logical_not(low_half)
        qe = jnp.where(keep, q, jnp.zeros_like(q))
        s_win = lax.dot_general(qe, kw, nt_dims, preferred_element_type=F32) + bias_ref[e]
        s_ctx = lax.dot_general(qe, kx, nt_dims, preferred_element_type=F32)
        m = jnp.maximum(jnp.max(s_win, axis=1, keepdims=True), jnp.max(s_ctx, axis=1, keepdims=True))
        p_win = jnp.exp2(s_win - m)
        p_ctx = jnp.exp2(s_ctx - m)
        l = jnp.sum(p_win, axis=1, keepdims=True) + jnp.sum(p_ctx, axis=1, keepdims=True)
        o = (jnp.dot(p_win.astype(BF16), vw, preferred_element_type=F32)
             + jnp.dot(p_ctx.astype(BF16), vx, preferred_element_type=F32))
        outs.append(o / l)
    o_ref[...] = jnp.where(low_half, outs[0], outs[1]).astype(BF16)


def _nbr(q, k, v, kx, vx, bias):
    bsz, s, w = q.shape
    rows_total = s // GRID_W
    n_rblocks = rows_total // NBR_QROWS
    tq = NBR_QROWS * GRID_W
    n_win = NBR_KROWS * GRID_W
    lx = kx.shape[1]
    n_pairs = w // V7X_LANES

    def bias_map(b, pr, rb):
        kind = jnp.where(rb == 0, 0, jnp.where(rb == n_rblocks - 1, 2, 1))
        return (kind, pr, 0, 0)

    vmem = 4 * s * V7X_LANES * 2 + 2 * 2 * tq * n_win * 4 + tq * (n_win + lx) * 24
    return pl.pallas_call(
        functools.partial(_nbr_kernel, n_rblocks=n_rblocks, rows_total=rows_total),
        grid=(bsz, n_pairs, n_rblocks),
        in_specs=[
            pl.BlockSpec((None, tq, V7X_LANES), lambda b, pr, rb: (b, rb, pr)),
            pl.BlockSpec((None, s, V7X_LANES), lambda b, pr, rb: (b, 0, pr)),
            pl.BlockSpec((None, s, V7X_LANES), lambda b, pr, rb: (b, 0, pr)),
            pl.BlockSpec((None, lx, V7X_LANES), lambda b, pr, rb: (b, 0, pr)),
            pl.BlockSpec((None, lx, V7X_LANES), lambda b, pr, rb: (b, 0, pr)),
            pl.BlockSpec((None, 2, tq, n_win), bias_map),
        ],
        out_specs=pl.BlockSpec((None, tq, V7X_LANES), lambda b, pr, rb: (b, rb, pr)),
        out_shape=jax.ShapeDtypeStruct(q.shape, BF16),
        compiler_params=_params(3, vmem),
        name="nbr_attn",
    )(q, k, v, kx, vx, bias)


def _nbr_bias_tables(rpb, rows_total):
    assert rows_total % NBR_QROWS == 0 and rows_total >= NBR_KROWS + NBR_QROWS
    wr = min(WIN_R, rows_total)
    kinds = [(0, 0), (2 * NBR_QROWS, 2 * NBR_QROWS - WIN_R // 2),
             (rows_total - NBR_QROWS, rows_total - NBR_KROWS)]
    qi = np.arange(NBR_QROWS)[:, None, None, None]
    qc = np.arange(GRID_W)[None, :, None, None]
    kj = np.arange(NBR_KROWS)[None, None, :, None]
    kc = np.arange(GRID_W)[None, None, None, :]
    n_dr, n_dc = rpb.shape[1], rpb.shape[2]
    cs = np.clip(qc - WIN_C // 2, 0, GRID_W - WIN_C)
    col_ok = (kc >= cs) & (kc < cs + WIN_C)
    col_sel = (kc - qc + (WIN_C - 1))[..., None] == np.arange(n_dc)
    col_sel = (col_sel & col_ok[..., None])[0, :, 0].astype(np.float32)
    row_sel, ok_all = [], []
    for r0, k0 in kinds:
        r = r0 + qi
        rs = np.clip(r - wr // 2, 0, rows_total - wr)
        kr = k0 + kj
        row_ok = (kr >= rs) & (kr < rs + wr)
        sel = ((kr - r + (WIN_R - 1))[..., None] == np.arange(n_dr)) & row_ok[..., None]
        row_sel.append(sel[:, 0, :, 0].astype(np.float32))
        ok_all.append(np.broadcast_to(row_ok & col_ok, (NBR_QROWS, GRID_W, NBR_KROWS, GRID_W)))
    row_sel = jnp.asarray(np.stack(row_sel))
    ok = np.stack(ok_all).reshape(3, 1, NBR_QROWS * GRID_W, NBR_KROWS * GRID_W)
    hi = lax.Precision.HIGHEST
    rows_picked = jnp.einsum('hrd,tijr->thijd', rpb, row_sel, precision=hi)
    tab = jnp.einsum('thijd,cnd->thicjn', rows_picked, jnp.asarray(col_sel), precision=hi)
    tab = tab.reshape(3, rpb.shape[0], NBR_QROWS * GRID_W, NBR_KROWS * GRID_W)
    return jnp.where(ok, tab * LOG2_E, NEG).astype(F32)


def _oproj_kernel(h_ref, oa_ref, ob_ref, oc_ref, mod_ref, g_ref, w_ref, o_ref):
    wa = oa_ref.shape[1]
    wb = ob_ref.shape[1]
    y = (jnp.dot(oa_ref[...], w_ref[0:wa, :], preferred_element_type=F32)
         + jnp.dot(ob_ref[...], w_ref[wa:wa + wb, :], preferred_element_type=F32)
         + jnp.dot(oc_ref[...], w_ref[wa + wb:, :], preferred_element_type=F32))
    o_ref[...] = h_ref[...] + mod_ref[5:6, :] * _rms(y, g_ref[3:4, :])


def _oproj(h, oa, ob, oc, mods, norm_g, w_o, *, layer, mod_row, tm):
    bsz, s, d = h.shape
    if mod_row is None:
        mod_map = lambda b, i: (layer, b, 0, 0)
    else:
        mod_map = lambda b, i: (layer, mod_row, 0, 0)
    tile = lambda w: pl.BlockSpec((None, tm, w), lambda b, i: (b, i, 0))
    vmem = w_o.shape[1] * d * 2 + tm * d * 4 * 8
    return pl.pallas_call(
        _oproj_kernel,
        grid=(bsz, s // tm),
        in_specs=[
            tile(d), tile(oa.shape[2]), tile(ob.shape[2]), tile(oc.shape[2]),
            pl.BlockSpec((None, None, N_MOD, d), mod_map),
            _resident((None, norm_g.shape[1], d), lambda b, i: (layer, 0, 0)),
            _resident((None, w_o.shape[1], d), lambda b, i: (layer, 0, 0)),
        ],
        out_specs=tile(d),
        out_shape=jax.ShapeDtypeStruct(h.shape, F32),
        compiler_params=_params(2, vmem),
        name="mix_out",
    )(h, oa, ob, oc, mods, norm_g, w_o)


def _rope_tables(seq):
    pos = jnp.arange(seq, dtype=jnp.int32)
    row = (pos // GRID_W).astype(F32)
    col = (pos % GRID_W).astype(F32)
    freq = 1.0 / (ROPE_THETA ** (jnp.arange(ROPE_FREQS, dtype=F32) / ROPE_FREQS))
    ar = row[:, None] * freq
    ac = col[:, None] * freq
    cos = jnp.concatenate([jnp.cos(ar), jnp.cos(ar), jnp.cos(ac), jnp.cos(ac)], axis=1)
    sin = jnp.concatenate([-jnp.sin(ar), jnp.sin(ar), -jnp.sin(ac), jnp.sin(ac)], axis=1)
    reps = V7X_LANES // HEAD_DIM
    return jnp.tile(cos, (1, reps)), jnp.tile(sin, (1, reps))


def _tile_rows(n, target):
    t = min(n, target)
    assert n % t == 0
    return t


def kernel(x, c, ctx, c_ctx, w_ada, b_ada, norm_g, w_in, qk_g, rpb, conv_w, w_o, ffn_wi, ffn_wo):
    bsz, seq, d = x.shape
    ctx_len = ctx.shape[1]
    depth = w_ada.shape[0]
    ctx_row = bsz

    mod_rows = -(-(bsz + 1) // 8) * 8
    c_all = jnp.zeros((mod_rows, d), F32).at[:bsz].set(c).at[ctx_row].set(c_ctx)
    mods = _ada(c_all, w_ada, b_ada).reshape(depth, mod_rows, N_MOD, d)

    w_in_b = w_in.astype(BF16)
    w_o_b = w_o.astype(BF16)
    wi_b = ffn_wi.astype(BF16)
    wo_b = ffn_wo.astype(BF16)
    qk_gain = jnp.concatenate(
        [jnp.tile(qk_g[:, 0], (1, A_Q_HEADS)), jnp.tile(qk_g[:, 1], (1, A_KV_HEADS))], axis=1
    ).reshape(depth, 1, A_Q_W + A_KV_W)
    gsz = (A_Q_W + A_KV_W) // 2
    head_of = np.arange(gsz) // HEAD_DIM
    gmat = jnp.asarray((head_of[:, None] == head_of[None, :]) / HEAD_DIM, dtype=BF16)
    lane_head = np.arange(V7X_LANES) // HEAD_DIM
    gsum = jnp.asarray(lane_head[:, None] == lane_head[None, :], dtype=BF16)
    rope_tabs = _rope_tables(seq)

    tm_ffn = _tile_rows(seq, 256)
    tm_proj = _tile_rows(seq, 512)
    tm_ctx = _tile_rows(ctx_len, 256)
    tq_a = _tile_rows(seq, 256)
    tk_a = _tile_rows(seq, 512)
    unroll_a = min(GQA_UNROLL, seq // tk_a)

    h, hc = x, ctx
    for layer in range(depth):
        last = layer == depth - 1
        lat = dict(layer=layer, mod_row=None)
        cx = dict(layer=layer, mod_row=ctx_row)
        h = _ffn(h, mods, norm_g, wi_b, wo_b, which=0, tm=tm_ffn, **lat)
        hc = _ffn(hc, mods, norm_g, wi_b, wo_b, which=0, tm=tm_ctx, **cx)
        qa, qb, ka, va, kb, vb, oc = _proj(h, mods, norm_g, w_in_b, qk_gain, conv_w, gmat, rope_tabs,
                                           tm=tm_proj, **lat)
        cqa, cqb, cka, cva, ckb, cvb, coc = _proj(hc, mods, norm_g, w_in_b, qk_gain, conv_w, gmat, None,
                                                  tm=tm_ctx, **cx)
        if not last:
            coa = _gqa(cqa, cka, cva, None, hq=A_Q_HEADS, hkv=A_KV_HEADS, tq=tm_ctx, tk=ctx_len)
            cob = _gqa(cqb, ckb, cvb, None, hq=B_HEADS, hkv=B_HEADS, tq=tm_ctx, tk=ctx_len)
            hc = _oproj(hc, coa, cob, coc, mods, norm_g, w_o_b, tm=tm_ctx, **cx)
        oa = _gqa_bounded(qa, ka, va, cka, cva, gsum, hq=A_Q_HEADS, tq=tq_a, tk=tk_a, unroll=unroll_a)
        ob = _nbr(qb, kb, vb, ckb, cvb, _nbr_bias_tables(rpb[layer], seq // GRID_W))
        h = _oproj(h, oa, ob, oc, mods, norm_g, w_o_b, tm=tm_proj, **lat)
        h = _ffn(h, mods, norm_g, wi_b, wo_b, which=1, tm=tm_ffn, **lat)
        if not last:
            hc = _ffn(hc, mods, norm_g, wi_b, wo_b, which=1, tm=tm_ctx, **cx)
    return h
```

```python
import functools

import numpy as np
import jax
import jax.numpy as jnp
from jax import lax
from jax.experimental import pallas as pl
from jax.experimental.pallas import tpu as pltpu

F32 = jnp.float32
BF16 = jnp.bfloat16

HEAD_DIM = 64
GRID_W = 64
A_Q_HEADS = 6
A_KV_HEADS = 2
B_HEADS = 6
C_WIDTH = 256
A_Q_W = A_Q_HEADS * HEAD_DIM
A_KV_W = A_KV_HEADS * HEAD_DIM
B_W = B_HEADS * HEAD_DIM
WIN_R = 8
WIN_C = 16
ROPE_FREQS = HEAD_DIM // 4
ROPE_THETA = 10000.0
N_MOD = 9
EPS = 1e-6
NEG = -1e30
LOG2_E = 1.4426950408889634

V7X_LANES = 128
V7X_SCOPED_VMEM_BYTES = 60000 * 1024

NBR_QROWS = 4
NBR_KROWS = NBR_QROWS + WIN_R

GQA_UNROLL = 4
ROW_SUBTILES = 4
NBR_BLOCKS_PER_STEP = 4


def _vmem_limit(estimate_bytes):
    return int(min(max(estimate_bytes, 16 * 1024 * 1024), V7X_SCOPED_VMEM_BYTES))


def _params(n_axes, vmem_bytes):
    return pltpu.CompilerParams(
        dimension_semantics=("arbitrary",) * n_axes,
        vmem_limit_bytes=_vmem_limit(vmem_bytes),
    )


def _rms(x, g):
    ms = jnp.mean(x * x, axis=-1, keepdims=True)
    return x * lax.rsqrt(ms + EPS) * g


def _resident(block_shape, index_map):
    return pl.BlockSpec(block_shape, index_map, pipeline_mode=pl.Buffered(1))


def _ada_kernel(c_ref, w_ref, b_ref, o_ref):
    c = c_ref[...]
    sc = c * jax.nn.sigmoid(c)
    o_ref[...] = jnp.dot(sc, w_ref[...], preferred_element_type=F32,
                         precision=lax.Precision.HIGHEST) + b_ref[...]


def _ada(c_all, w_ada, b_ada):
    depth, d, n = w_ada.shape
    rows = c_all.shape[0]
    tn = d
    return pl.pallas_call(
        _ada_kernel,
        grid=(depth, n // tn),
        in_specs=[
            pl.BlockSpec((rows, d), lambda l, j: (0, 0)),
            pl.BlockSpec((None, d, tn), lambda l, j: (l, 0, j)),
            pl.BlockSpec((None, 1, tn), lambda l, j: (l, 0, j)),
        ],
        out_specs=pl.BlockSpec((None, rows, tn), lambda l, j: (l, 0, j)),
        out_shape=jax.ShapeDtypeStruct((depth, rows, n), F32),
        compiler_params=_params(2, 4 * d * tn * 4),
        name="ada_mod",
    )(c_all, w_ada, b_ada.reshape(depth, 1, n))


def _ffn_kernel(h_ref, mod_ref, g_ref, wi_ref, wo_ref, o_ref, *, i0, gi, ffn_dim, n_sub):
    shift = mod_ref[i0:i0 + 1, :]
    scale = mod_ref[i0 + 1:i0 + 2, :]
    gate = mod_ref[i0 + 2:i0 + 3, :]
    sub = h_ref.shape[0] // n_sub
    for t in range(n_sub):
        rows = slice(t * sub, (t + 1) * sub)
        h = h_ref[rows, :]
        u = _rms(h, g_ref[gi:gi + 1, :]) * (1.0 + scale) + shift
        hid = jnp.dot(u.astype(BF16), wi_ref[...], preferred_element_type=F32)
        gt = hid[:, :ffn_dim]
        up = hid[:, ffn_dim:]
        act = (gt * jax.nn.sigmoid(gt) * up).astype(BF16)
        y = jnp.dot(act, wo_ref[...], preferred_element_type=F32)
        o_ref[rows, :] = h + 0.5 * gate * _rms(y, g_ref[gi + 1:gi + 2, :])


def _ffn(h, mods, norm_g, wi, wo, *, layer, which, mod_row, tm, n_sub=1):
    bsz, s, d = h.shape
    ffn_dim = wo.shape[2]
    i0 = 6 * which
    gi = 4 * which
    if mod_row is None:
        mod_map = lambda b, i: (layer, b, 0, 0)
    else:
        mod_map = lambda b, i: (layer, mod_row, 0, 0)
    vmem = ((wi.shape[2] * wi.shape[3] + wo.shape[2] * wo.shape[3]) * 2 + tm * d * 4 * 6
            + (tm // n_sub) * ffn_dim * 24)
    return pl.pallas_call(
        functools.partial(_ffn_kernel, i0=i0, gi=gi, ffn_dim=ffn_dim, n_sub=n_sub),
        grid=(bsz, s // tm),
        in_specs=[
            pl.BlockSpec((None, tm, d), lambda b, i: (b, i, 0)),
            pl.BlockSpec((None, None, N_MOD, d), mod_map),
            _resident((None, norm_g.shape[1], d), lambda b, i: (layer, 0, 0)),
            _resident((None, None, d, 2 * ffn_dim), lambda b, i: (layer, which, 0, 0)),
            _resident((None, None, ffn_dim, d), lambda b, i: (layer, which, 0, 0)),
        ],
        out_specs=pl.BlockSpec((None, tm, d), lambda b, i: (b, i, 0)),
        out_shape=jax.ShapeDtypeStruct(h.shape, F32),
        compiler_params=_params(2, vmem),
        name="ffn",
    )(h, mods, norm_g, wi, wo)


def _swap_rope_partners(x):
    lane = lax.broadcasted_iota(jnp.int32, x.shape, 1)
    first = (lane % (2 * ROPE_FREQS)) < ROPE_FREQS
    return jnp.where(first,
                     pltpu.roll(x, V7X_LANES - ROPE_FREQS, axis=1),
                     pltpu.roll(x, ROPE_FREQS, axis=1))


def _proj_kernel(*refs, rope, tm, n_tiles):
    if rope:
        (h_ref, hp_ref, hn_ref, mod_ref, g_ref, w_ref, qkg_ref, cw_ref, gm_ref, cos_ref, sin_ref,
         qa_ref, qb_ref, ka_ref, va_ref, kb_ref, vb_ref, oc_ref) = refs
    else:
        (h_ref, hp_ref, hn_ref, mod_ref, g_ref, w_ref, qkg_ref, cw_ref, gm_ref,
         qa_ref, qb_ref, ka_ref, va_ref, kb_ref, vb_ref, oc_ref) = refs
    i = pl.program_id(1)
    shift = mod_ref[3:4, :]
    scale = mod_ref[4:5, :]
    g2 = g_ref[2:3, :]

    def pre(x):
        return (_rms(x, g2) * (1.0 + scale) + shift).astype(BF16)

    p = jnp.dot(pre(h_ref[...]), w_ref[...], preferred_element_type=F32)
    o_qa, o_qb = 0, A_Q_W
    o_ka = o_qb + B_W
    o_va = o_ka + A_KV_W
    o_kb = o_va + A_KV_W
    o_vb = o_kb + B_W
    o_cx = o_vb + B_W
    o_cb = o_cx + C_WIDTH
    o_cc = o_cb + C_WIDTH

    xq = jnp.concatenate([p[:, o_qa:o_qa + A_Q_W], p[:, o_ka:o_ka + A_KV_W]], axis=1)
    sq = (xq * xq).astype(BF16)
    half = (A_Q_W + A_KV_W) // 2
    ms = jnp.concatenate(
        [jnp.dot(sq[:, :half], gm_ref[...], preferred_element_type=F32),
         jnp.dot(sq[:, half:], gm_ref[...], preferred_element_type=F32)], axis=1)
    xn = xq * lax.rsqrt(ms + EPS) * qkg_ref[...]
    if rope:
        cos = cos_ref[...]
        sin = sin_ref[...]
        cols = []
        for j in range((A_Q_W + A_KV_W) // V7X_LANES):
            xc = xn[:, j * V7X_LANES:(j + 1) * V7X_LANES]
            cols.append(xc * cos + _swap_rope_partners(xc) * sin)
        xn = jnp.concatenate(cols, axis=1)
    q_scale = HEAD_DIM ** -0.5 * LOG2_E
    qa_ref[...] = (xn[:, :A_Q_W] * q_scale).astype(BF16)
    ka_ref[...] = xn[:, A_Q_W:].astype(BF16)
    qb_ref[...] = (p[:, o_qb:o_qb + B_W] * q_scale).astype(BF16)
    va_ref[...] = p[:, o_va:o_va + A_KV_W].astype(BF16)
    kb_ref[...] = p[:, o_kb:o_kb + B_W].astype(BF16)
    vb_ref[...] = p[:, o_vb:o_vb + B_W].astype(BF16)

    z = p[:, o_cc:o_cc + C_WIDTH] * p[:, o_cx:o_cx + C_WIDTH]

    def halo(ref):
        ub = pre(ref[...])
        return (jnp.dot(ub, w_ref[:, o_cc:o_cc + C_WIDTH], preferred_element_type=F32)
                * jnp.dot(ub, w_ref[:, o_cx:o_cx + C_WIDTH], preferred_element_type=F32))

    halo_rows = hp_ref.shape[0]
    z_before = jnp.where(i > 0, halo(hp_ref)[halo_rows - 1:halo_rows, :], 0.0)
    z_after = jnp.where(i < n_tiles - 1, halo(hn_ref)[0:1, :], 0.0)
    row = lax.broadcasted_iota(jnp.int32, z.shape, 0)
    z_m1 = jnp.where(row == 0, z_before, pltpu.roll(z, 1, axis=0))
    z_p1 = jnp.where(row == tm - 1, z_after, pltpu.roll(z, tm - 1, axis=0))
    y = cw_ref[0:1, :] * z_m1 + cw_ref[1:2, :] * z + cw_ref[2:3, :] * z_p1
    oc_ref[...] = (p[:, o_cb:o_cb + C_WIDTH] * y).astype(BF16)


def _proj(h, mods, norm_g, w_in, qk_gain, conv_w, gmat, rope_tabs, *, layer, mod_row, tm):
    bsz, s, d = h.shape
    n_tiles = s // tm
    halo_rows = 8
    hb = tm // halo_rows
    n_hblk = s // halo_rows
    rope = rope_tabs is not None
    if mod_row is None:
        mod_map = lambda b, i: (layer, b, 0, 0)
    else:
        mod_map = lambda b, i: (layer, mod_row, 0, 0)
    pw = w_in.shape[2]
    in_specs = [
        pl.BlockSpec((None, tm, d), lambda b, i: (b, i, 0)),
        pl.BlockSpec((None, halo_rows, d), lambda b, i: (b, jnp.maximum(i * hb - 1, 0), 0)),
        pl.BlockSpec((None, halo_rows, d), lambda b, i: (b, jnp.minimum((i + 1) * hb, n_hblk - 1), 0)),
        pl.BlockSpec((None, None, N_MOD, d), mod_map),
        _resident((None, norm_g.shape[1], d), lambda b, i: (layer, 0, 0)),
        _resident((None, d, pw), lambda b, i: (layer, 0, 0)),
        _resident((None, 1, A_Q_W + A_KV_W), lambda b, i: (layer, 0, 0)),
        _resident((None, conv_w.shape[1], C_WIDTH), lambda b, i: (layer, 0, 0)),
        _resident(gmat.shape, lambda b, i: (0, 0)),
    ]
    args = [h, h, h, mods, norm_g, w_in, qk_gain, conv_w, gmat]
    if rope:
        in_specs += [pl.BlockSpec((tm, V7X_LANES), lambda b, i: (i, 0))] * 2
        args += list(rope_tabs)
    widths = (A_Q_W, B_W, A_KV_W, A_KV_W, B_W, B_W, C_WIDTH)
    out_specs = [pl.BlockSpec((None, tm, w), lambda b, i: (b, i, 0)) for w in widths]
    out_shape = [jax.ShapeDtypeStruct((bsz, s, w), BF16) for w in widths]
    vmem = d * pw * 2 + tm * d * 4 * 4 + tm * pw * 12
    return pl.pallas_call(
        functools.partial(_proj_kernel, rope=rope, tm=tm, n_tiles=n_tiles),
        grid=(bsz, n_tiles),
        in_specs=in_specs,
        out_specs=out_specs,
        out_shape=out_shape,
        compiler_params=_params(2, vmem),
        name="mix_proj",
    )(*args)


_NT_DIMS = (((1,), (1,)), ((), ()))

SCORE_BOUND_MAX = 40.0


def _load_q_rows(q_ref, qs_sc, heads, grp, tq):
    lane = lax.broadcasted_iota(jnp.int32, (tq, V7X_LANES), 1)
    low_half = lane < HEAD_DIM
    for j, h in enumerate(heads):
        xc = q_ref[:, (h // 2) * V7X_LANES:(h // 2 + 1) * V7X_LANES].astype(F32)
        dst_low = (h // grp) % 2 == 0
        if (h % 2 == 0) != dst_low:
            xc = pltpu.roll(xc, HEAD_DIM, axis=1)
        keep = low_half if dst_low else jnp.logical_not(low_half)
        qs_sc[j * tq:(j + 1) * tq, :] = jnp.where(keep, xc, 0.0).astype(BF16)


def _repack_heads(on_sc, o_ref, hq, grp, tq):
    lane = lax.broadcasted_iota(jnp.int32, (tq, V7X_LANES), 1)
    low_half = lane < HEAD_DIM
    for oc in range(hq // 2):
        pieces = []
        for e in range(2):
            h = 2 * oc + e
            piece = on_sc[h * tq:(h + 1) * tq, :]
            src_low = (h // grp) % 2 == 0
            if src_low != (e == 0):
                piece = pltpu.roll(piece, HEAD_DIM, axis=1)
            pieces.append(piece)
        o_ref[:, oc * V7X_LANES:(oc + 1) * V7X_LANES] = jnp.where(low_half, pieces[0], pieces[1]).astype(BF16)


def _online_softmax(qs_sc, m_sc, l_sc, acc_sc, rows, chunks, extra, *, tk, n_chunks, unroll):
    m_sc[0:rows, :] = jnp.full((rows, V7X_LANES), NEG, F32)
    l_sc[0:rows, :] = jnp.zeros((rows, V7X_LANES), F32)
    acc_sc[0:rows, :] = jnp.zeros((rows, V7X_LANES), F32)

    def load_state():
        return m_sc[0:rows, :], l_sc[0:rows, :], acc_sc[0:rows, :]

    def store_state(state):
        m_sc[0:rows, :], l_sc[0:rows, :], acc_sc[0:rows, :] = state

    def step(state, kc, vc):
        m_prev, l_prev, acc_prev = state
        s = lax.dot_general(qs_sc[0:rows, :], kc, _NT_DIMS, preferred_element_type=F32)
        m_next = jnp.maximum(m_prev, jnp.max(s, axis=1, keepdims=True))
        alpha = jnp.exp2(m_prev - m_next)
        p = jnp.exp2(s - jnp.concatenate([m_next] * (kc.shape[0] // V7X_LANES), axis=1))
        l_next = alpha * l_prev + jnp.sum(p, axis=1, keepdims=True)
        acc_next = alpha * acc_prev + jnp.dot(p.astype(BF16), vc, preferred_element_type=F32)
        return m_next, l_next, acc_next

    def body(c, carry):
        state = load_state()
        for u in range(unroll):
            state = step(state, *chunks(pl.multiple_of((c * unroll + u) * tk, tk)))
        store_state(state)
        return carry

    lax.fori_loop(0, n_chunks // unroll, body, 0)
    if extra is not None:
        store_state(step(load_state(), *extra))
    return acc_sc[0:rows, :] / l_sc[0:rows, :]


def _bounded_softmax(qs_sc, b_sc, l_sc, acc_sc, rows, chunks, extra, *, tk, n_chunks, unroll):
    l_sc[0:rows, :] = jnp.zeros((rows, V7X_LANES), F32)
    acc_sc[0:rows, :] = jnp.zeros((rows, V7X_LANES), F32)

    def step(state, kc, vc):
        l_prev, acc_prev = state
        n_cols = kc.shape[0] // V7X_LANES
        s = lax.dot_general(qs_sc[0:rows, :], kc, _NT_DIMS, preferred_element_type=F32)
        p = jnp.exp2(s - jnp.concatenate([b_sc[0:rows, :]] * n_cols, axis=1))
        l_next = l_prev
        for j in range(n_cols):
            l_next = l_next + p[:, j * V7X_LANES:(j + 1) * V7X_LANES]
        acc_next = acc_prev + jnp.dot(p.astype(BF16), vc, preferred_element_type=F32)
        return l_next, acc_next

    def body(c, carry):
        state = (l_sc[0:rows, :], acc_sc[0:rows, :])
        for u in range(unroll):
            state = step(state, *chunks(pl.multiple_of((c * unroll + u) * tk, tk)))
        l_sc[0:rows, :], acc_sc[0:rows, :] = state
        return carry

    lax.fori_loop(0, n_chunks // unroll, body, 0)
    state = (l_sc[0:rows, :], acc_sc[0:rows, :])
    if extra is not None:
        state = step(state, *extra)
    l_lanes, acc = state
    return acc / jnp.sum(l_lanes, axis=1, keepdims=True)


def _gqa_kernel(*refs, hq, hkv, tq, tk, n_chunks, unroll, extra_len):
    if extra_len:
        q_ref, k_ref, v_ref, kx_ref, vx_ref, o_ref, qs_sc, m_sc, l_sc, acc_sc, on_sc = refs
    else:
        q_ref, k_ref, v_ref, o_ref, qs_sc, m_sc, l_sc, acc_sc, on_sc = refs
    grp = hq // hkv
    for pc in range(hkv // 2):
        heads = [h for h in range(hq) if (h // grp) // 2 == pc]
        rows = len(heads) * tq
        col = slice(pc * V7X_LANES, (pc + 1) * V7X_LANES)
        _load_q_rows(q_ref, qs_sc, heads, grp, tq)
        chunks = lambda start: (k_ref[pl.ds(start, tk), col], v_ref[pl.ds(start, tk), col])
        extra = (kx_ref[:, col], vx_ref[:, col]) if extra_len else None
        on = _online_softmax(qs_sc, m_sc, l_sc, acc_sc, rows, chunks, extra,
                             tk=tk, n_chunks=n_chunks, unroll=unroll)
        for j, h in enumerate(heads):
            on_sc[h * tq:(h + 1) * tq, :] = on[j * tq:(j + 1) * tq, :]
    _repack_heads(on_sc, o_ref, hq, grp, tq)


def _gqa_bounded_kernel(q_ref, k_ref, v_ref, kx_ref, vx_ref, gs_ref, o_ref,
                        qs_sc, m_sc, l_sc, acc_sc, on_sc, b_sc, kmax_sc, *, hq, tq, tk, n_chunks, unroll):
    hkv = 2
    grp = hq // hkv
    rows = hq * tq

    def max_sq_norm(kc, mx):
        kf = kc.astype(F32)
        ss = jnp.dot((kf * kf).astype(BF16), gs_ref[...], preferred_element_type=F32)
        return jnp.maximum(mx, jnp.max(ss, axis=0, keepdims=True))

    @pl.when(pl.program_id(1) == 0)
    def _():
        def kbody(c, mx):
            return max_sq_norm(k_ref[pl.ds(pl.multiple_of(c * tk, tk), tk), :], mx)
        mx = lax.fori_loop(0, n_chunks, kbody, jnp.zeros((1, V7X_LANES), F32))
        kmax_sc[...] = jnp.broadcast_to(max_sq_norm(kx_ref[...], mx), kmax_sc.shape)

    _load_q_rows(q_ref, qs_sc, list(range(hq)), grp, tq)
    qf = qs_sc[...].astype(F32)
    q_sq = jnp.sum(qf * qf, axis=1, keepdims=True)
    row = lax.broadcasted_iota(jnp.int32, (rows, V7X_LANES), 0)
    lane = lax.broadcasted_iota(jnp.int32, (rows, V7X_LANES), 1)
    own_half = (row < rows // 2) == (lane < HEAD_DIM)
    k_sq = jnp.max(jnp.where(own_half, kmax_sc[0:1, :], 0.0), axis=1, keepdims=True)
    bound = jnp.sqrt(q_sq * k_sq) * (1.0 + 2.0 ** -5)
    b_sc[...] = jnp.broadcast_to(bound, (rows, V7X_LANES))
    bounded_ok = jnp.max(bound) <= SCORE_BOUND_MAX

    chunks = lambda start: (k_ref[pl.ds(start, tk), :], v_ref[pl.ds(start, tk), :])
    extra = (kx_ref[...], vx_ref[...])

    @pl.when(bounded_ok)
    def _():
        on_sc[...] = _bounded_softmax(qs_sc, b_sc, l_sc, acc_sc, rows, chunks, extra,
                                      tk=tk, n_chunks=n_chunks, unroll=unroll)

    @pl.when(jnp.logical_not(bounded_ok))
    def _():
        on_sc[...] = _online_softmax(qs_sc, m_sc, l_sc, acc_sc, rows, chunks, extra,
                                     tk=tk, n_chunks=n_chunks, unroll=unroll)

    _repack_heads(on_sc, o_ref, hq, grp, tq)


def _gqa(q, k, v, extra, *, hq, hkv, tq, tk, unroll=1):
    bsz, s, qw = q.shape
    t = k.shape[1]
    kw = k.shape[2]
    n_chunks = t // tk
    assert n_chunks % unroll == 0
    grp = hq // hkv
    max_heads = max(sum(1 for h in range(hq) if (h // grp) // 2 == pc) for pc in range(hkv // 2))
    rows = max_heads * tq
    in_specs = [
        pl.BlockSpec((None, tq, qw), lambda b, i: (b, i, 0)),
        pl.BlockSpec((None, t, kw), lambda b, i: (b, 0, 0)),
        pl.BlockSpec((None, t, kw), lambda b, i: (b, 0, 0)),
    ]
    args = [q, k, v]
    extra_len = 0
    if extra is not None:
        extra_len = extra[0].shape[1]
        in_specs += [pl.BlockSpec((None, extra_len, kw), lambda b, i: (b, 0, 0))] * 2
        args += list(extra)
    vmem = 4 * t * kw * 2 + rows * max(tk, extra_len) * 16 + rows * V7X_LANES * 24 + 8 * tq * qw * 2
    return pl.pallas_call(
        functools.partial(_gqa_kernel, hq=hq, hkv=hkv, tq=tq, tk=tk, n_chunks=n_chunks, unroll=unroll,
                          extra_len=extra_len),
        grid=(bsz, s // tq),
        in_specs=in_specs,
        out_specs=pl.BlockSpec((None, tq, qw), lambda b, i: (b, i, 0)),
        out_shape=jax.ShapeDtypeStruct(q.shape, BF16),
        scratch_shapes=[
            pltpu.VMEM((rows, V7X_LANES), BF16),
            pltpu.VMEM((rows, V7X_LANES), F32),
            pltpu.VMEM((rows, V7X_LANES), F32),
            pltpu.VMEM((rows, V7X_LANES), F32),
            pltpu.VMEM((hq * tq, V7X_LANES), F32),
        ],
        compiler_params=_params(2, vmem),
        name="gqa_attn",
    )(*args)


def _gqa_bounded(q, k, v, kx, vx, gsum, *, hq, tq, tk, unroll):
    bsz, s, qw = q.shape
    t, kw = k.shape[1], k.shape[2]
    lx = kx.shape[1]
    assert kw == V7X_LANES and t % (tk * unroll) == 0
    rows = hq * tq
    stat = pltpu.VMEM((rows, V7X_LANES), F32)
    vmem = 4 * (t + lx) * kw * 2 + rows * max(tk, lx) * 8 * unroll + rows * V7X_LANES * 32 + 8 * tq * qw * 2
    return pl.pallas_call(
        functools.partial(_gqa_bounded_kernel, hq=hq, tq=tq, tk=tk, n_chunks=t // tk, unroll=unroll),
        grid=(bsz, s // tq),
        in_specs=[
            pl.BlockSpec((None, tq, qw), lambda b, i: (b, i, 0)),
            pl.BlockSpec((None, t, kw), lambda b, i: (b, 0, 0)),
            pl.BlockSpec((None, t, kw), lambda b, i: (b, 0, 0)),
            pl.BlockSpec((None, lx, kw), lambda b, i: (b, 0, 0)),
            pl.BlockSpec((None, lx, kw), lambda b, i: (b, 0, 0)),
            _resident(gsum.shape, lambda b, i: (0, 0)),
        ],
        out_specs=pl.BlockSpec((None, tq, qw), lambda b, i: (b, i, 0)),
        out_shape=jax.ShapeDtypeStruct(q.shape, BF16),
        scratch_shapes=[
            pltpu.VMEM((rows, V7X_LANES), BF16), stat, stat, stat, stat, stat,
            pltpu.VMEM((8, V7X_LANES), F32),
        ],
        compiler_params=_params(2, vmem),
        name="gqa_attn_lat",
    )(q, k, v, kx, vx, gsum)


def _nbr_kernel(q_ref, k_ref, v_ref, kx_ref, vx_ref, *rest, n_blocks, rows_total):
    bias_refs, o_ref = rest[:n_blocks], rest[n_blocks]
    tq = NBR_QROWS * GRID_W
    n_win = NBR_KROWS * GRID_W
    kx = kx_ref[...]
    vx = vx_ref[...]
    lane = lax.broadcasted_iota(jnp.int32, (tq, V7X_LANES), 1)
    low_half = lane < HEAD_DIM
    for j in range(n_blocks):
        rb = pl.program_id(2) * n_blocks + j
        krow0 = jnp.clip(rb * NBR_QROWS - WIN_R // 2, 0, rows_total - NBR_KROWS)
        start = pl.multiple_of(krow0 * GRID_W, NBR_QROWS * GRID_W)
        kw = k_ref[pl.ds(start, n_win), :]
        vw = v_ref[pl.ds(start, n_win), :]
        q = q_ref[j * tq:(j + 1) * tq, :]
        zero = jnp.zeros_like(q)
        qs = jnp.concatenate([jnp.where(low_half, q, zero), jnp.where(low_half, zero, q)], axis=0)
        bias = bias_refs[j][...].reshape(2 * tq, n_win)
        s_win = lax.dot_general(qs, kw, _NT_DIMS, preferred_element_type=F32) + bias
        s_ctx = lax.dot_general(qs, kx, _NT_DIMS, preferred_element_type=F32)
        m = jnp.maximum(jnp.max(s_win, axis=1, keepdims=True), jnp.max(s_ctx, axis=1, keepdims=True))
        p_win = jnp.exp2(s_win - m)
        p_ctx = jnp.exp2(s_ctx - m)
        l = jnp.sum(p_win, axis=1, keepdims=True) + jnp.sum(p_ctx, axis=1, keepdims=True)
        o = (jnp.dot(p_win.astype(BF16), vw, preferred_element_type=F32)
             + jnp.dot(p_ctx.astype(BF16), vx, preferred_element_type=F32)) / l
        o_ref[j * tq:(j + 1) * tq, :] = jnp.where(low_half, o[:tq], o[tq:]).astype(BF16)


def _nbr(q, k, v, kx, vx, bias, *, n_blocks=1):
    bsz, s, w = q.shape
    rows_total = s // GRID_W
    n_rblocks = rows_total // NBR_QROWS
    assert n_rblocks % n_blocks == 0
    tq = NBR_QROWS * GRID_W
    n_win = NBR_KROWS * GRID_W
    lx = kx.shape[1]
    n_pairs = w // V7X_LANES

    def bias_spec(j):
        def bias_map(b, pr, st):
            rb = st * n_blocks + j
            kind = jnp.where(rb == 0, 0, jnp.where(rb == n_rblocks - 1, 2, 1))
            return (kind, pr, 0, 0)
        return pl.BlockSpec((None, 2, tq, n_win), bias_map)

    qo_spec = pl.BlockSpec((None, n_blocks * tq, V7X_LANES), lambda b, pr, st: (b, st, pr))
    vmem = 4 * s * V7X_LANES * 2 + n_blocks * (2 * 2 * tq * n_win * 4 + tq * (n_win + lx) * 24)
    return pl.pallas_call(
        functools.partial(_nbr_kernel, n_blocks=n_blocks, rows_total=rows_total),
        grid=(bsz, n_pairs, n_rblocks // n_blocks),
        in_specs=[
            qo_spec,
            pl.BlockSpec((None, s, V7X_LANES), lambda b, pr, st: (b, 0, pr)),
            pl.BlockSpec((None, s, V7X_LANES), lambda b, pr, st: (b, 0, pr)),
            pl.BlockSpec((None, lx, V7X_LANES), lambda b, pr, st: (b, 0, pr)),
            pl.BlockSpec((None, lx, V7X_LANES), lambda b, pr, st: (b, 0, pr)),
        ] + [bias_spec(j) for j in range(n_blocks)],
        out_specs=qo_spec,
        out_shape=jax.ShapeDtypeStruct(q.shape, BF16),
        compiler_params=_params(3, vmem),
        name="nbr_attn",
    )(q, k, v, kx, vx, *([bias] * n_blocks))


def _nbr_bias_tables(rpb, rows_total):
    assert rows_total % NBR_QROWS == 0 and rows_total >= NBR_KROWS + NBR_QROWS
    wr = min(WIN_R, rows_total)
    kinds = [(0, 0), (2 * NBR_QROWS, 2 * NBR_QROWS - WIN_R // 2),
             (rows_total - NBR_QROWS, rows_total - NBR_KROWS)]
    qi = np.arange(NBR_QROWS)[:, None, None, None]
    qc = np.arange(GRID_W)[None, :, None, None]
    kj = np.arange(NBR_KROWS)[None, None, :, None]
    kc = np.arange(GRID_W)[None, None, None, :]
    n_dr, n_dc = rpb.shape[1], rpb.shape[2]
    cs = np.clip(qc - WIN_C // 2, 0, GRID_W - WIN_C)
    col_ok = (kc >= cs) & (kc < cs + WIN_C)
    col_sel = (kc - qc + (WIN_C - 1))[..., None] == np.arange(n_dc)
    col_sel = (col_sel & col_ok[..., None])[0, :, 0].astype(np.float32)
    row_sel, ok_all = [], []
    for r0, k0 in kinds:
        r = r0 + qi
        rs = np.clip(r - wr // 2, 0, rows_total - wr)
        kr = k0 + kj
        row_ok = (kr >= rs) & (kr < rs + wr)
        sel = ((kr - r + (WIN_R - 1))[..., None] == np.arange(n_dr)) & row_ok[..., None]
        row_sel.append(sel[:, 0, :, 0].astype(np.float32))
        ok_all.append(np.broadcast_to(row_ok & col_ok, (NBR_QROWS, GRID_W, NBR_KROWS, GRID_W)))
    row_sel = jnp.asarray(np.stack(row_sel))
    ok = np.stack(ok_all).reshape(3, 1, NBR_QROWS * GRID_W, NBR_KROWS * GRID_W)
    hi = lax.Precision.HIGHEST
    rows_picked = jnp.einsum('hrd,tijr->thijd', rpb, row_sel, precision=hi)
    tab = jnp.einsum('thijd,cnd->thicjn', rows_picked, jnp.asarray(col_sel), precision=hi)
    tab = tab.reshape(3, rpb.shape[0], NBR_QROWS * GRID_W, NBR_KROWS * GRID_W)
    return jnp.where(ok, tab * LOG2_E, NEG).astype(F32)


def _oproj_kernel(h_ref, oa_ref, ob_ref, oc_ref, mod_ref, g_ref, w_ref, o_ref, *, n_sub):
    sub = h_ref.shape[0] // n_sub
    for t in range(n_sub):
        rows = slice(t * sub, (t + 1) * sub)
        o = jnp.concatenate([oa_ref[rows, :], ob_ref[rows, :], oc_ref[rows, :]], axis=1)
        y = jnp.dot(o, w_ref[...], preferred_element_type=F32)
        o_ref[rows, :] = h_ref[rows, :] + mod_ref[5:6, :] * _rms(y, g_ref[3:4, :])


def _oproj(h, oa, ob, oc, mods, norm_g, w_o, *, layer, mod_row, tm, n_sub=1):
    bsz, s, d = h.shape
    if mod_row is None:
        mod_map = lambda b, i: (layer, b, 0, 0)
    else:
        mod_map = lambda b, i: (layer, mod_row, 0, 0)
    tile = lambda w: pl.BlockSpec((None, tm, w), lambda b, i: (b, i, 0))
    vmem = w_o.shape[1] * d * 2 + tm * d * 4 * 8
    return pl.pallas_call(
        functools.partial(_oproj_kernel, n_sub=n_sub),
        grid=(bsz, s // tm),
        in_specs=[
            tile(d), tile(oa.shape[2]), tile(ob.shape[2]), tile(oc.shape[2]),
            pl.BlockSpec((None, None, N_MOD, d), mod_map),
            _resident((None, norm_g.shape[1], d), lambda b, i: (layer, 0, 0)),
            _resident((None, w_o.shape[1], d), lambda b, i: (layer, 0, 0)),
        ],
        out_specs=tile(d),
        out_shape=jax.ShapeDtypeStruct(h.shape, F32),
        compiler_params=_params(2, vmem),
        name="mix_out",
    )(h, oa, ob, oc, mods, norm_g, w_o)


def _rope_tables(seq):
    pos = jnp.arange(seq, dtype=jnp.int32)
    row = (pos // GRID_W).astype(F32)
    col = (pos % GRID_W).astype(F32)
    freq = 1.0 / (ROPE_THETA ** (jnp.arange(ROPE_FREQS, dtype=F32) / ROPE_FREQS))
    ar = row[:, None] * freq
    ac = col[:, None] * freq
    cos = jnp.concatenate([jnp.cos(ar), jnp.cos(ar), jnp.cos(ac), jnp.cos(ac)], axis=1)
    sin = jnp.concatenate([-jnp.sin(ar), jnp.sin(ar), -jnp.sin(ac), jnp.sin(ac)], axis=1)
    reps = V7X_LANES // HEAD_DIM
    return jnp.tile(cos, (1, reps)), jnp.tile(sin, (1, reps))


def _tile_rows(n, target):
    t = min(n, target)
    assert n % t == 0
    return t


def kernel(x, c, ctx, c_ctx, w_ada, b_ada, norm_g, w_in, qk_g, rpb, conv_w, w_o, ffn_wi, ffn_wo):
    bsz, seq, d = x.shape
    ctx_len = ctx.shape[1]
    depth = w_ada.shape[0]
    ctx_row = bsz

    mod_rows = -(-(bsz + 1) // 8) * 8
    c_all = jnp.zeros((mod_rows, d), F32).at[:bsz].set(c).at[ctx_row].set(c_ctx)
    mods = _ada(c_all, w_ada, b_ada).reshape(depth, mod_rows, N_MOD, d)

    w_in_b = w_in.astype(BF16)
    w_o_b = w_o.astype(BF16)
    wi_b = ffn_wi.astype(BF16)
    wo_b = ffn_wo.astype(BF16)
    qk_gain = jnp.concatenate(
        [jnp.tile(qk_g[:, 0], (1, A_Q_HEADS)), jnp.tile(qk_g[:, 1], (1, A_KV_HEADS))], axis=1
    ).reshape(depth, 1, A_Q_W + A_KV_W)
    gsz = (A_Q_W + A_KV_W) // 2
    head_of = np.arange(gsz) // HEAD_DIM
    gmat = jnp.asarray((head_of[:, None] == head_of[None, :]) / HEAD_DIM, dtype=BF16)
    lane_head = np.arange(V7X_LANES) // HEAD_DIM
    gsum = jnp.asarray(lane_head[:, None] == lane_head[None, :], dtype=BF16)
    rope_tabs = _rope_tables(seq)

    tm_ffn = _tile_rows(seq, 512)
    tm_proj = _tile_rows(seq, 1024)
    tm_ctx = _tile_rows(ctx_len, 256)
    tq_a = _tile_rows(seq, 256)
    tk_a = _tile_rows(seq, 512)
    unroll_a = min(GQA_UNROLL, seq // tk_a)
    n_rblocks = seq // (GRID_W * NBR_QROWS)
    nbr_blocks = NBR_BLOCKS_PER_STEP if n_rblocks % NBR_BLOCKS_PER_STEP == 0 else 1

    h, hc = x, ctx
    for layer in range(depth):
        last = layer == depth - 1
        lat = dict(layer=layer, mod_row=None, n_sub=ROW_SUBTILES)
        cx = dict(layer=layer, mod_row=ctx_row, n_sub=ROW_SUBTILES)
        h = _ffn(h, mods, norm_g, wi_b, wo_b, which=0, tm=tm_ffn, **lat)
        hc = _ffn(hc, mods, norm_g, wi_b, wo_b, which=0, tm=tm_ctx, **cx)
        qa, qb, ka, va, kb, vb, oc = _proj(h, mods, norm_g, w_in_b, qk_gain, conv_w, gmat, rope_tabs,
                                           tm=tm_proj, layer=layer, mod_row=None)
        cqa, cqb, cka, cva, ckb, cvb, coc = _proj(hc, mods, norm_g, w_in_b, qk_gain, conv_w, gmat, None,
                                                  tm=tm_ctx, layer=layer, mod_row=ctx_row)
        if not last:
            coa = _gqa(cqa, cka, cva, None, hq=A_Q_HEADS, hkv=A_KV_HEADS, tq=tm_ctx, tk=ctx_len)
            cob = _gqa(cqb, ckb, cvb, None, hq=B_HEADS, hkv=B_HEADS, tq=tm_ctx, tk=ctx_len)
            hc = _oproj(hc, coa, cob, coc, mods, norm_g, w_o_b, tm=tm_ctx, **cx)
        oa = _gqa_bounded(qa, ka, va, cka, cva, gsum, hq=A_Q_HEADS, tq=tq_a, tk=tk_a, unroll=unroll_a)
        ob = _nbr(qb, kb, vb, ckb, cvb, _nbr_bias_tables(rpb[layer], seq // GRID_W), n_blocks=nbr_blocks)
        h = _oproj(h, oa, ob, oc, mods, norm_g, w_o_b, tm=tm_proj, **lat)
        h = _ffn(h, mods, norm_g, wi_b, wo_b, which=1, tm=tm_ffn, **lat)
        if not last:
            hc = _ffn(hc, mods, norm_g, wi_b, wo_b, which=1, tm=tm_ctx, **cx)
    return h
```

```python
import functools

import numpy as np
import jax
import jax.numpy as jnp
from jax import lax
from jax.experimental import pallas as pl
from jax.experimental.pallas import tpu as pltpu

F32 = jnp.float32
BF16 = jnp.bfloat16

HEAD_DIM = 64
GRID_W = 64
A_Q_HEADS = 6
A_KV_HEADS = 2
B_HEADS = 6
C_WIDTH = 256
A_Q_W = A_Q_HEADS * HEAD_DIM
A_KV_W = A_KV_HEADS * HEAD_DIM
B_W = B_HEADS * HEAD_DIM
WIN_R = 8
WIN_C = 16
ROPE_FREQS = HEAD_DIM // 4
ROPE_THETA = 10000.0
N_MOD = 9
EPS = 1e-6
NEG = -1e30
LOG2_E = 1.4426950408889634

V7X_LANES = 128
V7X_SCOPED_VMEM_BYTES = 60000 * 1024

NBR_QROWS = 4
NBR_KROWS = NBR_QROWS + WIN_R

GQA_UNROLL = 4
ROW_SUBTILES = 4
NBR_BLOCKS_PER_STEP = 4


def _vmem_limit(estimate_bytes):
    return int(min(max(estimate_bytes, 16 * 1024 * 1024), V7X_SCOPED_VMEM_BYTES))


def _params(n_axes, vmem_bytes):
    return pltpu.CompilerParams(
        dimension_semantics=("arbitrary",) * n_axes,
        vmem_limit_bytes=_vmem_limit(vmem_bytes),
    )


def _rms(x, g):
    ms = jnp.mean(x * x, axis=-1, keepdims=True)
    return x * lax.rsqrt(ms + EPS) * g


def _resident(block_shape, index_map):
    return pl.BlockSpec(block_shape, index_map, pipeline_mode=pl.Buffered(1))


def _ada_kernel(c_ref, w_ref, b_ref, o_ref):
    c = c_ref[...]
    sc = c * jax.nn.sigmoid(c)
    o_ref[...] = jnp.dot(sc, w_ref[...], preferred_element_type=F32,
                         precision=lax.Precision.HIGHEST) + b_ref[...]


def _ada(c_all, w_ada, b_ada):
    depth, d, n = w_ada.shape
    rows = c_all.shape[0]
    tn = d
    return pl.pallas_call(
        _ada_kernel,
        grid=(depth, n // tn),
        in_specs=[
            pl.BlockSpec((rows, d), lambda l, j: (0, 0)),
            pl.BlockSpec((None, d, tn), lambda l, j: (l, 0, j)),
            pl.BlockSpec((None, 1, tn), lambda l, j: (l, 0, j)),
        ],
        out_specs=pl.BlockSpec((None, rows, tn), lambda l, j: (l, 0, j)),
        out_shape=jax.ShapeDtypeStruct((depth, rows, n), F32),
        compiler_params=_params(2, 4 * d * tn * 4),
        name="ada_mod",
    )(c_all, w_ada, b_ada.reshape(depth, 1, n))


def _ffn_kernel(*refs, i0, gi, ffn_dim, n_sub, mix):
    if mix:
        h_ref, oa_ref, ob_ref, oc_ref, mod_ref, g_ref, wmix_ref, wi_ref, wo_ref, o_ref = refs
    else:
        h_ref, mod_ref, g_ref, wi_ref, wo_ref, o_ref = refs
    shift = mod_ref[i0:i0 + 1, :]
    scale = mod_ref[i0 + 1:i0 + 2, :]
    gate = mod_ref[i0 + 2:i0 + 3, :]
    sub = h_ref.shape[0] // n_sub
    for t in range(n_sub):
        rows = slice(t * sub, (t + 1) * sub)
        h = h_ref[rows, :]
        if mix:
            o = jnp.concatenate([oa_ref[rows, :], ob_ref[rows, :], oc_ref[rows, :]], axis=1)
            ymix = jnp.dot(o, wmix_ref[...], preferred_element_type=F32)
            h = h + mod_ref[5:6, :] * _rms(ymix, g_ref[3:4, :])
        u = _rms(h, g_ref[gi:gi + 1, :]) * (1.0 + scale) + shift
        hid = jnp.dot(u.astype(BF16), wi_ref[...], preferred_element_type=F32)
        gt = hid[:, :ffn_dim]
        up = hid[:, ffn_dim:]
        act = (gt * jax.nn.sigmoid(gt) * up).astype(BF16)
        y = jnp.dot(act, wo_ref[...], preferred_element_type=F32)
        o_ref[rows, :] = h + 0.5 * gate * _rms(y, g_ref[gi + 1:gi + 2, :])


def _ffn(h, mods, norm_g, wi, wo, *, layer, which, mod_row, tm, n_sub=1, mix=None):
    bsz, s, d = h.shape
    ffn_dim = wo.shape[2]
    i0 = 6 * which
    gi = 4 * which
    if mod_row is None:
        mod_map = lambda b, i: (layer, b, 0, 0)
    else:
        mod_map = lambda b, i: (layer, mod_row, 0, 0)
    tile = lambda w: pl.BlockSpec((None, tm, w), lambda b, i: (b, i, 0))
    vmem = ((wi.shape[2] * wi.shape[3] + wo.shape[2] * wo.shape[3]) * 2 + tm * d * 4 * 6
            + (tm // n_sub) * ffn_dim * 24)
    in_specs = [tile(d)]
    args = [h]
    if mix is not None:
        in_specs += [tile(o.shape[2]) for o in mix[:3]]
        args += list(mix[:3])
    in_specs += [pl.BlockSpec((None, None, N_MOD, d), mod_map),
                 _resident((None, norm_g.shape[1], d), lambda b, i: (layer, 0, 0))]
    args += [mods, norm_g]
    if mix is not None:
        w_mix = mix[3]
        in_specs.append(_resident((None, w_mix.shape[1], d), lambda b, i: (layer, 0, 0)))
        args.append(w_mix)
        vmem += w_mix.shape[1] * d * 2 + tm * d * 2 * 2
    in_specs += [_resident((None, None, d, 2 * ffn_dim), lambda b, i: (layer, which, 0, 0)),
                 _resident((None, None, ffn_dim, d), lambda b, i: (layer, which, 0, 0))]
    args += [wi, wo]
    return pl.pallas_call(
        functools.partial(_ffn_kernel, i0=i0, gi=gi, ffn_dim=ffn_dim, n_sub=n_sub, mix=mix is not None),
        grid=(bsz, s // tm),
        in_specs=in_specs,
        out_specs=tile(d),
        out_shape=jax.ShapeDtypeStruct(h.shape, F32),
        compiler_params=_params(2, vmem),
        name="mix_out_ffn" if mix is not None else "ffn",
    )(*args)


def _swap_rope_partners(x):
    lane = lax.broadcasted_iota(jnp.int32, x.shape, 1)
    first = (lane % (2 * ROPE_FREQS)) < ROPE_FREQS
    return jnp.where(first,
                     pltpu.roll(x, V7X_LANES - ROPE_FREQS, axis=1),
                     pltpu.roll(x, ROPE_FREQS, axis=1))


def _proj_kernel(*refs, rope, tm, n_tiles):
    if rope:
        (h_ref, hp_ref, hn_ref, mod_ref, g_ref, w_ref, qkg_ref, cw_ref, gm_ref, cos_ref, sin_ref,
         qa_ref, qb_ref, ka_ref, va_ref, kb_ref, vb_ref, oc_ref) = refs
    else:
        (h_ref, hp_ref, hn_ref, mod_ref, g_ref, w_ref, qkg_ref, cw_ref, gm_ref,
         qa_ref, qb_ref, ka_ref, va_ref, kb_ref, vb_ref, oc_ref) = refs
    i = pl.program_id(1)
    shift = mod_ref[3:4, :]
    scale = mod_ref[4:5, :]
    g2 = g_ref[2:3, :]

    def pre(x):
        return (_rms(x, g2) * (1.0 + scale) + shift).astype(BF16)

    p = jnp.dot(pre(h_ref[...]), w_ref[...], preferred_element_type=F32)
    o_qa, o_qb = 0, A_Q_W
    o_ka = o_qb + B_W
    o_va = o_ka + A_KV_W
    o_kb = o_va + A_KV_W
    o_vb = o_kb + B_W
    o_cx = o_vb + B_W
    o_cb = o_cx + C_WIDTH
    o_cc = o_cb + C_WIDTH

    xq = jnp.concatenate([p[:, o_qa:o_qa + A_Q_W], p[:, o_ka:o_ka + A_KV_W]], axis=1)
    sq = (xq * xq).astype(BF16)
    half = (A_Q_W + A_KV_W) // 2
    ms = jnp.concatenate(
        [jnp.dot(sq[:, :half], gm_ref[...], preferred_element_type=F32),
         jnp.dot(sq[:, half:], gm_ref[...], preferred_element_type=F32)], axis=1)
    xn = xq * lax.rsqrt(ms + EPS) * qkg_ref[...]
    if rope:
        cos = cos_ref[...]
        sin = sin_ref[...]
        cols = []
        for j in range((A_Q_W + A_KV_W) // V7X_LANES):
            xc = xn[:, j * V7X_LANES:(j + 1) * V7X_LANES]
            cols.append(xc * cos + _swap_rope_partners(xc) * sin)
        xn = jnp.concatenate(cols, axis=1)
    q_scale = HEAD_DIM ** -0.5 * LOG2_E
    qa_ref[...] = (xn[:, :A_Q_W] * q_scale).astype(BF16)
    ka_ref[...] = xn[:, A_Q_W:].astype(BF16)
    qb_ref[...] = (p[:, o_qb:o_qb + B_W] * q_scale).astype(BF16)
    va_ref[...] = p[:, o_va:o_va + A_KV_W].astype(BF16)
    kb_ref[...] = p[:, o_kb:o_kb + B_W].astype(BF16)
    vb_ref[...] = p[:, o_vb:o_vb + B_W].astype(BF16)

    z = p[:, o_cc:o_cc + C_WIDTH] * p[:, o_cx:o_cx + C_WIDTH]

    def halo(ref):
        ub = pre(ref[...])
        return (jnp.dot(ub, w_ref[:, o_cc:o_cc + C_WIDTH], preferred_element_type=F32)
                * jnp.dot(ub, w_ref[:, o_cx:o_cx + C_WIDTH], preferred_element_type=F32))

    halo_rows = hp_ref.shape[0]
    z_before = jnp.where(i > 0, halo(hp_ref)[halo_rows - 1:halo_rows, :], 0.0)
    z_after = jnp.where(i < n_tiles - 1, halo(hn_ref)[0:1, :], 0.0)
    row = lax.broadcasted_iota(jnp.int32, z.shape, 0)
    z_m1 = jnp.where(row == 0, z_before, pltpu.roll(z, 1, axis=0))
    z_p1 = jnp.where(row == tm - 1, z_after, pltpu.roll(z, tm - 1, axis=0))
    y = cw_ref[0:1, :] * z_m1 + cw_ref[1:2, :] * z + cw_ref[2:3, :] * z_p1
    oc_ref[...] = (p[:, o_cb:o_cb + C_WIDTH] * y).astype(BF16)


def _proj(h, mods, norm_g, w_in, qk_gain, conv_w, gmat, rope_tabs, *, layer, mod_row, tm):
    bsz, s, d = h.shape
    n_tiles = s // tm
    halo_rows = 8
    hb = tm // halo_rows
    n_hblk = s // halo_rows
    rope = rope_tabs is not None
    if mod_row is None:
        mod_map = lambda b, i: (layer, b, 0, 0)
    else:
        mod_map = lambda b, i: (layer, mod_row, 0, 0)
    pw = w_in.shape[2]
    in_specs = [
        pl.BlockSpec((None, tm, d), lambda b, i: (b, i, 0)),
        pl.BlockSpec((None, halo_rows, d), lambda b, i: (b, jnp.maximum(i * hb - 1, 0), 0)),
        pl.BlockSpec((None, halo_rows, d), lambda b, i: (b, jnp.minimum((i + 1) * hb, n_hblk - 1), 0)),
        pl.BlockSpec((None, None, N_MOD, d), mod_map),
        _resident((None, norm_g.shape[1], d), lambda b, i: (layer, 0, 0)),
        _resident((None, d, pw), lambda b, i: (layer, 0, 0)),
        _resident((None, 1, A_Q_W + A_KV_W), lambda b, i: (layer, 0, 0)),
        _resident((None, conv_w.shape[1], C_WIDTH), lambda b, i: (layer, 0, 0)),
        _resident(gmat.shape, lambda b, i: (0, 0)),
    ]
    args = [h, h, h, mods, norm_g, w_in, qk_gain, conv_w, gmat]
    if rope:
        in_specs += [pl.BlockSpec((tm, V7X_LANES), lambda b, i: (i, 0))] * 2
        args += list(rope_tabs)
    widths = (A_Q_W, B_W, A_KV_W, A_KV_W, B_W, B_W, C_WIDTH)
    out_specs = [pl.BlockSpec((None, tm, w), lambda b, i: (b, i, 0)) for w in widths]
    out_shape = [jax.ShapeDtypeStruct((bsz, s, w), BF16) for w in widths]
    vmem = d * pw * 2 + tm * d * 4 * 4 + tm * pw * 12
    return pl.pallas_call(
        functools.partial(_proj_kernel, rope=rope, tm=tm, n_tiles=n_tiles),
        grid=(bsz, n_tiles),
        in_specs=in_specs,
        out_specs=out_specs,
        out_shape=out_shape,
        compiler_params=_params(2, vmem),
        name="mix_proj",
    )(*args)


_NT_DIMS = (((1,), (1,)), ((), ()))

SCORE_BOUND_MAX = 40.0
SCORE_BOUND_PIVOT = 16.0


def _load_q_rows(q_ref, qs_sc, heads, grp, tq):
    lane = lax.broadcasted_iota(jnp.int32, (tq, V7X_LANES), 1)
    low_half = lane < HEAD_DIM
    for j, h in enumerate(heads):
        xc = q_ref[:, (h // 2) * V7X_LANES:(h // 2 + 1) * V7X_LANES].astype(F32)
        dst_low = (h // grp) % 2 == 0
        if (h % 2 == 0) != dst_low:
            xc = pltpu.roll(xc, HEAD_DIM, axis=1)
        keep = low_half if dst_low else jnp.logical_not(low_half)
        qs_sc[j * tq:(j + 1) * tq, :] = jnp.where(keep, xc, 0.0).astype(BF16)


def _repack_heads(on_sc, o_ref, hq, grp, tq):
    lane = lax.broadcasted_iota(jnp.int32, (tq, V7X_LANES), 1)
    low_half = lane < HEAD_DIM
    for oc in range(hq // 2):
        pieces = []
        for e in range(2):
            h = 2 * oc + e
            piece = on_sc[h * tq:(h + 1) * tq, :]
            src_low = (h // grp) % 2 == 0
            if src_low != (e == 0):
                piece = pltpu.roll(piece, HEAD_DIM, axis=1)
            pieces.append(piece)
        o_ref[:, oc * V7X_LANES:(oc + 1) * V7X_LANES] = jnp.where(low_half, pieces[0], pieces[1]).astype(BF16)


def _online_softmax(qs_sc, m_sc, l_sc, acc_sc, rows, chunks, extra, *, tk, n_chunks, unroll):
    m_sc[0:rows, :] = jnp.full((rows, V7X_LANES), NEG, F32)
    l_sc[0:rows, :] = jnp.zeros((rows, V7X_LANES), F32)
    acc_sc[0:rows, :] = jnp.zeros((rows, V7X_LANES), F32)

    def load_state():
        return m_sc[0:rows, :], l_sc[0:rows, :], acc_sc[0:rows, :]

    def store_state(state):
        m_sc[0:rows, :], l_sc[0:rows, :], acc_sc[0:rows, :] = state

    def step(state, kc, vc):
        m_prev, l_prev, acc_prev = state
        s = lax.dot_general(qs_sc[0:rows, :], kc, _NT_DIMS, preferred_element_type=F32)
        m_next = jnp.maximum(m_prev, jnp.max(s, axis=1, keepdims=True))
        alpha = jnp.exp2(m_prev - m_next)
        p = jnp.exp2(s - jnp.concatenate([m_next] * (kc.shape[0] // V7X_LANES), axis=1))
        l_next = alpha * l_prev + jnp.sum(p, axis=1, keepdims=True)
        acc_next = alpha * acc_prev + jnp.dot(p.astype(BF16), vc, preferred_element_type=F32)
        return m_next, l_next, acc_next

    def body(c, carry):
        state = load_state()
        for u in range(unroll):
            state = step(state, *chunks(pl.multiple_of((c * unroll + u) * tk, tk)))
        store_state(state)
        return carry

    lax.fori_loop(0, n_chunks // unroll, body, 0)
    if extra is not None:
        store_state(step(load_state(), *extra))
    return acc_sc[0:rows, :] / l_sc[0:rows, :]


def _bounded_softmax(qs_sc, b_sc, l_sc, acc_sc, rows, chunks, extra, *, tk, n_chunks, unroll):
    l_sc[0:rows, :] = jnp.zeros((rows, V7X_LANES), F32)
    acc_sc[0:rows, :] = jnp.zeros((rows, V7X_LANES), F32)

    def step(state, kc, vc):
        l_prev, acc_prev = state
        n_cols = kc.shape[0] // V7X_LANES
        s = lax.dot_general(qs_sc[0:rows, :], kc, _NT_DIMS, preferred_element_type=F32)
        p = jnp.exp2(s - jnp.concatenate([b_sc[0:rows, :]] * n_cols, axis=1))
        l_next = l_prev
        for j in range(n_cols):
            l_next = l_next + p[:, j * V7X_LANES:(j + 1) * V7X_LANES]
        acc_next = acc_prev + jnp.dot(p.astype(BF16), vc, preferred_element_type=F32)
        return l_next, acc_next

    def body(c, carry):
        state = (l_sc[0:rows, :], acc_sc[0:rows, :])
        for u in range(unroll):
            state = step(state, *chunks(pl.multiple_of((c * unroll + u) * tk, tk)))
        l_sc[0:rows, :], acc_sc[0:rows, :] = state
        return carry

    n_trips = n_chunks // unroll
    lax.fori_loop(0, n_trips - 1, body, 0)
    state = (l_sc[0:rows, :], acc_sc[0:rows, :])
    for u in range(unroll):
        state = step(state, *chunks(((n_trips - 1) * unroll + u) * tk))
    if extra is not None:
        state = step(state, *extra)
    l_lanes, acc = state
    return acc / jnp.sum(l_lanes, axis=1, keepdims=True)


def _gqa_kernel(*refs, hq, hkv, tq, tk, n_chunks, unroll, extra_len):
    if extra_len:
        q_ref, k_ref, v_ref, kx_ref, vx_ref, o_ref, qs_sc, m_sc, l_sc, acc_sc, on_sc = refs
    else:
        q_ref, k_ref, v_ref, o_ref, qs_sc, m_sc, l_sc, acc_sc, on_sc = refs
    grp = hq // hkv
    for pc in range(hkv // 2):
        heads = [h for h in range(hq) if (h // grp) // 2 == pc]
        rows = len(heads) * tq
        col = slice(pc * V7X_LANES, (pc + 1) * V7X_LANES)
        _load_q_rows(q_ref, qs_sc, heads, grp, tq)
        chunks = lambda start: (k_ref[pl.ds(start, tk), col], v_ref[pl.ds(start, tk), col])
        extra = (kx_ref[:, col], vx_ref[:, col]) if extra_len else None
        on = _online_softmax(qs_sc, m_sc, l_sc, acc_sc, rows, chunks, extra,
                             tk=tk, n_chunks=n_chunks, unroll=unroll)
        for j, h in enumerate(heads):
            on_sc[h * tq:(h + 1) * tq, :] = on[j * tq:(j + 1) * tq, :]
    _repack_heads(on_sc, o_ref, hq, grp, tq)


def _gqa_bounded_kernel(q_ref, k_ref, v_ref, kx_ref, vx_ref, gs_ref, o_ref,
                        qs_sc, m_sc, l_sc, acc_sc, on_sc, b_sc, kmax_sc, *, hq, tq, tk, n_chunks, unroll):
    hkv = 2
    grp = hq // hkv
    rows = hq * tq

    def max_sq_norm(kc, mx):
        kf = kc.astype(F32)
        ss = lax.dot_general(gs_ref[...], (kf * kf).astype(BF16), _NT_DIMS, preferred_element_type=F32)
        return jnp.maximum(mx, jnp.max(ss, axis=1, keepdims=True))

    @pl.when(pl.program_id(1) == 0)
    def _():
        def kbody(c, mx):
            return max_sq_norm(k_ref[pl.ds(pl.multiple_of(c * tk, tk), tk), :], mx)
        mx = lax.fori_loop(0, n_chunks, kbody, jnp.zeros((V7X_LANES, 1), F32))
        kmax_sc[...] = jnp.broadcast_to(max_sq_norm(kx_ref[...], mx), kmax_sc.shape).astype(BF16)

    _load_q_rows(q_ref, qs_sc, list(range(hq)), grp, tq)
    qf = qs_sc[...].astype(F32)
    qk_sq = jnp.dot((qf * qf).astype(BF16), kmax_sc[...], preferred_element_type=F32)
    bound = (qk_sq * (0.5 / SCORE_BOUND_PIVOT) + 0.5 * SCORE_BOUND_PIVOT) * (1.0 + 2.0 ** -5)
    b_sc[...] = bound
    bounded_ok = jnp.max(bound) <= SCORE_BOUND_MAX

    chunks = lambda start: (k_ref[pl.ds(start, tk), :], v_ref[pl.ds(start, tk), :])
    extra = (kx_ref[...], vx_ref[...])

    @pl.when(bounded_ok)
    def _():
        on_sc[...] = _bounded_softmax(qs_sc, b_sc, l_sc, acc_sc, rows, chunks, extra,
                                      tk=tk, n_chunks=n_chunks, unroll=unroll)

    @pl.when(jnp.logical_not(bounded_ok))
    def _():
        on_sc[...] = _online_softmax(qs_sc, m_sc, l_sc, acc_sc, rows, chunks, extra,
                                     tk=tk, n_chunks=n_chunks, unroll=unroll)

    _repack_heads(on_sc, o_ref, hq, grp, tq)


def _gqa(q, k, v, extra, *, hq, hkv, tq, tk, unroll=1):
    bsz, s, qw = q.shape
    t = k.shape[1]
    kw = k.shape[2]
    n_chunks = t // tk
    assert n_chunks % unroll == 0
    grp = hq // hkv
    max_heads = max(sum(1 for h in range(hq) if (h // grp) // 2 == pc) for pc in range(hkv // 2))
    rows = max_heads * tq
    in_specs = [
        pl.BlockSpec((None, tq, qw), lambda b, i: (b, i, 0)),
        pl.BlockSpec((None, t, kw), lambda b, i: (b, 0, 0)),
        pl.BlockSpec((None, t, kw), lambda b, i: (b, 0, 0)),
    ]
    args = [q, k, v]
    extra_len = 0
    if extra is not None:
        extra_len = extra[0].shape[1]
        in_specs += [pl.BlockSpec((None, extra_len, kw), lambda b, i: (b, 0, 0))] * 2
        args += list(extra)
    vmem = 4 * t * kw * 2 + rows * max(tk, extra_len) * 16 + rows * V7X_LANES * 24 + 8 * tq * qw * 2
    return pl.pallas_call(
        functools.partial(_gqa_kernel, hq=hq, hkv=hkv, tq=tq, tk=tk, n_chunks=n_chunks, unroll=unroll,
                          extra_len=extra_len),
        grid=(bsz, s // tq),
        in_specs=in_specs,
        out_specs=pl.BlockSpec((None, tq, qw), lambda b, i: (b, i, 0)),
        out_shape=jax.ShapeDtypeStruct(q.shape, BF16),
        scratch_shapes=[
            pltpu.VMEM((rows, V7X_LANES), BF16),
            pltpu.VMEM((rows, V7X_LANES), F32),
            pltpu.VMEM((rows, V7X_LANES), F32),
            pltpu.VMEM((rows, V7X_LANES), F32),
            pltpu.VMEM((hq * tq, V7X_LANES), F32),
        ],
        compiler_params=_params(2, vmem),
        name="gqa_attn",
    )(*args)


def _gqa_bounded(q, k, v, kx, vx, gsum, *, hq, tq, tk, unroll):
    bsz, s, qw = q.shape
    t, kw = k.shape[1], k.shape[2]
    lx = kx.shape[1]
    assert kw == V7X_LANES and t % (tk * unroll) == 0
    rows = hq * tq
    stat = pltpu.VMEM((rows, V7X_LANES), F32)
    vmem = 4 * (t + lx) * kw * 2 + rows * max(tk, lx) * 8 * unroll + rows * V7X_LANES * 32 + 8 * tq * qw * 2
    return pl.pallas_call(
        functools.partial(_gqa_bounded_kernel, hq=hq, tq=tq, tk=tk, n_chunks=t // tk, unroll=unroll),
        grid=(bsz, s // tq),
        in_specs=[
            pl.BlockSpec((None, tq, qw), lambda b, i: (b, i, 0)),
            pl.BlockSpec((None, t, kw), lambda b, i: (b, 0, 0)),
            pl.BlockSpec((None, t, kw), lambda b, i: (b, 0, 0)),
            pl.BlockSpec((None, lx, kw), lambda b, i: (b, 0, 0)),
            pl.BlockSpec((None, lx, kw), lambda b, i: (b, 0, 0)),
            _resident(gsum.shape, lambda b, i: (0, 0)),
        ],
        out_specs=pl.BlockSpec((None, tq, qw), lambda b, i: (b, i, 0)),
        out_shape=jax.ShapeDtypeStruct(q.shape, BF16),
        scratch_shapes=[
            pltpu.VMEM((rows, V7X_LANES), BF16), stat, stat, stat, stat, stat,
            pltpu.VMEM((V7X_LANES, V7X_LANES), BF16),
        ],
        compiler_params=_params(2, vmem),
        name="gqa_attn_lat",
    )(q, k, v, kx, vx, gsum)


def _nbr_kernel(q_ref, k_ref, v_ref, kx_ref, vx_ref, *rest, n_blocks, rows_total):
    bias_refs, o_ref = rest[:n_blocks], rest[n_blocks]
    tq = NBR_QROWS * GRID_W
    n_win = NBR_KROWS * GRID_W
    kx = kx_ref[...]
    vx = vx_ref[...]
    lane = lax.broadcasted_iota(jnp.int32, (tq, V7X_LANES), 1)
    low_half = lane < HEAD_DIM
    for j in range(n_blocks):
        rb = pl.program_id(2) * n_blocks + j
        krow0 = jnp.clip(rb * NBR_QROWS - WIN_R // 2, 0, rows_total - NBR_KROWS)
        start = pl.multiple_of(krow0 * GRID_W, NBR_QROWS * GRID_W)
        kw = k_ref[pl.ds(start, n_win), :]
        vw = v_ref[pl.ds(start, n_win), :]
        q = q_ref[j * tq:(j + 1) * tq, :]
        zero = jnp.zeros_like(q)
        qs = jnp.concatenate([jnp.where(low_half, q, zero), jnp.where(low_half, zero, q)], axis=0)
        bias = bias_refs[j][...].reshape(2 * tq, n_win)
        s_win = lax.dot_general(qs, kw, _NT_DIMS, preferred_element_type=F32) + bias
        s_ctx = lax.dot_general(qs, kx, _NT_DIMS, preferred_element_type=F32)
        m = jnp.maximum(jnp.max(s_win, axis=1, keepdims=True), jnp.max(s_ctx, axis=1, keepdims=True))
        p_win = jnp.exp2(s_win - m)
        p_ctx = jnp.exp2(s_ctx - m)
        l = jnp.sum(p_win, axis=1, keepdims=True) + jnp.sum(p_ctx, axis=1, keepdims=True)
        o = (jnp.dot(p_win.astype(BF16), vw, preferred_element_type=F32)
             + jnp.dot(p_ctx.astype(BF16), vx, preferred_element_type=F32)) / l
        o_ref[j * tq:(j + 1) * tq, :] = jnp.where(low_half, o[:tq], o[tq:]).astype(BF16)


def _nbr(q, k, v, kx, vx, bias, *, n_blocks=1):
    bsz, s, w = q.shape
    rows_total = s // GRID_W
    n_rblocks = rows_total // NBR_QROWS
    assert n_rblocks % n_blocks == 0
    tq = NBR_QROWS * GRID_W
    n_win = NBR_KROWS * GRID_W
    lx = kx.shape[1]
    n_pairs = w // V7X_LANES

    def bias_spec(j):
        def bias_map(b, pr, st):
            rb = st * n_blocks + j
            kind = jnp.where(rb == 0, 0, jnp.where(rb == n_rblocks - 1, 2, 1))
            return (kind, pr, 0, 0)
        return pl.BlockSpec((None, 2, tq, n_win), bias_map)

    qo_spec = pl.BlockSpec((None, n_blocks * tq, V7X_LANES), lambda b, pr, st: (b, st, pr))
    vmem = 4 * s * V7X_LANES * 2 + n_blocks * (2 * 2 * tq * n_win * 4 + tq * (n_win + lx) * 24)
    return pl.pallas_call(
        functools.partial(_nbr_kernel, n_blocks=n_blocks, rows_total=rows_total),
        grid=(bsz, n_pairs, n_rblocks // n_blocks),
        in_specs=[
            qo_spec,
            pl.BlockSpec((None, s, V7X_LANES), lambda b, pr, st: (b, 0, pr)),
            pl.BlockSpec((None, s, V7X_LANES), lambda b, pr, st: (b, 0, pr)),
            pl.BlockSpec((None, lx, V7X_LANES), lambda b, pr, st: (b, 0, pr)),
            pl.BlockSpec((None, lx, V7X_LANES), lambda b, pr, st: (b, 0, pr)),
        ] + [bias_spec(j) for j in range(n_blocks)],
        out_specs=qo_spec,
        out_shape=jax.ShapeDtypeStruct(q.shape, BF16),
        compiler_params=_params(3, vmem),
        name="nbr_attn",
    )(q, k, v, kx, vx, *([bias] * n_blocks))


def _nbr_bias_tables(rpb, rows_total):
    assert rows_total % NBR_QROWS == 0 and rows_total >= NBR_KROWS + NBR_QROWS
    wr = min(WIN_R, rows_total)
    kinds = [(0, 0), (2 * NBR_QROWS, 2 * NBR_QROWS - WIN_R // 2),
             (rows_total - NBR_QROWS, rows_total - NBR_KROWS)]
    qi = np.arange(NBR_QROWS)[:, None, None, None]
    qc = np.arange(GRID_W)[None, :, None, None]
    kj = np.arange(NBR_KROWS)[None, None, :, None]
    kc = np.arange(GRID_W)[None, None, None, :]
    n_dr, n_dc = rpb.shape[1], rpb.shape[2]
    cs = np.clip(qc - WIN_C // 2, 0, GRID_W - WIN_C)
    col_ok = (kc >= cs) & (kc < cs + WIN_C)
    col_sel = (kc - qc + (WIN_C - 1))[..., None] == np.arange(n_dc)
    col_sel = (col_sel & col_ok[..., None])[0, :, 0].astype(np.float32)
    row_sel, ok_all = [], []
    for r0, k0 in kinds:
        r = r0 + qi
        rs = np.clip(r - wr // 2, 0, rows_total - wr)
        kr = k0 + kj
        row_ok = (kr >= rs) & (kr < rs + wr)
        sel = ((kr - r + (WIN_R - 1))[..., None] == np.arange(n_dr)) & row_ok[..., None]
        row_sel.append(sel[:, 0, :, 0].astype(np.float32))
        ok_all.append(np.broadcast_to(row_ok & col_ok, (NBR_QROWS, GRID_W, NBR_KROWS, GRID_W)))
    row_sel = jnp.asarray(np.stack(row_sel))
    ok = np.stack(ok_all).reshape(3, 1, NBR_QROWS * GRID_W, NBR_KROWS * GRID_W)
    hi = lax.Precision.HIGHEST
    rows_picked = jnp.einsum('hrd,tijr->thijd', rpb, row_sel, precision=hi)
    tab = jnp.einsum('thijd,cnd->thicjn', rows_picked, jnp.asarray(col_sel), precision=hi)
    tab = tab.reshape(3, rpb.shape[0], NBR_QROWS * GRID_W, NBR_KROWS * GRID_W)
    return jnp.where(ok, tab * LOG2_E, NEG).astype(F32)


def _oproj_kernel(h_ref, oa_ref, ob_ref, oc_ref, mod_ref, g_ref, w_ref, o_ref, *, n_sub):
    sub = h_ref.shape[0] // n_sub
    for t in range(n_sub):
        rows = slice(t * sub, (t + 1) * sub)
        o = jnp.concatenate([oa_ref[rows, :], ob_ref[rows, :], oc_ref[rows, :]], axis=1)
        y = jnp.dot(o, w_ref[...], preferred_element_type=F32)
        o_ref[rows, :] = h_ref[rows, :] + mod_ref[5:6, :] * _rms(y, g_ref[3:4, :])


def _oproj(h, oa, ob, oc, mods, norm_g, w_o, *, layer, mod_row, tm, n_sub=1):
    bsz, s, d = h.shape
    if mod_row is None:
        mod_map = lambda b, i: (layer, b, 0, 0)
    else:
        mod_map = lambda b, i: (layer, mod_row, 0, 0)
    tile = lambda w: pl.BlockSpec((None, tm, w), lambda b, i: (b, i, 0))
    vmem = w_o.shape[1] * d * 2 + tm * d * 4 * 8
    return pl.pallas_call(
        functools.partial(_oproj_kernel, n_sub=n_sub),
        grid=(bsz, s // tm),
        in_specs=[
            tile(d), tile(oa.shape[2]), tile(ob.shape[2]), tile(oc.shape[2]),
            pl.BlockSpec((None, None, N_MOD, d), mod_map),
            _resident((None, norm_g.shape[1], d), lambda b, i: (layer, 0, 0)),
            _resident((None, w_o.shape[1], d), lambda b, i: (layer, 0, 0)),
        ],
        out_specs=tile(d),
        out_shape=jax.ShapeDtypeStruct(h.shape, F32),
        compiler_params=_params(2, vmem),
        name="mix_out",
    )(h, oa, ob, oc, mods, norm_g, w_o)


def _rope_tables(seq):
    pos = jnp.arange(seq, dtype=jnp.int32)
    row = (pos // GRID_W).astype(F32)
    col = (pos % GRID_W).astype(F32)
    freq = 1.0 / (ROPE_THETA ** (jnp.arange(ROPE_FREQS, dtype=F32) / ROPE_FREQS))
    ar = row[:, None] * freq
    ac = col[:, None] * freq
    cos = jnp.concatenate([jnp.cos(ar), jnp.cos(ar), jnp.cos(ac), jnp.cos(ac)], axis=1)
    sin = jnp.concatenate([-jnp.sin(ar), jnp.sin(ar), -jnp.sin(ac), jnp.sin(ac)], axis=1)
    reps = V7X_LANES // HEAD_DIM
    return jnp.tile(cos, (1, reps)), jnp.tile(sin, (1, reps))


def _tile_rows(n, target):
    t = min(n, target)
    assert n % t == 0
    return t


def kernel(x, c, ctx, c_ctx, w_ada, b_ada, norm_g, w_in, qk_g, rpb, conv_w, w_o, ffn_wi, ffn_wo):
    bsz, seq, d = x.shape
    ctx_len = ctx.shape[1]
    depth = w_ada.shape[0]
    ctx_row = bsz

    mod_rows = -(-(bsz + 1) // 8) * 8
    c_all = jnp.zeros((mod_rows, d), F32).at[:bsz].set(c).at[ctx_row].set(c_ctx)
    mods = _ada(c_all, w_ada, b_ada).reshape(depth, mod_rows, N_MOD, d)

    w_in_b = w_in.astype(BF16)
    w_o_b = w_o.astype(BF16)
    wi_b = ffn_wi.astype(BF16)
    wo_b = ffn_wo.astype(BF16)
    qk_gain = jnp.concatenate(
        [jnp.tile(qk_g[:, 0], (1, A_Q_HEADS)), jnp.tile(qk_g[:, 1], (1, A_KV_HEADS))], axis=1
    ).reshape(depth, 1, A_Q_W + A_KV_W)
    gsz = (A_Q_W + A_KV_W) // 2
    head_of = np.arange(gsz) // HEAD_DIM
    gmat = jnp.asarray((head_of[:, None] == head_of[None, :]) / HEAD_DIM, dtype=BF16)
    lane_head = np.arange(V7X_LANES) // HEAD_DIM
    gsum = jnp.asarray(lane_head[:, None] == lane_head[None, :], dtype=BF16)
    rope_tabs = _rope_tables(seq)

    tm_ffn = _tile_rows(seq, 512)
    tm_proj = _tile_rows(seq, 1024)
    tm_ctx = _tile_rows(ctx_len, 256)
    tq_a = _tile_rows(seq, 256)
    tk_a = _tile_rows(seq, 512)
    unroll_a = min(GQA_UNROLL, seq // tk_a)
    n_rblocks = seq // (GRID_W * NBR_QROWS)
    nbr_blocks = NBR_BLOCKS_PER_STEP if n_rblocks % NBR_BLOCKS_PER_STEP == 0 else 1

    h, hc = x, ctx
    for layer in range(depth):
        last = layer == depth - 1
        lat = dict(layer=layer, mod_row=None, n_sub=ROW_SUBTILES)
        cx = dict(layer=layer, mod_row=ctx_row, n_sub=1)
        h = _ffn(h, mods, norm_g, wi_b, wo_b, which=0, tm=tm_ffn, **lat)
        hc = _ffn(hc, mods, norm_g, wi_b, wo_b, which=0, tm=tm_ctx, **cx)
        qa, qb, ka, va, kb, vb, oc = _proj(h, mods, norm_g, w_in_b, qk_gain, conv_w, gmat, rope_tabs,
                                           tm=tm_proj, layer=layer, mod_row=None)
        cqa, cqb, cka, cva, ckb, cvb, coc = _proj(hc, mods, norm_g, w_in_b, qk_gain, conv_w, gmat, None,
                                                  tm=tm_ctx, layer=layer, mod_row=ctx_row)
        if not last:
            coa = _gqa(cqa, cka, cva, None, hq=A_Q_HEADS, hkv=A_KV_HEADS, tq=tm_ctx, tk=ctx_len)
            cob = _gqa(cqb, ckb, cvb, None, hq=B_HEADS, hkv=B_HEADS, tq=tm_ctx, tk=ctx_len)
            hc = _ffn(hc, mods, norm_g, wi_b, wo_b, which=1, tm=tm_ctx, mix=(coa, cob, coc, w_o_b), **cx)
        oa = _gqa_bounded(qa, ka, va, cka, cva, gsum, hq=A_Q_HEADS, tq=tq_a, tk=tk_a, unroll=unroll_a)
        ob = _nbr(qb, kb, vb, ckb, cvb, _nbr_bias_tables(rpb[layer], seq // GRID_W), n_blocks=nbr_blocks)
        h = _oproj(h, oa, ob, oc, mods, norm_g, w_o_b, tm=tm_proj, **lat)
        h = _ffn(h, mods, norm_g, wi_b, wo_b, which=1, tm=tm_ffn, **lat)
    return h
```

```python
import functools

import numpy as np
import jax
import jax.numpy as jnp
from jax import lax
from jax.experimental import pallas as pl
from jax.experimental.pallas import tpu as pltpu

F32 = jnp.float32
BF16 = jnp.bfloat16

HEAD_DIM = 64
GRID_W = 64
A_Q_HEADS = 6
A_KV_HEADS = 2
B_HEADS = 6
C_WIDTH = 256
A_Q_W = A_Q_HEADS * HEAD_DIM
A_KV_W = A_KV_HEADS * HEAD_DIM
B_W = B_HEADS * HEAD_DIM
WIN_R = 8
WIN_C = 16
ROPE_FREQS = HEAD_DIM // 4
ROPE_THETA = 10000.0
N_MOD = 9
EPS = 1e-6
NEG = -1e30
LOG2_E = 1.4426950408889634

V7X_LANES = 128
V7X_SCOPED_VMEM_BYTES = 60000 * 1024

NBR_QROWS = 4
NBR_KROWS = NBR_QROWS + WIN_R

GQA_UNROLL = 4
ROW_SUBTILES = 4
PROJ_SUBTILES = 8
NBR_BLOCKS_PER_STEP = 4


def _vmem_limit(estimate_bytes):
    return int(min(max(estimate_bytes, 16 * 1024 * 1024), V7X_SCOPED_VMEM_BYTES))


def _params(n_axes, vmem_bytes):
    return pltpu.CompilerParams(
        dimension_semantics=("arbitrary",) * n_axes,
        vmem_limit_bytes=_vmem_limit(vmem_bytes),
    )


def _rms(x, g):
    ms = jnp.mean(x * x, axis=-1, keepdims=True)
    return x * lax.rsqrt(ms + EPS) * g


def _resident(block_shape, index_map):
    return pl.BlockSpec(block_shape, index_map, pipeline_mode=pl.Buffered(1))


def _ada_kernel(c_ref, w_ref, b_ref, o_ref):
    c = c_ref[...]
    sc = c * jax.nn.sigmoid(c)
    o_ref[...] = jnp.dot(sc, w_ref[...], preferred_element_type=F32,
                         precision=lax.Precision.HIGHEST) + b_ref[...]


def _ada(c_all, w_ada, b_ada):
    depth, d, n = w_ada.shape
    rows = c_all.shape[0]
    tn = d
    return pl.pallas_call(
        _ada_kernel,
        grid=(depth, n // tn),
        in_specs=[
            pl.BlockSpec((rows, d), lambda l, j: (0, 0)),
            pl.BlockSpec((None, d, tn), lambda l, j: (l, 0, j)),
            pl.BlockSpec((None, 1, tn), lambda l, j: (l, 0, j)),
        ],
        out_specs=pl.BlockSpec((None, rows, tn), lambda l, j: (l, 0, j)),
        out_shape=jax.ShapeDtypeStruct((depth, rows, n), F32),
        compiler_params=_params(2, 4 * d * tn * 4),
        name="ada_mod",
    )(c_all, w_ada, b_ada.reshape(depth, 1, n))


def _ffn_kernel(*refs, i0, gi, ffn_dim, n_sub, mix):
    if mix:
        h_ref, oa_ref, ob_ref, oc_ref, mod_ref, g_ref, wmix_ref, wi_ref, wo_ref, o_ref = refs
    else:
        h_ref, mod_ref, g_ref, wi_ref, wo_ref, o_ref = refs
    shift = mod_ref[i0:i0 + 1, :]
    scale = mod_ref[i0 + 1:i0 + 2, :]
    gate = mod_ref[i0 + 2:i0 + 3, :]
    sub = h_ref.shape[0] // n_sub
    rows_of = lambda t: slice(t * sub, (t + 1) * sub)
    hs, hids, acts = {}, {}, {}

    def up_proj(t):
        h = h_ref[rows_of(t), :]
        if mix:
            o = jnp.concatenate([oa_ref[rows_of(t), :], ob_ref[rows_of(t), :], oc_ref[rows_of(t), :]], axis=1)
            ymix = jnp.dot(o, wmix_ref[...], preferred_element_type=F32)
            h = h + mod_ref[5:6, :] * _rms(ymix, g_ref[3:4, :])
        hs[t] = h
        u = _rms(h, g_ref[gi:gi + 1, :]) * (1.0 + scale) + shift
        hids[t] = jnp.dot(u.astype(BF16), wi_ref[...], preferred_element_type=F32)

    def activate(t):
        hid = hids.pop(t)
        gt = hid[:, :ffn_dim]
        up = hid[:, ffn_dim:]
        acts[t] = (gt * jax.nn.sigmoid(gt) * up).astype(BF16)

    def down_proj(t):
        y = jnp.dot(acts.pop(t), wo_ref[...], preferred_element_type=F32)
        o_ref[rows_of(t), :] = hs.pop(t) + 0.5 * gate * _rms(y, g_ref[gi + 1:gi + 2, :])

    for t in range(n_sub):
        up_proj(t)
        activate(t)
        down_proj(t)


def _ffn(h, mods, norm_g, wi, wo, *, layer, which, mod_row, tm, n_sub=1, mix=None):
    bsz, s, d = h.shape
    ffn_dim = wo.shape[2]
    i0 = 6 * which
    gi = 4 * which
    if mod_row is None:
        mod_map = lambda b, i: (layer, b, 0, 0)
    else:
        mod_map = lambda b, i: (layer, mod_row, 0, 0)
    tile = lambda w: pl.BlockSpec((None, tm, w), lambda b, i: (b, i, 0))
    vmem = ((wi.shape[2] * wi.shape[3] + wo.shape[2] * wo.shape[3]) * 2 + tm * d * 4 * 6
            + (tm // n_sub) * ffn_dim * 24)
    in_specs = [tile(d)]
    args = [h]
    if mix is not None:
        in_specs += [tile(o.shape[2]) for o in mix[:3]]
        args += list(mix[:3])
    in_specs += [pl.BlockSpec((None, None, N_MOD, d), mod_map),
                 _resident((None, norm_g.shape[1], d), lambda b, i: (layer, 0, 0))]
    args += [mods, norm_g]
    if mix is not None:
        w_mix = mix[3]
        in_specs.append(_resident((None, w_mix.shape[1], d), lambda b, i: (layer, 0, 0)))
        args.append(w_mix)
        vmem += w_mix.shape[1] * d * 2 + tm * d * 2 * 2
    in_specs += [_resident((None, None, d, 2 * ffn_dim), lambda b, i: (layer, which, 0, 0)),
                 _resident((None, None, ffn_dim, d), lambda b, i: (layer, which, 0, 0))]
    args += [wi, wo]
    return pl.pallas_call(
        functools.partial(_ffn_kernel, i0=i0, gi=gi, ffn_dim=ffn_dim, n_sub=n_sub, mix=mix is not None),
        grid=(bsz, s // tm),
        in_specs=in_specs,
        out_specs=tile(d),
        out_shape=jax.ShapeDtypeStruct(h.shape, F32),
        compiler_params=_params(2, vmem),
        name="mix_out_ffn" if mix is not None else "ffn",
    )(*args)


def _swap_rope_partners(x):
    lane = lax.broadcasted_iota(jnp.int32, x.shape, 1)
    first = (lane % (2 * ROPE_FREQS)) < ROPE_FREQS
    return jnp.where(first,
                     pltpu.roll(x, V7X_LANES - ROPE_FREQS, axis=1),
                     pltpu.roll(x, ROPE_FREQS, axis=1))


def _proj_kernel(*refs, rope, tm, n_tiles, n_sub):
    if rope:
        (h_ref, hp_ref, hn_ref, mod_ref, g_ref, w_ref, qkg_ref, cw_ref, gm_ref, cos_ref, sin_ref,
         qa_ref, qb_ref, ka_ref, va_ref, kb_ref, vb_ref, oc_ref, z_sc, cb_sc) = refs
    else:
        (h_ref, hp_ref, hn_ref, mod_ref, g_ref, w_ref, qkg_ref, cw_ref, gm_ref,
         qa_ref, qb_ref, ka_ref, va_ref, kb_ref, vb_ref, oc_ref, z_sc, cb_sc) = refs
    i = pl.program_id(1)
    shift = mod_ref[3:4, :]
    scale = mod_ref[4:5, :]
    g2 = g_ref[2:3, :]

    def pre(x):
        return (_rms(x, g2) * (1.0 + scale) + shift).astype(BF16)

    o_qa, o_qb = 0, A_Q_W
    o_ka = o_qb + B_W
    o_va = o_ka + A_KV_W
    o_kb = o_va + A_KV_W
    o_vb = o_kb + B_W
    o_cx = o_vb + B_W
    o_cb = o_cx + C_WIDTH
    o_cc = o_cb + C_WIDTH
    q_scale = HEAD_DIM ** -0.5 * LOG2_E
    half = (A_Q_W + A_KV_W) // 2
    sub = tm // n_sub

    for t in range(n_sub):
        rows = slice(t * sub, (t + 1) * sub)
        p = jnp.dot(pre(h_ref[rows, :]), w_ref[...], preferred_element_type=F32)
        xq = jnp.concatenate([p[:, o_qa:o_qa + A_Q_W], p[:, o_ka:o_ka + A_KV_W]], axis=1)
        sq = (xq * xq).astype(BF16)
        ms = jnp.concatenate(
            [jnp.dot(sq[:, :half], gm_ref[...], preferred_element_type=F32),
             jnp.dot(sq[:, half:], gm_ref[...], preferred_element_type=F32)], axis=1)
        xn = xq * lax.rsqrt(ms + EPS) * qkg_ref[...]
        if rope:
            cos = cos_ref[rows, :]
            sin = sin_ref[rows, :]
            cols = []
            for j in range((A_Q_W + A_KV_W) // V7X_LANES):
                xc = xn[:, j * V7X_LANES:(j + 1) * V7X_LANES]
                cols.append(xc * cos + _swap_rope_partners(xc) * sin)
            xn = jnp.concatenate(cols, axis=1)
        qa_ref[rows, :] = (xn[:, :A_Q_W] * q_scale).astype(BF16)
        ka_ref[rows, :] = xn[:, A_Q_W:].astype(BF16)
        qb_ref[rows, :] = (p[:, o_qb:o_qb + B_W] * q_scale).astype(BF16)
        va_ref[rows, :] = p[:, o_va:o_va + A_KV_W].astype(BF16)
        kb_ref[rows, :] = p[:, o_kb:o_kb + B_W].astype(BF16)
        vb_ref[rows, :] = p[:, o_vb:o_vb + B_W].astype(BF16)
        z_sc[rows, :] = p[:, o_cc:o_cc + C_WIDTH] * p[:, o_cx:o_cx + C_WIDTH]
        cb_sc[rows, :] = p[:, o_cb:o_cb + C_WIDTH]

    z = z_sc[...]

    def halo(ref):
        ub = pre(ref[...])
        return (jnp.dot(ub, w_ref[:, o_cc:o_cc + C_WIDTH], preferred_element_type=F32)
                * jnp.dot(ub, w_ref[:, o_cx:o_cx + C_WIDTH], preferred_element_type=F32))

    halo_rows = hp_ref.shape[0]
    z_before = jnp.where(i > 0, halo(hp_ref)[halo_rows - 1:halo_rows, :], 0.0)
    z_after = jnp.where(i < n_tiles - 1, halo(hn_ref)[0:1, :], 0.0)
    row = lax.broadcasted_iota(jnp.int32, z.shape, 0)
    z_m1 = jnp.where(row == 0, z_before, pltpu.roll(z, 1, axis=0))
    z_p1 = jnp.where(row == tm - 1, z_after, pltpu.roll(z, tm - 1, axis=0))
    y = cw_ref[0:1, :] * z_m1 + cw_ref[1:2, :] * z + cw_ref[2:3, :] * z_p1
    oc_ref[...] = (cb_sc[...] * y).astype(BF16)


def _proj(h, mods, norm_g, w_in, qk_gain, conv_w, gmat, rope_tabs, *, layer, mod_row, tm, n_sub=1):
    bsz, s, d = h.shape
    n_tiles = s // tm
    halo_rows = 8
    hb = tm // halo_rows
    n_hblk = s // halo_rows
    rope = rope_tabs is not None
    if mod_row is None:
        mod_map = lambda b, i: (layer, b, 0, 0)
    else:
        mod_map = lambda b, i: (layer, mod_row, 0, 0)
    pw = w_in.shape[2]
    in_specs = [
        pl.BlockSpec((None, tm, d), lambda b, i: (b, i, 0)),
        pl.BlockSpec((None, halo_rows, d), lambda b, i: (b, jnp.maximum(i * hb - 1, 0), 0)),
        pl.BlockSpec((None, halo_rows, d), lambda b, i: (b, jnp.minimum((i + 1) * hb, n_hblk - 1), 0)),
        pl.BlockSpec((None, None, N_MOD, d), mod_map),
        _resident((None, norm_g.shape[1], d), lambda b, i: (layer, 0, 0)),
        _resident((None, d, pw), lambda b, i: (layer, 0, 0)),
        _resident((None, 1, A_Q_W + A_KV_W), lambda b, i: (layer, 0, 0)),
        _resident((None, conv_w.shape[1], C_WIDTH), lambda b, i: (layer, 0, 0)),
        _resident(gmat.shape, lambda b, i: (0, 0)),
    ]
    args = [h, h, h, mods, norm_g, w_in, qk_gain, conv_w, gmat]
    if rope:
        in_specs += [pl.BlockSpec((tm, V7X_LANES), lambda b, i: (i, 0))] * 2
        args += list(rope_tabs)
    widths = (A_Q_W, B_W, A_KV_W, A_KV_W, B_W, B_W, C_WIDTH)
    out_specs = [pl.BlockSpec((None, tm, w), lambda b, i: (b, i, 0)) for w in widths]
    out_shape = [jax.ShapeDtypeStruct((bsz, s, w), BF16) for w in widths]
    vmem = d * pw * 2 + tm * d * 4 * 4 + tm * pw * 12
    return pl.pallas_call(
        functools.partial(_proj_kernel, rope=rope, tm=tm, n_tiles=n_tiles, n_sub=n_sub),
        grid=(bsz, n_tiles),
        in_specs=in_specs,
        out_specs=out_specs,
        out_shape=out_shape,
        scratch_shapes=[pltpu.VMEM((tm, C_WIDTH), F32), pltpu.VMEM((tm, C_WIDTH), F32)],
        compiler_params=_params(2, vmem),
        name="mix_proj",
    )(*args)


_NT_DIMS = (((1,), (1,)), ((), ()))

SCORE_BOUND_MAX = 40.0
SCORE_BOUND_PIVOT = 16.0


def _load_q_rows(q_ref, qs_sc, heads, grp, tq):
    lane = lax.broadcasted_iota(jnp.int32, (tq, V7X_LANES), 1)
    low_half = lane < HEAD_DIM
    for j, h in enumerate(heads):
        xc = q_ref[:, (h // 2) * V7X_LANES:(h // 2 + 1) * V7X_LANES].astype(F32)
        dst_low = (h // grp) % 2 == 0
        if (h % 2 == 0) != dst_low:
            xc = pltpu.roll(xc, HEAD_DIM, axis=1)
        keep = low_half if dst_low else jnp.logical_not(low_half)
        qs_sc[j * tq:(j + 1) * tq, :] = jnp.where(keep, xc, 0.0).astype(BF16)


def _repack_heads(on_sc, o_ref, hq, grp, tq):
    lane = lax.broadcasted_iota(jnp.int32, (tq, V7X_LANES), 1)
    low_half = lane < HEAD_DIM
    for oc in range(hq // 2):
        pieces = []
        for e in range(2):
            h = 2 * oc + e
            piece = on_sc[h * tq:(h + 1) * tq, :]
            src_low = (h // grp) % 2 == 0
            if src_low != (e == 0):
                piece = pltpu.roll(piece, HEAD_DIM, axis=1)
            pieces.append(piece)
        o_ref[:, oc * V7X_LANES:(oc + 1) * V7X_LANES] = jnp.where(low_half, pieces[0], pieces[1]).astype(BF16)


def _online_softmax(qs_sc, m_sc, l_sc, acc_sc, rows, chunks, extra, *, tk, n_chunks, unroll):
    m_sc[0:rows, :] = jnp.full((rows, V7X_LANES), NEG, F32)
    l_sc[0:rows, :] = jnp.zeros((rows, V7X_LANES), F32)
    acc_sc[0:rows, :] = jnp.zeros((rows, V7X_LANES), F32)

    def load_state():
        return m_sc[0:rows, :], l_sc[0:rows, :], acc_sc[0:rows, :]

    def store_state(state):
        m_sc[0:rows, :], l_sc[0:rows, :], acc_sc[0:rows, :] = state

    def step(state, kc, vc):
        m_prev, l_prev, acc_prev = state
        s = lax.dot_general(qs_sc[0:rows, :], kc, _NT_DIMS, preferred_element_type=F32)
        m_next = jnp.maximum(m_prev, jnp.max(s, axis=1, keepdims=True))
        alpha = jnp.exp2(m_prev - m_next)
        p = jnp.exp2(s - jnp.concatenate([m_next] * (kc.shape[0] // V7X_LANES), axis=1))
        l_next = alpha * l_prev + jnp.sum(p, axis=1, keepdims=True)
        acc_next = alpha * acc_prev + jnp.dot(p.astype(BF16), vc, preferred_element_type=F32)
        return m_next, l_next, acc_next

    def body(c, carry):
        state = load_state()
        for u in range(unroll):
            state = step(state, *chunks(pl.multiple_of((c * unroll + u) * tk, tk)))
        store_state(state)
        return carry

    lax.fori_loop(0, n_chunks // unroll, body, 0)
    if extra is not None:
        store_state(step(load_state(), *extra))
    return acc_sc[0:rows, :] / l_sc[0:rows, :]


def _bounded_softmax(qs_sc, b_sc, l_sc, acc_sc, rows, chunks, extra, *, tk, n_chunks, unroll):
    l_sc[0:rows, :] = jnp.zeros((rows, V7X_LANES), F32)
    acc_sc[0:rows, :] = jnp.zeros((rows, V7X_LANES), F32)

    def step(state, kc, vc):
        l_prev, acc_prev = state
        n_cols = kc.shape[0] // V7X_LANES
        s = lax.dot_general(qs_sc[0:rows, :], kc, _NT_DIMS, preferred_element_type=F32)
        p = jnp.exp2(s - jnp.concatenate([b_sc[0:rows, :]] * n_cols, axis=1))
        l_next = l_prev
        for j in range(n_cols):
            l_next = l_next + p[:, j * V7X_LANES:(j + 1) * V7X_LANES]
        acc_next = acc_prev + jnp.dot(p.astype(BF16), vc, preferred_element_type=F32)
        return l_next, acc_next

    def body(c, carry):
        state = (l_sc[0:rows, :], acc_sc[0:rows, :])
        for u in range(unroll):
            state = step(state, *chunks(pl.multiple_of((c * unroll + u) * tk, tk)))
        l_sc[0:rows, :], acc_sc[0:rows, :] = state
        return carry

    n_trips = n_chunks // unroll
    lax.fori_loop(0, n_trips - 1, body, 0)
    state = (l_sc[0:rows, :], acc_sc[0:rows, :])
    for u in range(unroll):
        state = step(state, *chunks(((n_trips - 1) * unroll + u) * tk))
    if extra is not None:
        state = step(state, *extra)
    l_lanes, acc = state
    return acc / jnp.sum(l_lanes, axis=1, keepdims=True)


def _gqa_kernel(*refs, hq, hkv, tq, tk, n_chunks, unroll, extra_len):
    if extra_len:
        q_ref, k_ref, v_ref, kx_ref, vx_ref, o_ref, qs_sc, m_sc, l_sc, acc_sc, on_sc = refs
    else:
        q_ref, k_ref, v_ref, o_ref, qs_sc, m_sc, l_sc, acc_sc, on_sc = refs
    grp = hq // hkv
    for pc in range(hkv // 2):
        heads = [h for h in range(hq) if (h // grp) // 2 == pc]
        rows = len(heads) * tq
        col = slice(pc * V7X_LANES, (pc + 1) * V7X_LANES)
        _load_q_rows(q_ref, qs_sc, heads, grp, tq)
        chunks = lambda start: (k_ref[pl.ds(start, tk), col], v_ref[pl.ds(start, tk), col])
        extra = (kx_ref[:, col], vx_ref[:, col]) if extra_len else None
        on = _online_softmax(qs_sc, m_sc, l_sc, acc_sc, rows, chunks, extra,
                             tk=tk, n_chunks=n_chunks, unroll=unroll)
        for j, h in enumerate(heads):
            on_sc[h * tq:(h + 1) * tq, :] = on[j * tq:(j + 1) * tq, :]
    _repack_heads(on_sc, o_ref, hq, grp, tq)


def _gqa_bounded_kernel(q_ref, k_ref, v_ref, kx_ref, vx_ref, gs_ref, o_ref,
                        qs_sc, m_sc, l_sc, acc_sc, on_sc, b_sc, kmax_sc, *, hq, tq, tk, n_chunks, unroll):
    hkv = 2
    grp = hq // hkv
    rows = hq * tq

    def max_sq_norm(kc, mx):
        kf = kc.astype(F32)
        ss = lax.dot_general(gs_ref[...], (kf * kf).astype(BF16), _NT_DIMS, preferred_element_type=F32)
        return jnp.maximum(mx, jnp.max(ss, axis=1, keepdims=True))

    @pl.when(pl.program_id(1) == 0)
    def _():
        def kbody(c, mx):
            return max_sq_norm(k_ref[pl.ds(pl.multiple_of(c * tk, tk), tk), :], mx)
        mx = lax.fori_loop(0, n_chunks, kbody, jnp.zeros((V7X_LANES, 1), F32))
        kmax_sc[...] = jnp.broadcast_to(max_sq_norm(kx_ref[...], mx), kmax_sc.shape).astype(BF16)

    _load_q_rows(q_ref, qs_sc, list(range(hq)), grp, tq)
    qf = qs_sc[...].astype(F32)
    qk_sq = jnp.dot((qf * qf).astype(BF16), kmax_sc[...], preferred_element_type=F32)
    bound = (qk_sq * (0.5 / SCORE_BOUND_PIVOT) + 0.5 * SCORE_BOUND_PIVOT) * (1.0 + 2.0 ** -5)
    b_sc[...] = bound
    bounded_ok = jnp.max(bound) <= SCORE_BOUND_MAX

    chunks = lambda start: (k_ref[pl.ds(start, tk), :], v_ref[pl.ds(start, tk), :])
    extra = (kx_ref[...], vx_ref[...])

    @pl.when(bounded_ok)
    def _():
        on_sc[...] = _bounded_softmax(qs_sc, b_sc, l_sc, acc_sc, rows, chunks, extra,
                                      tk=tk, n_chunks=n_chunks, unroll=unroll)

    @pl.when(jnp.logical_not(bounded_ok))
    def _():
        on_sc[...] = _online_softmax(qs_sc, m_sc, l_sc, acc_sc, rows, chunks, extra,
                                     tk=tk, n_chunks=n_chunks, unroll=unroll)

    _repack_heads(on_sc, o_ref, hq, grp, tq)


def _gqa(q, k, v, extra, *, hq, hkv, tq, tk, unroll=1):
    bsz, s, qw = q.shape
    t = k.shape[1]
    kw = k.shape[2]
    n_chunks = t // tk
    assert n_chunks % unroll == 0
    grp = hq // hkv
    max_heads = max(sum(1 for h in range(hq) if (h // grp) // 2 == pc) for pc in range(hkv // 2))
    rows = max_heads * tq
    in_specs = [
        pl.BlockSpec((None, tq, qw), lambda b, i: (b, i, 0)),
        pl.BlockSpec((None, t, kw), lambda b, i: (b, 0, 0)),
        pl.BlockSpec((None, t, kw), lambda b, i: (b, 0, 0)),
    ]
    args = [q, k, v]
    extra_len = 0
    if extra is not None:
        extra_len = extra[0].shape[1]
        in_specs += [pl.BlockSpec((None, extra_len, kw), lambda b, i: (b, 0, 0))] * 2
        args += list(extra)
    vmem = 4 * t * kw * 2 + rows * max(tk, extra_len) * 16 + rows * V7X_LANES * 24 + 8 * tq * qw * 2
    return pl.pallas_call(
        functools.partial(_gqa_kernel, hq=hq, hkv=hkv, tq=tq, tk=tk, n_chunks=n_chunks, unroll=unroll,
                          extra_len=extra_len),
        grid=(bsz, s // tq),
        in_specs=in_specs,
        out_specs=pl.BlockSpec((None, tq, qw), lambda b, i: (b, i, 0)),
        out_shape=jax.ShapeDtypeStruct(q.shape, BF16),
        scratch_shapes=[
            pltpu.VMEM((rows, V7X_LANES), BF16),
            pltpu.VMEM((rows, V7X_LANES), F32),
            pltpu.VMEM((rows, V7X_LANES), F32),
            pltpu.VMEM((rows, V7X_LANES), F32),
            pltpu.VMEM((hq * tq, V7X_LANES), F32),
        ],
        compiler_params=_params(2, vmem),
        name="gqa_attn",
    )(*args)


def _gqa_bounded(q, k, v, kx, vx, gsum, *, hq, tq, tk, unroll):
    bsz, s, qw = q.shape
    t, kw = k.shape[1], k.shape[2]
    lx = kx.shape[1]
    assert kw == V7X_LANES and t % (tk * unroll) == 0
    rows = hq * tq
    stat = pltpu.VMEM((rows, V7X_LANES), F32)
    vmem = 4 * (t + lx) * kw * 2 + rows * max(tk, lx) * 8 * unroll + rows * V7X_LANES * 32 + 8 * tq * qw * 2
    return pl.pallas_call(
        functools.partial(_gqa_bounded_kernel, hq=hq, tq=tq, tk=tk, n_chunks=t // tk, unroll=unroll),
        grid=(bsz, s // tq),
        in_specs=[
            pl.BlockSpec((None, tq, qw), lambda b, i: (b, i, 0)),
            pl.BlockSpec((None, t, kw), lambda b, i: (b, 0, 0)),
            pl.BlockSpec((None, t, kw), lambda b, i: (b, 0, 0)),
            pl.BlockSpec((None, lx, kw), lambda b, i: (b, 0, 0)),
            pl.BlockSpec((None, lx, kw), lambda b, i: (b, 0, 0)),
            _resident(gsum.shape, lambda b, i: (0, 0)),
        ],
        out_specs=pl.BlockSpec((None, tq, qw), lambda b, i: (b, i, 0)),
        out_shape=jax.ShapeDtypeStruct(q.shape, BF16),
        scratch_shapes=[
            pltpu.VMEM((rows, V7X_LANES), BF16), stat, stat, stat, stat, stat,
            pltpu.VMEM((V7X_LANES, V7X_LANES), BF16),
        ],
        compiler_params=_params(2, vmem),
        name="gqa_attn_lat",
    )(q, k, v, kx, vx, gsum)


def _nbr_kernel(q_ref, k_ref, v_ref, kx_ref, vx_ref, *rest, n_blocks, rows_total):
    bias_refs, o_ref = rest[:n_blocks], rest[n_blocks]
    tq = NBR_QROWS * GRID_W
    n_win = NBR_KROWS * GRID_W
    kx = kx_ref[...]
    vx = vx_ref[...]
    lane = lax.broadcasted_iota(jnp.int32, (tq, V7X_LANES), 1)
    low_half = lane < HEAD_DIM
    starts, scores, probs = {}, {}, {}

    def qk(j):
        rb = pl.program_id(2) * n_blocks + j
        krow0 = jnp.clip(rb * NBR_QROWS - WIN_R // 2, 0, rows_total - NBR_KROWS)
        starts[j] = pl.multiple_of(krow0 * GRID_W, NBR_QROWS * GRID_W)
        kw = k_ref[pl.ds(starts[j], n_win), :]
        q = q_ref[j * tq:(j + 1) * tq, :]
        zero = jnp.zeros_like(q)
        qs = jnp.concatenate([jnp.where(low_half, q, zero), jnp.where(low_half, zero, q)], axis=0)
        bias = bias_refs[j][...].reshape(2 * tq, n_win)
        scores[j] = (lax.dot_general(qs, kw, _NT_DIMS, preferred_element_type=F32) + bias,
                     lax.dot_general(qs, kx, _NT_DIMS, preferred_element_type=F32))

    def soft(j):
        s_win, s_ctx = scores.pop(j)
        m = jnp.maximum(jnp.max(s_win, axis=1, keepdims=True), jnp.max(s_ctx, axis=1, keepdims=True))
        p_win = jnp.exp2(s_win - m)
        p_ctx = jnp.exp2(s_ctx - m)
        l = jnp.sum(p_win, axis=1, keepdims=True) + jnp.sum(p_ctx, axis=1, keepdims=True)
        probs[j] = (p_win.astype(BF16), p_ctx.astype(BF16), l)

    def pv(j):
        p_win, p_ctx, l = probs.pop(j)
        vw = v_ref[pl.ds(starts[j], n_win), :]
        o = (jnp.dot(p_win, vw, preferred_element_type=F32)
             + jnp.dot(p_ctx, vx, preferred_element_type=F32)) / l
        o_ref[j * tq:(j + 1) * tq, :] = jnp.where(low_half, o[:tq], o[tq:]).astype(BF16)

    for stage in (qk, soft, pv):
        for j in range(n_blocks):
            stage(j)


def _nbr(q, k, v, kx, vx, bias, *, n_blocks=1):
    bsz, s, w = q.shape
    rows_total = s // GRID_W
    n_rblocks = rows_total // NBR_QROWS
    assert n_rblocks % n_blocks == 0
    tq = NBR_QROWS * GRID_W
    n_win = NBR_KROWS * GRID_W
    lx = kx.shape[1]
    n_pairs = w // V7X_LANES

    def bias_spec(j):
        def bias_map(b, pr, st):
            rb = st * n_blocks + j
            kind = jnp.where(rb == 0, 0, jnp.where(rb == n_rblocks - 1, 2, 1))
            return (kind, pr, 0, 0)
        return pl.BlockSpec((None, 2, tq, n_win), bias_map)

    qo_spec = pl.BlockSpec((None, n_blocks * tq, V7X_LANES), lambda b, pr, st: (b, st, pr))
    vmem = 4 * s * V7X_LANES * 2 + n_blocks * (2 * 2 * tq * n_win * 4 + tq * (n_win + lx) * 24)
    return pl.pallas_call(
        functools.partial(_nbr_kernel, n_blocks=n_blocks, rows_total=rows_total),
        grid=(bsz, n_pairs, n_rblocks // n_blocks),
        in_specs=[
            qo_spec,
            pl.BlockSpec((None, s, V7X_LANES), lambda b, pr, st: (b, 0, pr)),
            pl.BlockSpec((None, s, V7X_LANES), lambda b, pr, st: (b, 0, pr)),
            pl.BlockSpec((None, lx, V7X_LANES), lambda b, pr, st: (b, 0, pr)),
            pl.BlockSpec((None, lx, V7X_LANES), lambda b, pr, st: (b, 0, pr)),
        ] + [bias_spec(j) for j in range(n_blocks)],
        out_specs=qo_spec,
        out_shape=jax.ShapeDtypeStruct(q.shape, BF16),
        compiler_params=_params(3, vmem),
        name="nbr_attn",
    )(q, k, v, kx, vx, *([bias] * n_blocks))


def _nbr_bias_tables(rpb, rows_total):
    assert rows_total % NBR_QROWS == 0 and rows_total >= NBR_KROWS + NBR_QROWS
    wr = min(WIN_R, rows_total)
    kinds = [(0, 0), (2 * NBR_QROWS, 2 * NBR_QROWS - WIN_R // 2),
             (rows_total - NBR_QROWS, rows_total - NBR_KROWS)]
    qi = np.arange(NBR_QROWS)[:, None, None, None]
    qc = np.arange(GRID_W)[None, :, None, None]
    kj = np.arange(NBR_KROWS)[None, None, :, None]
    kc = np.arange(GRID_W)[None, None, None, :]
    n_dr, n_dc = rpb.shape[1], rpb.shape[2]
    cs = np.clip(qc - WIN_C // 2, 0, GRID_W - WIN_C)
    col_ok = (kc >= cs) & (kc < cs + WIN_C)
    col_sel = (kc - qc + (WIN_C - 1))[..., None] == np.arange(n_dc)
    col_sel = (col_sel & col_ok[..., None])[0, :, 0].astype(np.float32)
    row_sel, ok_all = [], []
    for r0, k0 in kinds:
        r = r0 + qi
        rs = np.clip(r - wr // 2, 0, rows_total - wr)
        kr = k0 + kj
        row_ok = (kr >= rs) & (kr < rs + wr)
        sel = ((kr - r + (WIN_R - 1))[..., None] == np.arange(n_dr)) & row_ok[..., None]
        row_sel.append(sel[:, 0, :, 0].astype(np.float32))
        ok_all.append(np.broadcast_to(row_ok & col_ok, (NBR_QROWS, GRID_W, NBR_KROWS, GRID_W)))
    row_sel = jnp.asarray(np.stack(row_sel))
    ok = np.stack(ok_all).reshape(3, 1, NBR_QROWS * GRID_W, NBR_KROWS * GRID_W)
    hi = lax.Precision.HIGHEST
    rows_picked = jnp.einsum('hrd,tijr->thijd', rpb, row_sel, precision=hi)
    tab = jnp.einsum('thijd,cnd->thicjn', rows_picked, jnp.asarray(col_sel), precision=hi)
    tab = tab.reshape(3, rpb.shape[0], NBR_QROWS * GRID_W, NBR_KROWS * GRID_W)
    return jnp.where(ok, tab * LOG2_E, NEG).astype(F32)


def _oproj_kernel(h_ref, oa_ref, ob_ref, oc_ref, mod_ref, g_ref, w_ref, o_ref, *, n_sub):
    sub = h_ref.shape[0] // n_sub
    for t in range(n_sub):
        rows = slice(t * sub, (t + 1) * sub)
        o = jnp.concatenate([oa_ref[rows, :], ob_ref[rows, :], oc_ref[rows, :]], axis=1)
        y = jnp.dot(o, w_ref[...], preferred_element_type=F32)
        o_ref[rows, :] = h_ref[rows, :] + mod_ref[5:6, :] * _rms(y, g_ref[3:4, :])


def _oproj(h, oa, ob, oc, mods, norm_g, w_o, *, layer, mod_row, tm, n_sub=1):
    bsz, s, d = h.shape
    if mod_row is None:
        mod_map = lambda b, i: (layer, b, 0, 0)
    else:
        mod_map = lambda b, i: (layer, mod_row, 0, 0)
    tile = lambda w: pl.BlockSpec((None, tm, w), lambda b, i: (b, i, 0))
    vmem = w_o.shape[1] * d * 2 + tm * d * 4 * 8
    return pl.pallas_call(
        functools.partial(_oproj_kernel, n_sub=n_sub),
        grid=(bsz, s // tm),
        in_specs=[
            tile(d), tile(oa.shape[2]), tile(ob.shape[2]), tile(oc.shape[2]),
            pl.BlockSpec((None, None, N_MOD, d), mod_map),
            _resident((None, norm_g.shape[1], d), lambda b, i: (layer, 0, 0)),
            _resident((None, w_o.shape[1], d), lambda b, i: (layer, 0, 0)),
        ],
        out_specs=tile(d),
        out_shape=jax.ShapeDtypeStruct(h.shape, F32),
        compiler_params=_params(2, vmem),
        name="mix_out",
    )(h, oa, ob, oc, mods, norm_g, w_o)


def _rope_tables(seq):
    pos = jnp.arange(seq, dtype=jnp.int32)
    row = (pos // GRID_W).astype(F32)
    col = (pos % GRID_W).astype(F32)
    freq = 1.0 / (ROPE_THETA ** (jnp.arange(ROPE_FREQS, dtype=F32) / ROPE_FREQS))
    ar = row[:, None] * freq
    ac = col[:, None] * freq
    cos = jnp.concatenate([jnp.cos(ar), jnp.cos(ar), jnp.cos(ac), jnp.cos(ac)], axis=1)
    sin = jnp.concatenate([-jnp.sin(ar), jnp.sin(ar), -jnp.sin(ac), jnp.sin(ac)], axis=1)
    reps = V7X_LANES // HEAD_DIM
    return jnp.tile(cos, (1, reps)), jnp.tile(sin, (1, reps))


def _tile_rows(n, target):
    t = min(n, target)
    assert n % t == 0
    return t


def kernel(x, c, ctx, c_ctx, w_ada, b_ada, norm_g, w_in, qk_g, rpb, conv_w, w_o, ffn_wi, ffn_wo):
    bsz, seq, d = x.shape
    ctx_len = ctx.shape[1]
    depth = w_ada.shape[0]
    ctx_row = bsz

    mod_rows = -(-(bsz + 1) // 8) * 8
    c_all = jnp.zeros((mod_rows, d), F32).at[:bsz].set(c).at[ctx_row].set(c_ctx)
    mods = _ada(c_all, w_ada, b_ada).reshape(depth, mod_rows, N_MOD, d)

    w_in_b = w_in.astype(BF16)
    w_o_b = w_o.astype(BF16)
    wi_b = ffn_wi.astype(BF16)
    wo_b = ffn_wo.astype(BF16)
    qk_gain = jnp.concatenate(
        [jnp.tile(qk_g[:, 0], (1, A_Q_HEADS)), jnp.tile(qk_g[:, 1], (1, A_KV_HEADS))], axis=1
    ).reshape(depth, 1, A_Q_W + A_KV_W)
    gsz = (A_Q_W + A_KV_W) // 2
    head_of = np.arange(gsz) // HEAD_DIM
    gmat = jnp.asarray((head_of[:, None] == head_of[None, :]) / HEAD_DIM, dtype=BF16)
    lane_head = np.arange(V7X_LANES) // HEAD_DIM
    gsum = jnp.asarray(lane_head[:, None] == lane_head[None, :], dtype=BF16)
    rope_tabs = _rope_tables(seq)

    tm_ffn = _tile_rows(seq, 512)
    tm_proj = _tile_rows(seq, 1024)
    tm_ctx = _tile_rows(ctx_len, 256)
    tq_a = _tile_rows(seq, 256)
    tk_a = _tile_rows(seq, 512)
    unroll_a = min(GQA_UNROLL, seq // tk_a)
    n_rblocks = seq // (GRID_W * NBR_QROWS)
    nbr_blocks = NBR_BLOCKS_PER_STEP if n_rblocks % NBR_BLOCKS_PER_STEP == 0 else 1

    h, hc = x, ctx
    for layer in range(depth):
        last = layer == depth - 1
        lat = dict(layer=layer, mod_row=None, n_sub=ROW_SUBTILES)
        cx = dict(layer=layer, mod_row=ctx_row, n_sub=1)
        h = _ffn(h, mods, norm_g, wi_b, wo_b, which=0, tm=tm_ffn, **lat)
        hc = _ffn(hc, mods, norm_g, wi_b, wo_b, which=0, tm=tm_ctx, **cx)
        qa, qb, ka, va, kb, vb, oc = _proj(h, mods, norm_g, w_in_b, qk_gain, conv_w, gmat, rope_tabs,
                                           tm=tm_proj, layer=layer, mod_row=None, n_sub=PROJ_SUBTILES)
        cqa, cqb, cka, cva, ckb, cvb, coc = _proj(hc, mods, norm_g, w_in_b, qk_gain, conv_w, gmat, None,
                                                  tm=tm_ctx, layer=layer, mod_row=ctx_row)
        if not last:
            coa = _gqa(cqa, cka, cva, None, hq=A_Q_HEADS, hkv=A_KV_HEADS, tq=tm_ctx, tk=ctx_len)
            cob = _gqa(cqb, ckb, cvb, None, hq=B_HEADS, hkv=B_HEADS, tq=tm_ctx, tk=ctx_len)
            hc = _ffn(hc, mods, norm_g, wi_b, wo_b, which=1, tm=tm_ctx, mix=(coa, cob, coc, w_o_b), **cx)
        oa = _gqa_bounded(qa, ka, va, cka, cva, gsum, hq=A_Q_HEADS, tq=tq_a, tk=tk_a, unroll=unroll_a)
        ob = _nbr(qb, kb, vb, ckb, cvb, _nbr_bias_tables(rpb[layer], seq // GRID_W), n_blocks=nbr_blocks)
        h = _oproj(h, oa, ob, oc, mods, norm_g, w_o_b, tm=tm_proj, **lat)
        h = _ffn(h, mods, norm_g, wi_b, wo_b, which=1, tm=tm_ffn, **lat)
    return h
```

```python
import functools

import numpy as np
import jax
import jax.numpy as jnp
from jax import lax
from jax.experimental import pallas as pl
from jax.experimental.pallas import tpu as pltpu

F32 = jnp.float32
BF16 = jnp.bfloat16

HEAD_DIM = 64
GRID_W = 64
A_Q_HEADS = 6
A_KV_HEADS = 2
B_HEADS = 6
C_WIDTH = 256
A_Q_W = A_Q_HEADS * HEAD_DIM
A_KV_W = A_KV_HEADS * HEAD_DIM
B_W = B_HEADS * HEAD_DIM
WIN_R = 8
WIN_C = 16
ROPE_FREQS = HEAD_DIM // 4
ROPE_THETA = 10000.0
N_MOD = 9
EPS = 1e-6
NEG = -1e30
LOG2_E = 1.4426950408889634

V7X_LANES = 128
V7X_SCOPED_VMEM_BYTES = 60000 * 1024

NBR_QROWS = 4
NBR_KROWS = NBR_QROWS + WIN_R

GQA_UNROLL = 8
GQA_FALLBACK_UNROLL = 2
ROW_SUBTILES = 4
PROJ_SUBTILES = 8
NBR_BLOCKS_PER_STEP = 4


def _vmem_limit(estimate_bytes):
    return int(min(max(estimate_bytes, 16 * 1024 * 1024), V7X_SCOPED_VMEM_BYTES))


def _params(n_axes, vmem_bytes):
    return pltpu.CompilerParams(
        dimension_semantics=("arbitrary",) * n_axes,
        vmem_limit_bytes=_vmem_limit(vmem_bytes),
    )


def _rms(x, g):
    ms = jnp.mean(x * x, axis=-1, keepdims=True)
    return x * lax.rsqrt(ms + EPS) * g


def _resident(block_shape, index_map):
    return pl.BlockSpec(block_shape, index_map, pipeline_mode=pl.Buffered(1))


def _ada_kernel(c_ref, w_ref, b_ref, o_ref):
    c = c_ref[...]
    sc = c * jax.nn.sigmoid(c)
    o_ref[...] = jnp.dot(sc, w_ref[...], preferred_element_type=F32,
                         precision=lax.Precision.HIGHEST) + b_ref[...]


def _ada(c_all, w_ada, b_ada):
    depth, d, n = w_ada.shape
    rows = c_all.shape[0]
    tn = d
    return pl.pallas_call(
        _ada_kernel,
        grid=(depth, n // tn),
        in_specs=[
            pl.BlockSpec((rows, d), lambda l, j: (0, 0)),
            pl.BlockSpec((None, d, tn), lambda l, j: (l, 0, j)),
            pl.BlockSpec((None, 1, tn), lambda l, j: (l, 0, j)),
        ],
        out_specs=pl.BlockSpec((None, rows, tn), lambda l, j: (l, 0, j)),
        out_shape=jax.ShapeDtypeStruct((depth, rows, n), F32),
        compiler_params=_params(2, 4 * d * tn * 4),
        name="ada_mod",
    )(c_all, w_ada, b_ada.reshape(depth, 1, n))


def _ffn_kernel(*refs, i0, gi, ffn_dim, n_sub, mix):
    if mix:
        h_ref, oa_ref, ob_ref, oc_ref, mod_ref, g_ref, wmix_ref, wi_ref, wo_ref, o_ref = refs
    else:
        h_ref, mod_ref, g_ref, wi_ref, wo_ref, o_ref = refs
    shift = mod_ref[i0:i0 + 1, :]
    scale = mod_ref[i0 + 1:i0 + 2, :]
    gate = mod_ref[i0 + 2:i0 + 3, :]
    sub = h_ref.shape[0] // n_sub
    rows_of = lambda t: slice(t * sub, (t + 1) * sub)
    hs, hids, acts = {}, {}, {}

    def up_proj(t):
        h = h_ref[rows_of(t), :]
        if mix:
            o = jnp.concatenate([oa_ref[rows_of(t), :], ob_ref[rows_of(t), :], oc_ref[rows_of(t), :]], axis=1)
            ymix = jnp.dot(o, wmix_ref[...], preferred_element_type=F32)
            h = h + mod_ref[5:6, :] * _rms(ymix, g_ref[3:4, :])
        hs[t] = h
        u = _rms(h, g_ref[gi:gi + 1, :]) * (1.0 + scale) + shift
        hids[t] = jnp.dot(u.astype(BF16), wi_ref[...], preferred_element_type=F32)

    def activate(t):
        hid = hids.pop(t)
        gt = hid[:, :ffn_dim]
        up = hid[:, ffn_dim:]
        acts[t] = (gt * jax.nn.sigmoid(gt) * up).astype(BF16)

    def down_proj(t):
        y = jnp.dot(acts.pop(t), wo_ref[...], preferred_element_type=F32)
        o_ref[rows_of(t), :] = hs.pop(t) + 0.5 * gate * _rms(y, g_ref[gi + 1:gi + 2, :])

    for t in range(n_sub):
        up_proj(t)
        activate(t)
        down_proj(t)


def _ffn(h, mods, norm_g, wi, wo, *, layer, which, mod_row, tm, n_sub=1, mix=None):
    bsz, s, d = h.shape
    ffn_dim = wo.shape[2]
    i0 = 6 * which
    gi = 4 * which
    if mod_row is None:
        mod_map = lambda b, i: (layer, b, 0, 0)
    else:
        mod_map = lambda b, i: (layer, mod_row, 0, 0)
    tile = lambda w: pl.BlockSpec((None, tm, w), lambda b, i: (b, i, 0))
    vmem = ((wi.shape[2] * wi.shape[3] + wo.shape[2] * wo.shape[3]) * 2 + tm * d * 4 * 6
            + (tm // n_sub) * ffn_dim * 24)
    in_specs = [tile(d)]
    args = [h]
    if mix is not None:
        in_specs += [tile(o.shape[2]) for o in mix[:3]]
        args += list(mix[:3])
    in_specs += [pl.BlockSpec((None, None, N_MOD, d), mod_map),
                 _resident((None, norm_g.shape[1], d), lambda b, i: (layer, 0, 0))]
    args += [mods, norm_g]
    if mix is not None:
        w_mix = mix[3]
        in_specs.append(_resident((None, w_mix.shape[1], d), lambda b, i: (layer, 0, 0)))
        args.append(w_mix)
        vmem += w_mix.shape[1] * d * 2 + tm * d * 2 * 2
    in_specs += [_resident((None, None, d, 2 * ffn_dim), lambda b, i: (layer, which, 0, 0)),
                 _resident((None, None, ffn_dim, d), lambda b, i: (layer, which, 0, 0))]
    args += [wi, wo]
    return pl.pallas_call(
        functools.partial(_ffn_kernel, i0=i0, gi=gi, ffn_dim=ffn_dim, n_sub=n_sub, mix=mix is not None),
        grid=(bsz, s // tm),
        in_specs=in_specs,
        out_specs=tile(d),
        out_shape=jax.ShapeDtypeStruct(h.shape, F32),
        compiler_params=_params(2, vmem),
        name="mix_out_ffn" if mix is not None else "ffn",
    )(*args)


def _swap_rope_partners(x):
    lane = lax.broadcasted_iota(jnp.int32, x.shape, 1)
    first = (lane % (2 * ROPE_FREQS)) < ROPE_FREQS
    return jnp.where(first,
                     pltpu.roll(x, V7X_LANES - ROPE_FREQS, axis=1),
                     pltpu.roll(x, ROPE_FREQS, axis=1))


def _proj_kernel(*refs, rope, tm, n_tiles, n_sub):
    if rope:
        (h_ref, hp_ref, hn_ref, mod_ref, g_ref, w_ref, qkg_ref, cw_ref, gm_ref, cos_ref, sin_ref,
         qa_ref, qb_ref, ka_ref, va_ref, kb_ref, vb_ref, oc_ref, z_sc, cb_sc) = refs
    else:
        (h_ref, hp_ref, hn_ref, mod_ref, g_ref, w_ref, qkg_ref, cw_ref, gm_ref,
         qa_ref, qb_ref, ka_ref, va_ref, kb_ref, vb_ref, oc_ref, z_sc, cb_sc) = refs
    i = pl.program_id(1)
    shift = mod_ref[3:4, :]
    scale = mod_ref[4:5, :]
    g2 = g_ref[2:3, :]

    def pre(x):
        return (_rms(x, g2) * (1.0 + scale) + shift).astype(BF16)

    o_qa, o_qb = 0, A_Q_W
    o_ka = o_qb + B_W
    o_va = o_ka + A_KV_W
    o_kb = o_va + A_KV_W
    o_vb = o_kb + B_W
    o_cx = o_vb + B_W
    o_cb = o_cx + C_WIDTH
    o_cc = o_cb + C_WIDTH
    q_scale = HEAD_DIM ** -0.5 * LOG2_E
    half = (A_Q_W + A_KV_W) // 2
    sub = tm // n_sub

    for t in range(n_sub):
        rows = slice(t * sub, (t + 1) * sub)
        p = jnp.dot(pre(h_ref[rows, :]), w_ref[...], preferred_element_type=F32)
        xq = jnp.concatenate([p[:, o_qa:o_qa + A_Q_W], p[:, o_ka:o_ka + A_KV_W]], axis=1)
        sq = (xq * xq).astype(BF16)
        ms = jnp.concatenate(
            [jnp.dot(sq[:, :half], gm_ref[...], preferred_element_type=F32),
             jnp.dot(sq[:, half:], gm_ref[...], preferred_element_type=F32)], axis=1)
        xn = xq * lax.rsqrt(ms + EPS) * qkg_ref[...]
        if rope:
            cos = cos_ref[rows, :]
            sin = sin_ref[rows, :]
            cols = []
            for j in range((A_Q_W + A_KV_W) // V7X_LANES):
                xc = xn[:, j * V7X_LANES:(j + 1) * V7X_LANES]
                cols.append(xc * cos + _swap_rope_partners(xc) * sin)
            xn = jnp.concatenate(cols, axis=1)
        qa_ref[rows, :] = (xn[:, :A_Q_W] * q_scale).astype(BF16)
        ka_ref[rows, :] = xn[:, A_Q_W:].astype(BF16)
        qb_ref[rows, :] = (p[:, o_qb:o_qb + B_W] * q_scale).astype(BF16)
        va_ref[rows, :] = p[:, o_va:o_va + A_KV_W].astype(BF16)
        kb_ref[rows, :] = p[:, o_kb:o_kb + B_W].astype(BF16)
        vb_ref[rows, :] = p[:, o_vb:o_vb + B_W].astype(BF16)
        z_sc[rows, :] = p[:, o_cc:o_cc + C_WIDTH] * p[:, o_cx:o_cx + C_WIDTH]
        cb_sc[rows, :] = p[:, o_cb:o_cb + C_WIDTH]

    z = z_sc[...]

    def halo(ref):
        ub = pre(ref[...])
        return (jnp.dot(ub, w_ref[:, o_cc:o_cc + C_WIDTH], preferred_element_type=F32)
                * jnp.dot(ub, w_ref[:, o_cx:o_cx + C_WIDTH], preferred_element_type=F32))

    halo_rows = hp_ref.shape[0]
    z_before = jnp.where(i > 0, halo(hp_ref)[halo_rows - 1:halo_rows, :], 0.0)
    z_after = jnp.where(i < n_tiles - 1, halo(hn_ref)[0:1, :], 0.0)
    row = lax.broadcasted_iota(jnp.int32, z.shape, 0)
    z_m1 = jnp.where(row == 0, z_before, pltpu.roll(z, 1, axis=0))
    z_p1 = jnp.where(row == tm - 1, z_after, pltpu.roll(z, tm - 1, axis=0))
    y = cw_ref[0:1, :] * z_m1 + cw_ref[1:2, :] * z + cw_ref[2:3, :] * z_p1
    oc_ref[...] = (cb_sc[...] * y).astype(BF16)


def _proj(h, mods, norm_g, w_in, qk_gain, conv_w, gmat, rope_tabs, *, layer, mod_row, tm, n_sub=1):
    bsz, s, d = h.shape
    n_tiles = s // tm
    halo_rows = 8
    hb = tm // halo_rows
    n_hblk = s // halo_rows
    rope = rope_tabs is not None
    if mod_row is None:
        mod_map = lambda b, i: (layer, b, 0, 0)
    else:
        mod_map = lambda b, i: (layer, mod_row, 0, 0)
    pw = w_in.shape[2]
    in_specs = [
        pl.BlockSpec((None, tm, d), lambda b, i: (b, i, 0)),
        pl.BlockSpec((None, halo_rows, d), lambda b, i: (b, jnp.maximum(i * hb - 1, 0), 0)),
        pl.BlockSpec((None, halo_rows, d), lambda b, i: (b, jnp.minimum((i + 1) * hb, n_hblk - 1), 0)),
        pl.BlockSpec((None, None, N_MOD, d), mod_map),
        _resident((None, norm_g.shape[1], d), lambda b, i: (layer, 0, 0)),
        _resident((None, d, pw), lambda b, i: (layer, 0, 0)),
        _resident((None, 1, A_Q_W + A_KV_W), lambda b, i: (layer, 0, 0)),
        _resident((None, conv_w.shape[1], C_WIDTH), lambda b, i: (layer, 0, 0)),
        _resident(gmat.shape, lambda b, i: (0, 0)),
    ]
    args = [h, h, h, mods, norm_g, w_in, qk_gain, conv_w, gmat]
    if rope:
        in_specs += [pl.BlockSpec((tm, V7X_LANES), lambda b, i: (i, 0))] * 2
        args += list(rope_tabs)
    widths = (A_Q_W, B_W, A_KV_W, A_KV_W, B_W, B_W, C_WIDTH)
    out_specs = [pl.BlockSpec((None, tm, w), lambda b, i: (b, i, 0)) for w in widths]
    out_shape = [jax.ShapeDtypeStruct((bsz, s, w), BF16) for w in widths]
    vmem = d * pw * 2 + tm * d * 4 * 4 + tm * pw * 12
    return pl.pallas_call(
        functools.partial(_proj_kernel, rope=rope, tm=tm, n_tiles=n_tiles, n_sub=n_sub),
        grid=(bsz, n_tiles),
        in_specs=in_specs,
        out_specs=out_specs,
        out_shape=out_shape,
        scratch_shapes=[pltpu.VMEM((tm, C_WIDTH), F32), pltpu.VMEM((tm, C_WIDTH), F32)],
        compiler_params=_params(2, vmem),
        name="mix_proj",
    )(*args)


_NT_DIMS = (((1,), (1,)), ((), ()))

SCORE_BOUND_MAX = 40.0
SCORE_BOUND_PIVOT = 16.0


def _load_q_rows(q_ref, qs_sc, heads, grp, tq):
    lane = lax.broadcasted_iota(jnp.int32, (tq, V7X_LANES), 1)
    low_half = lane < HEAD_DIM
    for j, h in enumerate(heads):
        xc = q_ref[:, (h // 2) * V7X_LANES:(h // 2 + 1) * V7X_LANES].astype(F32)
        dst_low = (h // grp) % 2 == 0
        if (h % 2 == 0) != dst_low:
            xc = pltpu.roll(xc, HEAD_DIM, axis=1)
        keep = low_half if dst_low else jnp.logical_not(low_half)
        qs_sc[j * tq:(j + 1) * tq, :] = jnp.where(keep, xc, 0.0).astype(BF16)


def _repack_heads(on_sc, o_ref, hq, grp, tq):
    lane = lax.broadcasted_iota(jnp.int32, (tq, V7X_LANES), 1)
    low_half = lane < HEAD_DIM
    for oc in range(hq // 2):
        pieces = []
        for e in range(2):
            h = 2 * oc + e
            piece = on_sc[h * tq:(h + 1) * tq, :]
            src_low = (h // grp) % 2 == 0
            if src_low != (e == 0):
                piece = pltpu.roll(piece, HEAD_DIM, axis=1)
            pieces.append(piece)
        o_ref[:, oc * V7X_LANES:(oc + 1) * V7X_LANES] = jnp.where(low_half, pieces[0], pieces[1]).astype(BF16)


def _online_softmax(qs_sc, m_sc, l_sc, acc_sc, rows, chunks, extra, *, tk, n_chunks, unroll):
    m_sc[0:rows, :] = jnp.full((rows, V7X_LANES), NEG, F32)
    l_sc[0:rows, :] = jnp.zeros((rows, V7X_LANES), F32)
    acc_sc[0:rows, :] = jnp.zeros((rows, V7X_LANES), F32)

    def load_state():
        return m_sc[0:rows, :], l_sc[0:rows, :], acc_sc[0:rows, :]

    def store_state(state):
        m_sc[0:rows, :], l_sc[0:rows, :], acc_sc[0:rows, :] = state

    def step(state, kc, vc):
        m_prev, l_prev, acc_prev = state
        s = lax.dot_general(qs_sc[0:rows, :], kc, _NT_DIMS, preferred_element_type=F32)
        m_next = jnp.maximum(m_prev, jnp.max(s, axis=1, keepdims=True))
        alpha = jnp.exp2(m_prev - m_next)
        p = jnp.exp2(s - jnp.concatenate([m_next] * (kc.shape[0] // V7X_LANES), axis=1))
        l_next = alpha * l_prev + jnp.sum(p, axis=1, keepdims=True)
        acc_next = alpha * acc_prev + jnp.dot(p.astype(BF16), vc, preferred_element_type=F32)
        return m_next, l_next, acc_next

    def body(c, carry):
        state = load_state()
        for u in range(unroll):
            state = step(state, *chunks(pl.multiple_of((c * unroll + u) * tk, tk)))
        store_state(state)
        return carry

    lax.fori_loop(0, n_chunks // unroll, body, 0)
    if extra is not None:
        store_state(step(load_state(), *extra))
    return acc_sc[0:rows, :] / l_sc[0:rows, :]


def _bounded_softmax(qs_sc, b_sc, l_sc, acc_sc, rows, chunks, extra, *, tk, n_chunks, unroll):
    l_sc[0:rows, :] = jnp.zeros((rows, V7X_LANES), F32)
    acc_sc[0:rows, :] = jnp.zeros((rows, V7X_LANES), F32)

    def step(state, kc, vc):
        l_prev, acc_prev = state
        n_cols = kc.shape[0] // V7X_LANES
        s = lax.dot_general(qs_sc[0:rows, :], kc, _NT_DIMS, preferred_element_type=F32)
        p = jnp.exp2(s - jnp.concatenate([b_sc[0:rows, :]] * n_cols, axis=1))
        l_next = l_prev
        for j in range(n_cols):
            l_next = l_next + p[:, j * V7X_LANES:(j + 1) * V7X_LANES]
        acc_next = acc_prev + jnp.dot(p.astype(BF16), vc, preferred_element_type=F32)
        return l_next, acc_next

    def body(c, carry):
        state = (l_sc[0:rows, :], acc_sc[0:rows, :])
        for u in range(unroll):
            state = step(state, *chunks(pl.multiple_of((c * unroll + u) * tk, tk)))
        l_sc[0:rows, :], acc_sc[0:rows, :] = state
        return carry

    n_trips = n_chunks // unroll
    lax.fori_loop(0, n_trips - 1, body, 0)
    state = (l_sc[0:rows, :], acc_sc[0:rows, :])
    for u in range(unroll):
        state = step(state, *chunks(((n_trips - 1) * unroll + u) * tk))
    if extra is not None:
        state = step(state, *extra)
    l_lanes, acc = state
    return acc / jnp.sum(l_lanes, axis=1, keepdims=True)


def _gqa_kernel(*refs, hq, hkv, tq, tk, n_chunks, unroll, extra_len):
    if extra_len:
        q_ref, k_ref, v_ref, kx_ref, vx_ref, o_ref, qs_sc, m_sc, l_sc, acc_sc, on_sc = refs
    else:
        q_ref, k_ref, v_ref, o_ref, qs_sc, m_sc, l_sc, acc_sc, on_sc = refs
    grp = hq // hkv
    for pc in range(hkv // 2):
        heads = [h for h in range(hq) if (h // grp) // 2 == pc]
        rows = len(heads) * tq
        col = slice(pc * V7X_LANES, (pc + 1) * V7X_LANES)
        _load_q_rows(q_ref, qs_sc, heads, grp, tq)
        chunks = lambda start: (k_ref[pl.ds(start, tk), col], v_ref[pl.ds(start, tk), col])
        extra = (kx_ref[:, col], vx_ref[:, col]) if extra_len else None
        on = _online_softmax(qs_sc, m_sc, l_sc, acc_sc, rows, chunks, extra,
                             tk=tk, n_chunks=n_chunks, unroll=unroll)
        for j, h in enumerate(heads):
            on_sc[h * tq:(h + 1) * tq, :] = on[j * tq:(j + 1) * tq, :]
    _repack_heads(on_sc, o_ref, hq, grp, tq)


def _gqa_bounded_kernel(q_ref, k_ref, v_ref, kx_ref, vx_ref, gs_ref, o_ref,
                        qs_sc, m_sc, l_sc, acc_sc, on_sc, b_sc, kmax_sc, *, hq, tq, tk, n_chunks, unroll):
    hkv = 2
    grp = hq // hkv
    rows = hq * tq

    def max_sq_norm(kc, mx):
        kf = kc.astype(F32)
        ss = lax.dot_general(gs_ref[...], (kf * kf).astype(BF16), _NT_DIMS, preferred_element_type=F32)
        return jnp.maximum(mx, jnp.max(ss, axis=1, keepdims=True))

    @pl.when(pl.program_id(1) == 0)
    def _():
        def kbody(c, mx):
            return max_sq_norm(k_ref[pl.ds(pl.multiple_of(c * tk, tk), tk), :], mx)
        mx = lax.fori_loop(0, n_chunks, kbody, jnp.zeros((V7X_LANES, 1), F32))
        kmax_sc[...] = jnp.broadcast_to(max_sq_norm(kx_ref[...], mx), kmax_sc.shape).astype(BF16)

    _load_q_rows(q_ref, qs_sc, list(range(hq)), grp, tq)
    qf = qs_sc[...].astype(F32)
    qk_sq = jnp.dot((qf * qf).astype(BF16), kmax_sc[...], preferred_element_type=F32)
    bound = (qk_sq * (0.5 / SCORE_BOUND_PIVOT) + 0.5 * SCORE_BOUND_PIVOT) * (1.0 + 2.0 ** -5)
    b_sc[...] = bound
    bounded_ok = jnp.max(bound) <= SCORE_BOUND_MAX

    chunks = lambda start: (k_ref[pl.ds(start, tk), :], v_ref[pl.ds(start, tk), :])
    extra = (kx_ref[...], vx_ref[...])

    @pl.when(bounded_ok)
    def _():
        on_sc[...] = _bounded_softmax(qs_sc, b_sc, l_sc, acc_sc, rows, chunks, extra,
                                      tk=tk, n_chunks=n_chunks, unroll=unroll)

    @pl.when(jnp.logical_not(bounded_ok))
    def _():
        on_sc[...] = _online_softmax(qs_sc, m_sc, l_sc, acc_sc, rows, chunks, extra,
                                     tk=tk, n_chunks=n_chunks, unroll=min(unroll, GQA_FALLBACK_UNROLL))

    _repack_heads(on_sc, o_ref, hq, grp, tq)


def _gqa(q, k, v, extra, *, hq, hkv, tq, tk, unroll=1):
    bsz, s, qw = q.shape
    t = k.shape[1]
    kw = k.shape[2]
    n_chunks = t // tk
    assert n_chunks % unroll == 0
    grp = hq // hkv
    max_heads = max(sum(1 for h in range(hq) if (h // grp) // 2 == pc) for pc in range(hkv // 2))
    rows = max_heads * tq
    in_specs = [
        pl.BlockSpec((None, tq, qw), lambda b, i: (b, i, 0)),
        pl.BlockSpec((None, t, kw), lambda b, i: (b, 0, 0)),
        pl.BlockSpec((None, t, kw), lambda b, i: (b, 0, 0)),
    ]
    args = [q, k, v]
    extra_len = 0
    if extra is not None:
        extra_len = extra[0].shape[1]
        in_specs += [pl.BlockSpec((None, extra_len, kw), lambda b, i: (b, 0, 0))] * 2
        args += list(extra)
    vmem = 4 * t * kw * 2 + rows * max(tk, extra_len) * 16 + rows * V7X_LANES * 24 + 8 * tq * qw * 2
    return pl.pallas_call(
        functools.partial(_gqa_kernel, hq=hq, hkv=hkv, tq=tq, tk=tk, n_chunks=n_chunks, unroll=unroll,
                          extra_len=extra_len),
        grid=(bsz, s // tq),
        in_specs=in_specs,
        out_specs=pl.BlockSpec((None, tq, qw), lambda b, i: (b, i, 0)),
        out_shape=jax.ShapeDtypeStruct(q.shape, BF16),
        scratch_shapes=[
            pltpu.VMEM((rows, V7X_LANES), BF16),
            pltpu.VMEM((rows, V7X_LANES), F32),
            pltpu.VMEM((rows, V7X_LANES), F32),
            pltpu.VMEM((rows, V7X_LANES), F32),
            pltpu.VMEM((hq * tq, V7X_LANES), F32),
        ],
        compiler_params=_params(2, vmem),
        name="gqa_attn",
    )(*args)


def _gqa_bounded(q, k, v, kx, vx, gsum, *, hq, tq, tk, unroll):
    bsz, s, qw = q.shape
    t, kw = k.shape[1], k.shape[2]
    lx = kx.shape[1]
    assert kw == V7X_LANES and t % (tk * unroll) == 0
    rows = hq * tq
    stat = pltpu.VMEM((rows, V7X_LANES), F32)
    vmem = 4 * (t + lx) * kw * 2 + rows * max(tk, lx) * 8 * unroll + rows * V7X_LANES * 32 + 8 * tq * qw * 2
    return pl.pallas_call(
        functools.partial(_gqa_bounded_kernel, hq=hq, tq=tq, tk=tk, n_chunks=t // tk, unroll=unroll),
        grid=(bsz, s // tq),
        in_specs=[
            pl.BlockSpec((None, tq, qw), lambda b, i: (b, i, 0)),
            pl.BlockSpec((None, t, kw), lambda b, i: (b, 0, 0)),
            pl.BlockSpec((None, t, kw), lambda b, i: (b, 0, 0)),
            pl.BlockSpec((None, lx, kw), lambda b, i: (b, 0, 0)),
            pl.BlockSpec((None, lx, kw), lambda b, i: (b, 0, 0)),
            _resident(gsum.shape, lambda b, i: (0, 0)),
        ],
        out_specs=pl.BlockSpec((None, tq, qw), lambda b, i: (b, i, 0)),
        out_shape=jax.ShapeDtypeStruct(q.shape, BF16),
        scratch_shapes=[
            pltpu.VMEM((rows, V7X_LANES), BF16), stat, stat, stat, stat, stat,
            pltpu.VMEM((V7X_LANES, V7X_LANES), BF16),
        ],
        compiler_params=_params(2, vmem),
        name="gqa_attn_lat",
    )(q, k, v, kx, vx, gsum)


def _nbr_kernel(q_ref, k_ref, v_ref, kx_ref, vx_ref, *rest, n_blocks, rows_total):
    bias_refs, o_ref = rest[:n_blocks], rest[n_blocks]
    tq = NBR_QROWS * GRID_W
    n_win = NBR_KROWS * GRID_W
    kx = kx_ref[...]
    vx = vx_ref[...]
    lane = lax.broadcasted_iota(jnp.int32, (tq, V7X_LANES), 1)
    low_half = lane < HEAD_DIM
    starts, scores, probs = {}, {}, {}

    def qk(j):
        rb = pl.program_id(2) * n_blocks + j
        krow0 = jnp.clip(rb * NBR_QROWS - WIN_R // 2, 0, rows_total - NBR_KROWS)
        starts[j] = pl.multiple_of(krow0 * GRID_W, NBR_QROWS * GRID_W)
        kw = k_ref[pl.ds(starts[j], n_win), :]
        q = q_ref[j * tq:(j + 1) * tq, :]
        zero = jnp.zeros_like(q)
        qs = jnp.concatenate([jnp.where(low_half, q, zero), jnp.where(low_half, zero, q)], axis=0)
        bias = bias_refs[j][...].reshape(2 * tq, n_win)
        scores[j] = (lax.dot_general(qs, kw, _NT_DIMS, preferred_element_type=F32) + bias,
                     lax.dot_general(qs, kx, _NT_DIMS, preferred_element_type=F32))

    def soft(j):
        s_win, s_ctx = scores.pop(j)
        m = jnp.maximum(jnp.max(s_win, axis=1, keepdims=True), jnp.max(s_ctx, axis=1, keepdims=True))
        p_win = jnp.exp2(s_win - m)
        p_ctx = jnp.exp2(s_ctx - m)
        l = jnp.sum(p_win, axis=1, keepdims=True) + jnp.sum(p_ctx, axis=1, keepdims=True)
        probs[j] = (p_win.astype(BF16), p_ctx.astype(BF16), l)

    def pv(j):
        p_win, p_ctx, l = probs.pop(j)
        vw = v_ref[pl.ds(starts[j], n_win), :]
        o = (jnp.dot(p_win, vw, preferred_element_type=F32)
             + jnp.dot(p_ctx, vx, preferred_element_type=F32)) / l
        o_ref[j * tq:(j + 1) * tq, :] = jnp.where(low_half, o[:tq], o[tq:]).astype(BF16)

    for stage in (qk, soft, pv):
        for j in range(n_blocks):
            stage(j)


def _nbr(q, k, v, kx, vx, bias, *, n_blocks=1):
    bsz, s, w = q.shape
    rows_total = s // GRID_W
    n_rblocks = rows_total // NBR_QROWS
    assert n_rblocks % n_blocks == 0
    tq = NBR_QROWS * GRID_W
    n_win = NBR_KROWS * GRID_W
    lx = kx.shape[1]
    n_pairs = w // V7X_LANES

    def bias_spec(j):
        def bias_map(b, pr, st):
            rb = st * n_blocks + j
            kind = jnp.where(rb == 0, 0, jnp.where(rb == n_rblocks - 1, 2, 1))
            return (kind, pr, 0, 0)
        return pl.BlockSpec((None, 2, tq, n_win), bias_map)

    qo_spec = pl.BlockSpec((None, n_blocks * tq, V7X_LANES), lambda b, pr, st: (b, st, pr))
    vmem = 4 * s * V7X_LANES * 2 + n_blocks * (2 * 2 * tq * n_win * 4 + tq * (n_win + lx) * 24)
    return pl.pallas_call(
        functools.partial(_nbr_kernel, n_blocks=n_blocks, rows_total=rows_total),
        grid=(bsz, n_pairs, n_rblocks // n_blocks),
        in_specs=[
            qo_spec,
            pl.BlockSpec((None, s, V7X_LANES), lambda b, pr, st: (b, 0, pr)),
            pl.BlockSpec((None, s, V7X_LANES), lambda b, pr, st: (b, 0, pr)),
            pl.BlockSpec((None, lx, V7X_LANES), lambda b, pr, st: (b, 0, pr)),
            pl.BlockSpec((None, lx, V7X_LANES), lambda b, pr, st: (b, 0, pr)),
        ] + [bias_spec(j) for j in range(n_blocks)],
        out_specs=qo_spec,
        out_shape=jax.ShapeDtypeStruct(q.shape, BF16),
        compiler_params=_params(3, vmem),
        name="nbr_attn",
    )(q, k, v, kx, vx, *([bias] * n_blocks))


def _nbr_bias_tables(rpb, rows_total):
    assert rows_total % NBR_QROWS == 0 and rows_total >= NBR_KROWS + NBR_QROWS
    wr = min(WIN_R, rows_total)
    kinds = [(0, 0), (2 * NBR_QROWS, 2 * NBR_QROWS - WIN_R // 2),
             (rows_total - NBR_QROWS, rows_total - NBR_KROWS)]
    qi = np.arange(NBR_QROWS)[:, None, None, None]
    qc = np.arange(GRID_W)[None, :, None, None]
    kj = np.arange(NBR_KROWS)[None, None, :, None]
    kc = np.arange(GRID_W)[None, None, None, :]
    n_dr, n_dc = rpb.shape[1], rpb.shape[2]
    cs = np.clip(qc - WIN_C // 2, 0, GRID_W - WIN_C)
    col_ok = (kc >= cs) & (kc < cs + WIN_C)
    col_sel = (kc - qc + (WIN_C - 1))[..., None] == np.arange(n_dc)
    col_sel = (col_sel & col_ok[..., None])[0, :, 0].astype(np.float32)
    row_sel, ok_all = [], []
    for r0, k0 in kinds:
        r = r0 + qi
        rs = np.clip(r - wr // 2, 0, rows_total - wr)
        kr = k0 + kj
        row_ok = (kr >= rs) & (kr < rs + wr)
        sel = ((kr - r + (WIN_R - 1))[..., None] == np.arange(n_dr)) & row_ok[..., None]
        row_sel.append(sel[:, 0, :, 0].astype(np.float32))
        ok_all.append(np.broadcast_to(row_ok & col_ok, (NBR_QROWS, GRID_W, NBR_KROWS, GRID_W)))
    row_sel = jnp.asarray(np.stack(row_sel))
    ok = np.stack(ok_all).reshape(3, 1, NBR_QROWS * GRID_W, NBR_KROWS * GRID_W)
    hi = lax.Precision.HIGHEST
    rows_picked = jnp.einsum('hrd,tijr->thijd', rpb, row_sel, precision=hi)
    tab = jnp.einsum('thijd,cnd->thicjn', rows_picked, jnp.asarray(col_sel), precision=hi)
    tab = tab.reshape(3, rpb.shape[0], NBR_QROWS * GRID_W, NBR_KROWS * GRID_W)
    return jnp.where(ok, tab * LOG2_E, NEG).astype(F32)


def _oproj_kernel(h_ref, oa_ref, ob_ref, oc_ref, mod_ref, g_ref, w_ref, o_ref, *, n_sub):
    sub = h_ref.shape[0] // n_sub
    for t in range(n_sub):
        rows = slice(t * sub, (t + 1) * sub)
        o = jnp.concatenate([oa_ref[rows, :], ob_ref[rows, :], oc_ref[rows, :]], axis=1)
        y = jnp.dot(o, w_ref[...], preferred_element_type=F32)
        o_ref[rows, :] = h_ref[rows, :] + mod_ref[5:6, :] * _rms(y, g_ref[3:4, :])


def _oproj(h, oa, ob, oc, mods, norm_g, w_o, *, layer, mod_row, tm, n_sub=1):
    bsz, s, d = h.shape
    if mod_row is None:
        mod_map = lambda b, i: (layer, b, 0, 0)
    else:
        mod_map = lambda b, i: (layer, mod_row, 0, 0)
    tile = lambda w: pl.BlockSpec((None, tm, w), lambda b, i: (b, i, 0))
    vmem = w_o.shape[1] * d * 2 + tm * d * 4 * 8
    return pl.pallas_call(
        functools.partial(_oproj_kernel, n_sub=n_sub),
        grid=(bsz, s // tm),
        in_specs=[
            tile(d), tile(oa.shape[2]), tile(ob.shape[2]), tile(oc.shape[2]),
            pl.BlockSpec((None, None, N_MOD, d), mod_map),
            _resident((None, norm_g.shape[1], d), lambda b, i: (layer, 0, 0)),
            _resident((None, w_o.shape[1], d), lambda b, i: (layer, 0, 0)),
        ],
        out_specs=tile(d),
        out_shape=jax.ShapeDtypeStruct(h.shape, F32),
        compiler_params=_params(2, vmem),
        name="mix_out",
    )(h, oa, ob, oc, mods, norm_g, w_o)


def _rope_tables(seq):
    pos = jnp.arange(seq, dtype=jnp.int32)
    row = (pos // GRID_W).astype(F32)
    col = (pos % GRID_W).astype(F32)
    freq = 1.0 / (ROPE_THETA ** (jnp.arange(ROPE_FREQS, dtype=F32) / ROPE_FREQS))
    ar = row[:, None] * freq
    ac = col[:, None] * freq
    cos = jnp.concatenate([jnp.cos(ar), jnp.cos(ar), jnp.cos(ac), jnp.cos(ac)], axis=1)
    sin = jnp.concatenate([-jnp.sin(ar), jnp.sin(ar), -jnp.sin(ac), jnp.sin(ac)], axis=1)
    reps = V7X_LANES // HEAD_DIM
    return jnp.tile(cos, (1, reps)), jnp.tile(sin, (1, reps))


def _tile_rows(n, target):
    t = min(n, target)
    assert n % t == 0
    return t


def kernel(x, c, ctx, c_ctx, w_ada, b_ada, norm_g, w_in, qk_g, rpb, conv_w, w_o, ffn_wi, ffn_wo):
    bsz, seq, d = x.shape
    ctx_len = ctx.shape[1]
    depth = w_ada.shape[0]
    ctx_row = bsz

    mod_rows = -(-(bsz + 1) // 8) * 8
    c_all = jnp.zeros((mod_rows, d), F32).at[:bsz].set(c).at[ctx_row].set(c_ctx)
    mods = _ada(c_all, w_ada, b_ada).reshape(depth, mod_rows, N_MOD, d)

    w_in_b = w_in.astype(BF16)
    w_o_b = w_o.astype(BF16)
    wi_b = ffn_wi.astype(BF16)
    wo_b = ffn_wo.astype(BF16)
    qk_gain = jnp.concatenate(
        [jnp.tile(qk_g[:, 0], (1, A_Q_HEADS)), jnp.tile(qk_g[:, 1], (1, A_KV_HEADS))], axis=1
    ).reshape(depth, 1, A_Q_W + A_KV_W)
    gsz = (A_Q_W + A_KV_W) // 2
    head_of = np.arange(gsz) // HEAD_DIM
    gmat = jnp.asarray((head_of[:, None] == head_of[None, :]) / HEAD_DIM, dtype=BF16)
    lane_head = np.arange(V7X_LANES) // HEAD_DIM
    gsum = jnp.asarray(lane_head[:, None] == lane_head[None, :], dtype=BF16)
    rope_tabs = _rope_tables(seq)

    tm_ffn = _tile_rows(seq, 512)
    tm_proj = _tile_rows(seq, 1024)
    tm_ctx = _tile_rows(ctx_len, 256)
    tq_a = _tile_rows(seq, 256)
    tk_a = _tile_rows(seq, 512)
    unroll_a = min(GQA_UNROLL, seq // tk_a)
    n_rblocks = seq // (GRID_W * NBR_QROWS)
    nbr_blocks = NBR_BLOCKS_PER_STEP if n_rblocks % NBR_BLOCKS_PER_STEP == 0 else 1

    h, hc = x, ctx
    for layer in range(depth):
        last = layer == depth - 1
        lat = dict(layer=layer, mod_row=None, n_sub=ROW_SUBTILES)
        cx = dict(layer=layer, mod_row=ctx_row, n_sub=1)
        h = _ffn(h, mods, norm_g, wi_b, wo_b, which=0, tm=tm_ffn, **lat)
        hc = _ffn(hc, mods, norm_g, wi_b, wo_b, which=0, tm=tm_ctx, **cx)
        qa, qb, ka, va, kb, vb, oc = _proj(h, mods, norm_g, w_in_b, qk_gain, conv_w, gmat, rope_tabs,
                                           tm=tm_proj, layer=layer, mod_row=None, n_sub=PROJ_SUBTILES)
        cqa, cqb, cka, cva, ckb, cvb, coc = _proj(hc, mods, norm_g, w_in_b, qk_gain, conv_w, gmat, None,
                                                  tm=tm_ctx, layer=layer, mod_row=ctx_row)
        if not last:
            coa = _gqa(cqa, cka, cva, None, hq=A_Q_HEADS, hkv=A_KV_HEADS, tq=tm_ctx, tk=ctx_len)
            cob = _gqa(cqb, ckb, cvb, None, hq=B_HEADS, hkv=B_HEADS, tq=tm_ctx, tk=ctx_len)
            hc = _ffn(hc, mods, norm_g, wi_b, wo_b, which=1, tm=tm_ctx, mix=(coa, cob, coc, w_o_b), **cx)
        oa = _gqa_bounded(qa, ka, va, cka, cva, gsum, hq=A_Q_HEADS, tq=tq_a, tk=tk_a, unroll=unroll_a)
        ob = _nbr(qb, kb, vb, ckb, cvb, _nbr_bias_tables(rpb[layer], seq // GRID_W), n_blocks=nbr_blocks)
        h = _oproj(h, oa, ob, oc, mods, norm_g, w_o_b, tm=tm_proj, **lat)
        h = _ffn(h, mods, norm_g, wi_b, wo_b, which=1, tm=tm_ffn, **lat)
    return h
```

```python
import functools

import numpy as np
import jax
import jax.numpy as jnp
from jax import lax
from jax.experimental import pallas as pl
from jax.experimental.pallas import tpu as pltpu

F32 = jnp.float32
BF16 = jnp.bfloat16

HEAD_DIM = 64
GRID_W = 64
A_Q_HEADS = 6
A_KV_HEADS = 2
B_HEADS = 6
C_WIDTH = 256
A_Q_W = A_Q_HEADS * HEAD_DIM
A_KV_W = A_KV_HEADS * HEAD_DIM
B_W = B_HEADS * HEAD_DIM
WIN_R = 8
WIN_C = 16
ROPE_FREQS = HEAD_DIM // 4
ROPE_THETA = 10000.0
N_MOD = 9
EPS = 1e-6
NEG = -1e30
LOG2_E = 1.4426950408889634

V7X_LANES = 128
V7X_SCOPED_VMEM_BYTES = 60000 * 1024

NBR_QROWS = 4
NBR_KROWS = NBR_QROWS + WIN_R

GQA_UNROLL = 8
GQA_FALLBACK_UNROLL = 2
ROW_SUBTILES = 4
PROJ_SUBTILES = 8
NBR_BLOCKS_PER_STEP = 4


def _vmem_limit(estimate_bytes):
    return int(min(max(estimate_bytes, 16 * 1024 * 1024), V7X_SCOPED_VMEM_BYTES))


def _params(n_axes, vmem_bytes):
    return pltpu.CompilerParams(
        dimension_semantics=("arbitrary",) * n_axes,
        vmem_limit_bytes=_vmem_limit(vmem_bytes),
    )


def _rms(x, g):
    ms = jnp.mean(x * x, axis=-1, keepdims=True)
    return x * lax.rsqrt(ms + EPS) * g


def _resident(block_shape, index_map):
    return pl.BlockSpec(block_shape, index_map, pipeline_mode=pl.Buffered(1))


def _ada_kernel(c_ref, w_ref, b_ref, o_ref):
    c = c_ref[...]
    sc = c * jax.nn.sigmoid(c)
    o_ref[...] = jnp.dot(sc, w_ref[...], preferred_element_type=F32,
                         precision=lax.Precision.HIGHEST) + b_ref[...]


def _ada(c_all, w_ada, b_ada):
    depth, d, n = w_ada.shape
    rows = c_all.shape[0]
    tn = d
    return pl.pallas_call(
        _ada_kernel,
        grid=(depth, n // tn),
        in_specs=[
            pl.BlockSpec((rows, d), lambda l, j: (0, 0)),
            pl.BlockSpec((None, d, tn), lambda l, j: (l, 0, j)),
            pl.BlockSpec((None, 1, tn), lambda l, j: (l, 0, j)),
        ],
        out_specs=pl.BlockSpec((None, rows, tn), lambda l, j: (l, 0, j)),
        out_shape=jax.ShapeDtypeStruct((depth, rows, n), F32),
        compiler_params=_params(2, 4 * d * tn * 4),
        name="ada_mod",
    )(c_all, w_ada, b_ada.reshape(depth, 1, n))


def _ffn_kernel(*refs, i0, gi, ffn_dim, n_sub, mix):
    if mix:
        h_ref, oa_ref, ob_ref, oc_ref, mod_ref, g_ref, wmix_ref, wi_ref, wo_ref, o_ref = refs
    else:
        h_ref, mod_ref, g_ref, wi_ref, wo_ref, o_ref = refs
    shift = mod_ref[i0:i0 + 1, :]
    scale = mod_ref[i0 + 1:i0 + 2, :]
    gate = mod_ref[i0 + 2:i0 + 3, :]
    sub = h_ref.shape[0] // n_sub
    rows_of = lambda t: slice(t * sub, (t + 1) * sub)
    hs, hids, acts = {}, {}, {}

    def up_proj(t):
        h = h_ref[rows_of(t), :]
        if mix:
            o = jnp.concatenate([oa_ref[rows_of(t), :], ob_ref[rows_of(t), :], oc_ref[rows_of(t), :]], axis=1)
            ymix = jnp.dot(o, wmix_ref[...], preferred_element_type=F32)
            h = h + mod_ref[5:6, :] * _rms(ymix, g_ref[3:4, :])
        hs[t] = h
        u = _rms(h, g_ref[gi:gi + 1, :]) * (1.0 + scale) + shift
        hids[t] = jnp.dot(u.astype(BF16), wi_ref[...], preferred_element_type=F32)

    def activate(t):
        hid = hids.pop(t)
        gt = hid[:, :ffn_dim]
        up = hid[:, ffn_dim:]
        acts[t] = (gt * jax.nn.sigmoid(gt) * up).astype(BF16)

    def down_proj(t):
        y = jnp.dot(acts.pop(t), wo_ref[...], preferred_element_type=F32)
        o_ref[rows_of(t), :] = hs.pop(t) + 0.5 * gate * _rms(y, g_ref[gi + 1:gi + 2, :])

    for t in range(n_sub):
        up_proj(t)
        activate(t)
        down_proj(t)


def _ffn(h, mods, norm_g, wi, wo, *, layer, which, mod_row, tm, n_sub=1, mix=None):
    bsz, s, d = h.shape
    ffn_dim = wo.shape[2]
    i0 = 6 * which
    gi = 4 * which
    if mod_row is None:
        mod_map = lambda b, i: (layer, b, 0, 0)
    else:
        mod_map = lambda b, i: (layer, mod_row, 0, 0)
    tile = lambda w: pl.BlockSpec((None, tm, w), lambda b, i: (b, i, 0))
    vmem = ((wi.shape[2] * wi.shape[3] + wo.shape[2] * wo.shape[3]) * 2 + tm * d * 4 * 6
            + (tm // n_sub) * ffn_dim * 24)
    in_specs = [tile(d)]
    args = [h]
    if mix is not None:
        in_specs += [tile(o.shape[2]) for o in mix[:3]]
        args += list(mix[:3])
    in_specs += [pl.BlockSpec((None, None, N_MOD, d), mod_map),
                 _resident((None, norm_g.shape[1], d), lambda b, i: (layer, 0, 0))]
    args += [mods, norm_g]
    if mix is not None:
        w_mix = mix[3]
        in_specs.append(_resident((None, w_mix.shape[1], d), lambda b, i: (layer, 0, 0)))
        args.append(w_mix)
        vmem += w_mix.shape[1] * d * 2 + tm * d * 2 * 2
    in_specs += [_resident((None, None, d, 2 * ffn_dim), lambda b, i: (layer, which, 0, 0)),
                 _resident((None, None, ffn_dim, d), lambda b, i: (layer, which, 0, 0))]
    args += [wi, wo]
    return pl.pallas_call(
        functools.partial(_ffn_kernel, i0=i0, gi=gi, ffn_dim=ffn_dim, n_sub=n_sub, mix=mix is not None),
        grid=(bsz, s // tm),
        in_specs=in_specs,
        out_specs=tile(d),
        out_shape=jax.ShapeDtypeStruct(h.shape, F32),
        compiler_params=_params(2, vmem),
        name="mix_out_ffn" if mix is not None else "ffn",
    )(*args)


def _swap_rope_partners(x):
    lane = lax.broadcasted_iota(jnp.int32, x.shape, 1)
    first = (lane % (2 * ROPE_FREQS)) < ROPE_FREQS
    return jnp.where(first,
                     pltpu.roll(x, V7X_LANES - ROPE_FREQS, axis=1),
                     pltpu.roll(x, ROPE_FREQS, axis=1))


def _proj_kernel(*refs, rope, tm, n_tiles, n_sub):
    if rope:
        (h_ref, hp_ref, hn_ref, mod_ref, g_ref, w_ref, qkg_ref, cw_ref, gm_ref, cos_ref, sin_ref,
         qa_ref, qb_ref, ka_ref, va_ref, kb_ref, vb_ref, oc_ref, z_sc, cb_sc) = refs
    else:
        (h_ref, hp_ref, hn_ref, mod_ref, g_ref, w_ref, qkg_ref, cw_ref, gm_ref,
         qa_ref, qb_ref, ka_ref, va_ref, kb_ref, vb_ref, oc_ref, z_sc, cb_sc) = refs
    i = pl.program_id(1)
    shift = mod_ref[3:4, :]
    scale = mod_ref[4:5, :]
    g2 = g_ref[2:3, :]

    def pre(x):
        return (_rms(x, g2) * (1.0 + scale) + shift).astype(BF16)

    o_qa, o_qb = 0, A_Q_W
    o_ka = o_qb + B_W
    o_va = o_ka + A_KV_W
    o_kb = o_va + A_KV_W
    o_vb = o_kb + B_W
    o_cx = o_vb + B_W
    o_cb = o_cx + C_WIDTH
    o_cc = o_cb + C_WIDTH
    q_scale = HEAD_DIM ** -0.5 * LOG2_E
    half = (A_Q_W + A_KV_W) // 2
    sub = tm // n_sub

    for t in range(n_sub):
        rows = slice(t * sub, (t + 1) * sub)
        p = jnp.dot(pre(h_ref[rows, :]), w_ref[...], preferred_element_type=F32)
        xq = jnp.concatenate([p[:, o_qa:o_qa + A_Q_W], p[:, o_ka:o_ka + A_KV_W]], axis=1)
        sq = (xq * xq).astype(BF16)
        ms = jnp.concatenate(
            [jnp.dot(sq[:, :half], gm_ref[...], preferred_element_type=F32),
             jnp.dot(sq[:, half:], gm_ref[...], preferred_element_type=F32)], axis=1)
        xn = xq * lax.rsqrt(ms + EPS) * qkg_ref[...]
        if rope:
            cos = cos_ref[rows, :]
            sin = sin_ref[rows, :]
            cols = []
            for j in range((A_Q_W + A_KV_W) // V7X_LANES):
                xc = xn[:, j * V7X_LANES:(j + 1) * V7X_LANES]
                cols.append(xc * cos + _swap_rope_partners(xc) * sin)
            xn = jnp.concatenate(cols, axis=1)
        qa_ref[rows, :] = (xn[:, :A_Q_W] * q_scale).astype(BF16)
        ka_ref[rows, :] = xn[:, A_Q_W:].astype(BF16)
        qb_ref[rows, :] = (p[:, o_qb:o_qb + B_W] * q_scale).astype(BF16)
        va_ref[rows, :] = p[:, o_va:o_va + A_KV_W].astype(BF16)
        kb_ref[rows, :] = p[:, o_kb:o_kb + B_W].astype(BF16)
        vb_ref[rows, :] = p[:, o_vb:o_vb + B_W].astype(BF16)
        z_sc[rows, :] = p[:, o_cc:o_cc + C_WIDTH] * p[:, o_cx:o_cx + C_WIDTH]
        cb_sc[rows, :] = p[:, o_cb:o_cb + C_WIDTH]

    z = z_sc[...]

    def halo(ref):
        ub = pre(ref[...])
        return (jnp.dot(ub, w_ref[:, o_cc:o_cc + C_WIDTH], preferred_element_type=F32)
                * jnp.dot(ub, w_ref[:, o_cx:o_cx + C_WIDTH], preferred_element_type=F32))

    halo_rows = hp_ref.shape[0]
    z_before = jnp.where(i > 0, halo(hp_ref)[halo_rows - 1:halo_rows, :], 0.0)
    z_after = jnp.where(i < n_tiles - 1, halo(hn_ref)[0:1, :], 0.0)
    row = lax.broadcasted_iota(jnp.int32, z.shape, 0)
    z_m1 = jnp.where(row == 0, z_before, pltpu.roll(z, 1, axis=0))
    z_p1 = jnp.where(row == tm - 1, z_after, pltpu.roll(z, tm - 1, axis=0))
    y = cw_ref[0:1, :] * z_m1 + cw_ref[1:2, :] * z + cw_ref[2:3, :] * z_p1
    oc_ref[...] = (cb_sc[...] * y).astype(BF16)


def _proj(h, mods, norm_g, w_in, qk_gain, conv_w, gmat, rope_tabs, *, layer, mod_row, tm, n_sub=1):
    bsz, s, d = h.shape
    n_tiles = s // tm
    halo_rows = 8
    hb = tm // halo_rows
    n_hblk = s // halo_rows
    rope = rope_tabs is not None
    if mod_row is None:
        mod_map = lambda b, i: (layer, b, 0, 0)
    else:
        mod_map = lambda b, i: (layer, mod_row, 0, 0)
    pw = w_in.shape[2]
    in_specs = [
        pl.BlockSpec((None, tm, d), lambda b, i: (b, i, 0)),
        pl.BlockSpec((None, halo_rows, d), lambda b, i: (b, jnp.maximum(i * hb - 1, 0), 0)),
        pl.BlockSpec((None, halo_rows, d), lambda b, i: (b, jnp.minimum((i + 1) * hb, n_hblk - 1), 0)),
        pl.BlockSpec((None, None, N_MOD, d), mod_map),
        _resident((None, norm_g.shape[1], d), lambda b, i: (layer, 0, 0)),
        _resident((None, d, pw), lambda b, i: (layer, 0, 0)),
        _resident((None, 1, A_Q_W + A_KV_W), lambda b, i: (layer, 0, 0)),
        _resident((None, conv_w.shape[1], C_WIDTH), lambda b, i: (layer, 0, 0)),
        _resident(gmat.shape, lambda b, i: (0, 0)),
    ]
    args = [h, h, h, mods, norm_g, w_in, qk_gain, conv_w, gmat]
    if rope:
        in_specs += [pl.BlockSpec((tm, V7X_LANES), lambda b, i: (i, 0))] * 2
        args += list(rope_tabs)
    widths = (A_Q_W, B_W, A_KV_W, A_KV_W, B_W, B_W, C_WIDTH)
    out_specs = [pl.BlockSpec((None, tm, w), lambda b, i: (b, i, 0)) for w in widths]
    out_shape = [jax.ShapeDtypeStruct((bsz, s, w), BF16) for w in widths]
    vmem = d * pw * 2 + tm * d * 4 * 4 + tm * pw * 12
    return pl.pallas_call(
        functools.partial(_proj_kernel, rope=rope, tm=tm, n_tiles=n_tiles, n_sub=n_sub),
        grid=(bsz, n_tiles),
        in_specs=in_specs,
        out_specs=out_specs,
        out_shape=out_shape,
        scratch_shapes=[pltpu.VMEM((tm, C_WIDTH), F32), pltpu.VMEM((tm, C_WIDTH), F32)],
        compiler_params=_params(2, vmem),
        name="mix_proj",
    )(*args)


_NT_DIMS = (((1,), (1,)), ((), ()))

SCORE_BOUND_MAX = 40.0
SCORE_BOUND_PIVOT = 16.0


def _load_q_rows(q_ref, qs_sc, heads, grp, tq):
    lane = lax.broadcasted_iota(jnp.int32, (tq, V7X_LANES), 1)
    low_half = lane < HEAD_DIM
    for j, h in enumerate(heads):
        xc = q_ref[:, (h // 2) * V7X_LANES:(h // 2 + 1) * V7X_LANES].astype(F32)
        dst_low = (h // grp) % 2 == 0
        if (h % 2 == 0) != dst_low:
            xc = pltpu.roll(xc, HEAD_DIM, axis=1)
        keep = low_half if dst_low else jnp.logical_not(low_half)
        qs_sc[j * tq:(j + 1) * tq, :] = jnp.where(keep, xc, 0.0).astype(BF16)


def _repack_heads(on_sc, hq, grp, tq):
    lane = lax.broadcasted_iota(jnp.int32, (tq, V7X_LANES), 1)
    low_half = lane < HEAD_DIM
    cols = []
    for oc in range(hq // 2):
        pieces = []
        for e in range(2):
            h = 2 * oc + e
            piece = on_sc[h * tq:(h + 1) * tq, :]
            src_low = (h // grp) % 2 == 0
            if src_low != (e == 0):
                piece = pltpu.roll(piece, HEAD_DIM, axis=1)
            pieces.append(piece)
        cols.append(jnp.where(low_half, pieces[0], pieces[1]).astype(BF16))
    return cols


def _online_softmax(qs_sc, m_sc, l_sc, acc_sc, rows, chunks, extra, *, tk, n_chunks, unroll):
    m_sc[0:rows, :] = jnp.full((rows, V7X_LANES), NEG, F32)
    l_sc[0:rows, :] = jnp.zeros((rows, V7X_LANES), F32)
    acc_sc[0:rows, :] = jnp.zeros((rows, V7X_LANES), F32)

    def load_state():
        return m_sc[0:rows, :], l_sc[0:rows, :], acc_sc[0:rows, :]

    def store_state(state):
        m_sc[0:rows, :], l_sc[0:rows, :], acc_sc[0:rows, :] = state

    def step(state, kc, vc):
        m_prev, l_prev, acc_prev = state
        s = lax.dot_general(qs_sc[0:rows, :], kc, _NT_DIMS, preferred_element_type=F32)
        m_next = jnp.maximum(m_prev, jnp.max(s, axis=1, keepdims=True))
        alpha = jnp.exp2(m_prev - m_next)
        p = jnp.exp2(s - jnp.concatenate([m_next] * (kc.shape[0] // V7X_LANES), axis=1))
        l_next = alpha * l_prev + jnp.sum(p, axis=1, keepdims=True)
        acc_next = alpha * acc_prev + jnp.dot(p.astype(BF16), vc, preferred_element_type=F32)
        return m_next, l_next, acc_next

    def body(c, carry):
        state = load_state()
        for u in range(unroll):
            state = step(state, *chunks(pl.multiple_of((c * unroll + u) * tk, tk)))
        store_state(state)
        return carry

    lax.fori_loop(0, n_chunks // unroll, body, 0)
    if extra is not None:
        store_state(step(load_state(), *extra))
    return acc_sc[0:rows, :] / l_sc[0:rows, :]


def _bounded_softmax(qs_sc, b_sc, l_sc, acc_sc, rows, chunks, extra, *, tk, n_chunks, unroll):
    l_sc[0:rows, :] = jnp.zeros((rows, V7X_LANES), F32)
    acc_sc[0:rows, :] = jnp.zeros((rows, V7X_LANES), F32)

    def step(state, kc, vc):
        l_prev, acc_prev = state
        n_cols = kc.shape[0] // V7X_LANES
        s = lax.dot_general(qs_sc[0:rows, :], kc, _NT_DIMS, preferred_element_type=F32)
        p = jnp.exp2(s - jnp.concatenate([b_sc[0:rows, :]] * n_cols, axis=1))
        l_next = l_prev
        for j in range(n_cols):
            l_next = l_next + p[:, j * V7X_LANES:(j + 1) * V7X_LANES]
        acc_next = acc_prev + jnp.dot(p.astype(BF16), vc, preferred_element_type=F32)
        return l_next, acc_next

    def body(c, carry):
        state = (l_sc[0:rows, :], acc_sc[0:rows, :])
        for u in range(unroll):
            state = step(state, *chunks(pl.multiple_of((c * unroll + u) * tk, tk)))
        l_sc[0:rows, :], acc_sc[0:rows, :] = state
        return carry

    n_trips = n_chunks // unroll
    lax.fori_loop(0, n_trips - 1, body, 0)
    state = (l_sc[0:rows, :], acc_sc[0:rows, :])
    for u in range(unroll):
        state = step(state, *chunks(((n_trips - 1) * unroll + u) * tk))
    if extra is not None:
        state = step(state, *extra)
    l_lanes, acc = state
    return acc / jnp.sum(l_lanes, axis=1, keepdims=True)


def _gqa_kernel(*refs, hq, hkv, tq, tk, n_chunks, unroll, extra_len):
    if extra_len:
        q_ref, k_ref, v_ref, kx_ref, vx_ref, o_ref, qs_sc, m_sc, l_sc, acc_sc, on_sc = refs
    else:
        q_ref, k_ref, v_ref, o_ref, qs_sc, m_sc, l_sc, acc_sc, on_sc = refs
    grp = hq // hkv
    for pc in range(hkv // 2):
        heads = [h for h in range(hq) if (h // grp) // 2 == pc]
        rows = len(heads) * tq
        col = slice(pc * V7X_LANES, (pc + 1) * V7X_LANES)
        _load_q_rows(q_ref, qs_sc, heads, grp, tq)
        chunks = lambda start: (k_ref[pl.ds(start, tk), col], v_ref[pl.ds(start, tk), col])
        extra = (kx_ref[:, col], vx_ref[:, col]) if extra_len else None
        on = _online_softmax(qs_sc, m_sc, l_sc, acc_sc, rows, chunks, extra,
                             tk=tk, n_chunks=n_chunks, unroll=unroll)
        for j, h in enumerate(heads):
            on_sc[h * tq:(h + 1) * tq, :] = on[j * tq:(j + 1) * tq, :]
    o_ref[...] = jnp.concatenate(_repack_heads(on_sc, hq, grp, tq), axis=1)


def _gqa_bounded_kernel(q_ref, k_ref, v_ref, kx_ref, vx_ref, gs_ref, h_ref, ob_ref, oc_ref, mod_ref, g_ref,
                        wmix_ref, o_ref, qs_sc, m_sc, l_sc, acc_sc, on_sc, b_sc, kmax_sc,
                        *, hq, tq, tk, n_chunks, unroll, n_q):
    hkv = 2
    grp = hq // hkv
    rows = hq * tq
    step = pl.program_id(1)

    def max_sq_norm(kc, mx):
        kf = kc.astype(F32)
        ss = lax.dot_general(gs_ref[...], (kf * kf).astype(BF16), _NT_DIMS, preferred_element_type=F32)
        return jnp.maximum(mx, jnp.max(ss, axis=1, keepdims=True))

    @pl.when(step == 0)
    def _():
        def kbody(c, mx):
            return max_sq_norm(k_ref[pl.ds(pl.multiple_of(c * tk, tk), tk), :], mx)
        mx = lax.fori_loop(0, n_chunks, kbody, jnp.zeros((V7X_LANES, 1), F32))
        kmax_sc[...] = jnp.broadcast_to(max_sq_norm(kx_ref[...], mx), kmax_sc.shape).astype(BF16)
        on_sc[...] = jnp.zeros(on_sc.shape, F32)

    o = jnp.concatenate(_repack_heads(on_sc, hq, grp, tq) + [ob_ref[...], oc_ref[...]], axis=1)
    y = jnp.dot(o, wmix_ref[...], preferred_element_type=F32)
    o_ref[...] = h_ref[...] + mod_ref[5:6, :] * _rms(y, g_ref[3:4, :])

    _load_q_rows(q_ref, qs_sc, list(range(hq)), grp, tq)
    qf = qs_sc[...].astype(F32)
    qk_sq = jnp.dot((qf * qf).astype(BF16), kmax_sc[...], preferred_element_type=F32)
    bound = (qk_sq * (0.5 / SCORE_BOUND_PIVOT) + 0.5 * SCORE_BOUND_PIVOT) * (1.0 + 2.0 ** -5)
    b_sc[...] = bound
    bounded_ok = jnp.max(bound) <= SCORE_BOUND_MAX
    has_tile = step < n_q

    chunks = lambda start: (k_ref[pl.ds(start, tk), :], v_ref[pl.ds(start, tk), :])
    extra = (kx_ref[...], vx_ref[...])

    @pl.when(jnp.logical_and(has_tile, bounded_ok))
    def _():
        on_sc[...] = _bounded_softmax(qs_sc, b_sc, l_sc, acc_sc, rows, chunks, extra,
                                      tk=tk, n_chunks=n_chunks, unroll=unroll)

    @pl.when(jnp.logical_and(has_tile, jnp.logical_not(bounded_ok)))
    def _():
        on_sc[...] = _online_softmax(qs_sc, m_sc, l_sc, acc_sc, rows, chunks, extra,
                                     tk=tk, n_chunks=n_chunks, unroll=min(unroll, GQA_FALLBACK_UNROLL))


def _gqa(q, k, v, extra, *, hq, hkv, tq, tk, unroll=1):
    bsz, s, qw = q.shape
    t = k.shape[1]
    kw = k.shape[2]
    n_chunks = t // tk
    assert n_chunks % unroll == 0
    grp = hq // hkv
    max_heads = max(sum(1 for h in range(hq) if (h // grp) // 2 == pc) for pc in range(hkv // 2))
    rows = max_heads * tq
    in_specs = [
        pl.BlockSpec((None, tq, qw), lambda b, i: (b, i, 0)),
        pl.BlockSpec((None, t, kw), lambda b, i: (b, 0, 0)),
        pl.BlockSpec((None, t, kw), lambda b, i: (b, 0, 0)),
    ]
    args = [q, k, v]
    extra_len = 0
    if extra is not None:
        extra_len = extra[0].shape[1]
        in_specs += [pl.BlockSpec((None, extra_len, kw), lambda b, i: (b, 0, 0))] * 2
        args += list(extra)
    vmem = 4 * t * kw * 2 + rows * max(tk, extra_len) * 16 + rows * V7X_LANES * 24 + 8 * tq * qw * 2
    return pl.pallas_call(
        functools.partial(_gqa_kernel, hq=hq, hkv=hkv, tq=tq, tk=tk, n_chunks=n_chunks, unroll=unroll,
                          extra_len=extra_len),
        grid=(bsz, s // tq),
        in_specs=in_specs,
        out_specs=pl.BlockSpec((None, tq, qw), lambda b, i: (b, i, 0)),
        out_shape=jax.ShapeDtypeStruct(q.shape, BF16),
        scratch_shapes=[
            pltpu.VMEM((rows, V7X_LANES), BF16),
            pltpu.VMEM((rows, V7X_LANES), F32),
            pltpu.VMEM((rows, V7X_LANES), F32),
            pltpu.VMEM((rows, V7X_LANES), F32),
            pltpu.VMEM((hq * tq, V7X_LANES), F32),
        ],
        compiler_params=_params(2, vmem),
        name="gqa_attn",
    )(*args)


def _gqa_bounded(q, k, v, kx, vx, gsum, h, ob, oc, mods, norm_g, w_o, *, layer, hq, tq, tk, unroll):
    bsz, s, qw = q.shape
    d = h.shape[2]
    t, kw = k.shape[1], k.shape[2]
    lx = kx.shape[1]
    n_q = s // tq
    assert kw == V7X_LANES and t % (tk * unroll) == 0
    rows = hq * tq
    stat = pltpu.VMEM((rows, V7X_LANES), F32)
    keys = lambda n: pl.BlockSpec((None, n, kw), lambda b, i: (b, 0, 0))
    prev = lambda w: pl.BlockSpec((None, tq, w), lambda b, i: (b, jnp.maximum(i - 1, 0), 0))
    vmem = (4 * (t + lx) * kw * 2 + rows * max(tk, lx) * 8 * min(unroll, 4) + rows * V7X_LANES * 32
            + 8 * tq * qw * 2 + w_o.shape[1] * d * 2 + 6 * tq * d * 4)
    return pl.pallas_call(
        functools.partial(_gqa_bounded_kernel, hq=hq, tq=tq, tk=tk, n_chunks=t // tk, unroll=unroll, n_q=n_q),
        grid=(bsz, n_q + 1),
        in_specs=[
            pl.BlockSpec((None, tq, qw), lambda b, i: (b, jnp.minimum(i, n_q - 1), 0)),
            keys(t), keys(t), keys(lx), keys(lx),
            _resident(gsum.shape, lambda b, i: (0, 0)),
            prev(d), prev(ob.shape[2]), prev(oc.shape[2]),
            pl.BlockSpec((None, None, N_MOD, d), lambda b, i: (layer, b, 0, 0)),
            _resident((None, norm_g.shape[1], d), lambda b, i: (layer, 0, 0)),
            _resident((None, w_o.shape[1], d), lambda b, i: (layer, 0, 0)),
        ],
        out_specs=prev(d),
        out_shape=jax.ShapeDtypeStruct(h.shape, F32),
        scratch_shapes=[
            pltpu.VMEM((rows, V7X_LANES), BF16), stat, stat, stat, stat, stat,
            pltpu.VMEM((V7X_LANES, V7X_LANES), BF16),
        ],
        compiler_params=_params(2, vmem),
        name="gqa_attn_mix_out",
    )(q, k, v, kx, vx, gsum, h, ob, oc, mods, norm_g, w_o)


def _nbr_kernel(q_ref, k_ref, v_ref, kx_ref, vx_ref, *rest, n_blocks, rows_total):
    bias_refs, o_ref = rest[:n_blocks], rest[n_blocks]
    tq = NBR_QROWS * GRID_W
    n_win = NBR_KROWS * GRID_W
    kx = kx_ref[...]
    vx = vx_ref[...]
    lane = lax.broadcasted_iota(jnp.int32, (tq, V7X_LANES), 1)
    low_half = lane < HEAD_DIM
    starts, scores, probs = {}, {}, {}

    def qk(j):
        rb = pl.program_id(2) * n_blocks + j
        krow0 = jnp.clip(rb * NBR_QROWS - WIN_R // 2, 0, rows_total - NBR_KROWS)
        starts[j] = pl.multiple_of(krow0 * GRID_W, NBR_QROWS * GRID_W)
        kw = k_ref[pl.ds(starts[j], n_win), :]
        q = q_ref[j * tq:(j + 1) * tq, :]
        zero = jnp.zeros_like(q)
        qs = jnp.concatenate([jnp.where(low_half, q, zero), jnp.where(low_half, zero, q)], axis=0)
        bias = bias_refs[j][...].reshape(2 * tq, n_win)
        scores[j] = (lax.dot_general(qs, kw, _NT_DIMS, preferred_element_type=F32) + bias,
                     lax.dot_general(qs, kx, _NT_DIMS, preferred_element_type=F32))

    def soft(j):
        s_win, s_ctx = scores.pop(j)
        m = jnp.maximum(jnp.max(s_win, axis=1, keepdims=True), jnp.max(s_ctx, axis=1, keepdims=True))
        p_win = jnp.exp2(s_win - m)
        p_ctx = jnp.exp2(s_ctx - m)
        l = jnp.sum(p_win, axis=1, keepdims=True) + jnp.sum(p_ctx, axis=1, keepdims=True)
        probs[j] = (p_win.astype(BF16), p_ctx.astype(BF16), l)

    def pv(j):
        p_win, p_ctx, l = probs.pop(j)
        vw = v_ref[pl.ds(starts[j], n_win), :]
        o = (jnp.dot(p_win, vw, preferred_element_type=F32)
             + jnp.dot(p_ctx, vx, preferred_element_type=F32)) / l
        o_ref[j * tq:(j + 1) * tq, :] = jnp.where(low_half, o[:tq], o[tq:]).astype(BF16)

    for stage in (qk, soft, pv):
        for j in range(n_blocks):
            stage(j)


def _nbr(q, k, v, kx, vx, bias, *, n_blocks=1):
    bsz, s, w = q.shape
    rows_total = s // GRID_W
    n_rblocks = rows_total // NBR_QROWS
    assert n_rblocks % n_blocks == 0
    tq = NBR_QROWS * GRID_W
    n_win = NBR_KROWS * GRID_W
    lx = kx.shape[1]
    n_pairs = w // V7X_LANES

    def bias_spec(j):
        def bias_map(b, pr, st):
            rb = st * n_blocks + j
            kind = jnp.where(rb == 0, 0, jnp.where(rb == n_rblocks - 1, 2, 1))
            return (kind, pr, 0, 0)
        return pl.BlockSpec((None, 2, tq, n_win), bias_map)

    qo_spec = pl.BlockSpec((None, n_blocks * tq, V7X_LANES), lambda b, pr, st: (b, st, pr))
    vmem = 4 * s * V7X_LANES * 2 + n_blocks * (2 * 2 * tq * n_win * 4 + tq * (n_win + lx) * 24)
    return pl.pallas_call(
        functools.partial(_nbr_kernel, n_blocks=n_blocks, rows_total=rows_total),
        grid=(bsz, n_pairs, n_rblocks // n_blocks),
        in_specs=[
            qo_spec,
            pl.BlockSpec((None, s, V7X_LANES), lambda b, pr, st: (b, 0, pr)),
            pl.BlockSpec((None, s, V7X_LANES), lambda b, pr, st: (b, 0, pr)),
            pl.BlockSpec((None, lx, V7X_LANES), lambda b, pr, st: (b, 0, pr)),
            pl.BlockSpec((None, lx, V7X_LANES), lambda b, pr, st: (b, 0, pr)),
        ] + [bias_spec(j) for j in range(n_blocks)],
        out_specs=qo_spec,
        out_shape=jax.ShapeDtypeStruct(q.shape, BF16),
        compiler_params=_params(3, vmem),
        name="nbr_attn",
    )(q, k, v, kx, vx, *([bias] * n_blocks))


def _nbr_bias_tables(rpb, rows_total):
    assert rows_total % NBR_QROWS == 0 and rows_total >= NBR_KROWS + NBR_QROWS
    wr = min(WIN_R, rows_total)
    kinds = [(0, 0), (2 * NBR_QROWS, 2 * NBR_QROWS - WIN_R // 2),
             (rows_total - NBR_QROWS, rows_total - NBR_KROWS)]
    qi = np.arange(NBR_QROWS)[:, None, None, None]
    qc = np.arange(GRID_W)[None, :, None, None]
    kj = np.arange(NBR_KROWS)[None, None, :, None]
    kc = np.arange(GRID_W)[None, None, None, :]
    n_dr, n_dc = rpb.shape[1], rpb.shape[2]
    cs = np.clip(qc - WIN_C // 2, 0, GRID_W - WIN_C)
    col_ok = (kc >= cs) & (kc < cs + WIN_C)
    col_sel = (kc - qc + (WIN_C - 1))[..., None] == np.arange(n_dc)
    col_sel = (col_sel & col_ok[..., None])[0, :, 0].astype(np.float32)
    row_sel, ok_all = [], []
    for r0, k0 in kinds:
        r = r0 + qi
        rs = np.clip(r - wr // 2, 0, rows_total - wr)
        kr = k0 + kj
        row_ok = (kr >= rs) & (kr < rs + wr)
        sel = ((kr - r + (WIN_R - 1))[..., None] == np.arange(n_dr)) & row_ok[..., None]
        row_sel.append(sel[:, 0, :, 0].astype(np.float32))
        ok_all.append(np.broadcast_to(row_ok & col_ok, (NBR_QROWS, GRID_W, NBR_KROWS, GRID_W)))
    row_sel = jnp.asarray(np.stack(row_sel))
    ok = np.stack(ok_all).reshape(3, 1, NBR_QROWS * GRID_W, NBR_KROWS * GRID_W)
    hi = lax.Precision.HIGHEST
    rows_picked = jnp.einsum('hrd,tijr->thijd', rpb, row_sel, precision=hi)
    tab = jnp.einsum('thijd,cnd->thicjn', rows_picked, jnp.asarray(col_sel), precision=hi)
    tab = tab.reshape(3, rpb.shape[0], NBR_QROWS * GRID_W, NBR_KROWS * GRID_W)
    return jnp.where(ok, tab * LOG2_E, NEG).astype(F32)


def _rope_tables(seq):
    pos = jnp.arange(seq, dtype=jnp.int32)
    row = (pos // GRID_W).astype(F32)
    col = (pos % GRID_W).astype(F32)
    freq = 1.0 / (ROPE_THETA ** (jnp.arange(ROPE_FREQS, dtype=F32) / ROPE_FREQS))
    ar = row[:, None] * freq
    ac = col[:, None] * freq
    cos = jnp.concatenate([jnp.cos(ar), jnp.cos(ar), jnp.cos(ac), jnp.cos(ac)], axis=1)
    sin = jnp.concatenate([-jnp.sin(ar), jnp.sin(ar), -jnp.sin(ac), jnp.sin(ac)], axis=1)
    reps = V7X_LANES // HEAD_DIM
    return jnp.tile(cos, (1, reps)), jnp.tile(sin, (1, reps))


def _tile_rows(n, target):
    t = min(n, target)
    assert n % t == 0
    return t


def kernel(x, c, ctx, c_ctx, w_ada, b_ada, norm_g, w_in, qk_g, rpb, conv_w, w_o, ffn_wi, ffn_wo):
    bsz, seq, d = x.shape
    ctx_len = ctx.shape[1]
    depth = w_ada.shape[0]
    ctx_row = bsz

    mod_rows = -(-(bsz + 1) // 8) * 8
    c_all = jnp.zeros((mod_rows, d), F32).at[:bsz].set(c).at[ctx_row].set(c_ctx)
    mods = _ada(c_all, w_ada, b_ada).reshape(depth, mod_rows, N_MOD, d)

    w_in_b = w_in.astype(BF16)
    w_o_b = w_o.astype(BF16)
    wi_b = ffn_wi.astype(BF16)
    wo_b = ffn_wo.astype(BF16)
    qk_gain = jnp.concatenate(
        [jnp.tile(qk_g[:, 0], (1, A_Q_HEADS)), jnp.tile(qk_g[:, 1], (1, A_KV_HEADS))], axis=1
    ).reshape(depth, 1, A_Q_W + A_KV_W)
    gsz = (A_Q_W + A_KV_W) // 2
    head_of = np.arange(gsz) // HEAD_DIM
    gmat = jnp.asarray((head_of[:, None] == head_of[None, :]) / HEAD_DIM, dtype=BF16)
    lane_head = np.arange(V7X_LANES) // HEAD_DIM
    gsum = jnp.asarray(lane_head[:, None] == lane_head[None, :], dtype=BF16)
    rope_tabs = _rope_tables(seq)

    tm_ffn = _tile_rows(seq, 512)
    tm_proj = _tile_rows(seq, 1024)
    tm_ctx = _tile_rows(ctx_len, 256)
    tq_a = _tile_rows(seq, 512)
    tk_a = _tile_rows(seq, 512)
    unroll_a = min(GQA_UNROLL, seq // tk_a)
    n_rblocks = seq // (GRID_W * NBR_QROWS)
    nbr_blocks = NBR_BLOCKS_PER_STEP if n_rblocks % NBR_BLOCKS_PER_STEP == 0 else 1

    h, hc = x, ctx
    for layer in range(depth):
        last = layer == depth - 1
        lat = dict(layer=layer, mod_row=None, n_sub=ROW_SUBTILES)
        cx = dict(layer=layer, mod_row=ctx_row, n_sub=1)
        h = _ffn(h, mods, norm_g, wi_b, wo_b, which=0, tm=tm_ffn, **lat)
        hc = _ffn(hc, mods, norm_g, wi_b, wo_b, which=0, tm=tm_ctx, **cx)
        qa, qb, ka, va, kb, vb, oc = _proj(h, mods, norm_g, w_in_b, qk_gain, conv_w, gmat, rope_tabs,
                                           tm=tm_proj, layer=layer, mod_row=None, n_sub=PROJ_SUBTILES)
        cqa, cqb, cka, cva, ckb, cvb, coc = _proj(hc, mods, norm_g, w_in_b, qk_gain, conv_w, gmat, None,
                                                  tm=tm_ctx, layer=layer, mod_row=ctx_row)
        if not last:
            coa = _gqa(cqa, cka, cva, None, hq=A_Q_HEADS, hkv=A_KV_HEADS, tq=tm_ctx, tk=ctx_len)
            cob = _gqa(cqb, ckb, cvb, None, hq=B_HEADS, hkv=B_HEADS, tq=tm_ctx, tk=ctx_len)
            hc = _ffn(hc, mods, norm_g, wi_b, wo_b, which=1, tm=tm_ctx, mix=(coa, cob, coc, w_o_b), **cx)
        ob = _nbr(qb, kb, vb, ckb, cvb, _nbr_bias_tables(rpb[layer], seq // GRID_W), n_blocks=nbr_blocks)
        h = _gqa_bounded(qa, ka, va, cka, cva, gsum, h, ob, oc, mods, norm_g, w_o_b,
                         layer=layer, hq=A_Q_HEADS, tq=tq_a, tk=tk_a, unroll=unroll_a)
        h = _ffn(h, mods, norm_g, wi_b, wo_b, which=1, tm=tm_ffn, **lat)
    return h
```

```python
import functools

import numpy as np
import jax
import jax.numpy as jnp
from jax import lax
from jax.experimental import pallas as pl
from jax.experimental.pallas import tpu as pltpu

F32 = jnp.float32
BF16 = jnp.bfloat16

HEAD_DIM = 64
GRID_W = 64
A_Q_HEADS = 6
A_KV_HEADS = 2
B_HEADS = 6
C_WIDTH = 256
A_Q_W = A_Q_HEADS * HEAD_DIM
A_KV_W = A_KV_HEADS * HEAD_DIM
B_W = B_HEADS * HEAD_DIM
WIN_R = 8
WIN_C = 16
ROPE_FREQS = HEAD_DIM // 4
ROPE_THETA = 10000.0
N_MOD = 9
EPS = 1e-6
NEG = -1e30
LOG2_E = 1.4426950408889634

V7X_LANES = 128
V7X_SCOPED_VMEM_BYTES = 60000 * 1024

NBR_QROWS = 4
NBR_KROWS = NBR_QROWS + WIN_R

GQA_UNROLL = 8
GQA_FALLBACK_UNROLL = 2
ROW_SUBTILES = 4
PROJ_SUBTILES = 8
NBR_BLOCKS_PER_STEP = 4


def _vmem_limit(estimate_bytes):
    return int(min(max(estimate_bytes, 16 * 1024 * 1024), V7X_SCOPED_VMEM_BYTES))


def _params(n_axes, vmem_bytes):
    return pltpu.CompilerParams(
        dimension_semantics=("arbitrary",) * n_axes,
        vmem_limit_bytes=_vmem_limit(vmem_bytes),
    )


def _rms(x, g):
    ms = jnp.mean(x * x, axis=-1, keepdims=True)
    return x * lax.rsqrt(ms + EPS) * g


def _resident(block_shape, index_map):
    return pl.BlockSpec(block_shape, index_map, pipeline_mode=pl.Buffered(1))


def _ada_kernel(c_ref, w_ref, b_ref, o_ref):
    c = c_ref[...]
    sc = c * jax.nn.sigmoid(c)
    o_ref[...] = jnp.dot(sc, w_ref[...], preferred_element_type=F32,
                         precision=lax.Precision.HIGHEST) + b_ref[...]


def _ada(c_all, w_ada, b_ada):
    depth, d, n = w_ada.shape
    rows = c_all.shape[0]
    tn = d
    return pl.pallas_call(
        _ada_kernel,
        grid=(depth, n // tn),
        in_specs=[
            pl.BlockSpec((rows, d), lambda l, j: (0, 0)),
            pl.BlockSpec((None, d, tn), lambda l, j: (l, 0, j)),
            pl.BlockSpec((None, 1, tn), lambda l, j: (l, 0, j)),
        ],
        out_specs=pl.BlockSpec((None, rows, tn), lambda l, j: (l, 0, j)),
        out_shape=jax.ShapeDtypeStruct((depth, rows, n), F32),
        compiler_params=_params(2, 4 * d * tn * 4),
        name="ada_mod",
    )(c_all, w_ada, b_ada.reshape(depth, 1, n))


def _ffn_kernel(*refs, i0, gi, ffn_dim, n_sub, mix):
    if mix:
        h_ref, oa_ref, ob_ref, oc_ref, mod_ref, g_ref, wmix_ref, wi_ref, wo_ref, o_ref = refs
    else:
        h_ref, mod_ref, g_ref, wi_ref, wo_ref, o_ref = refs
    shift = mod_ref[i0:i0 + 1, :]
    scale = mod_ref[i0 + 1:i0 + 2, :]
    gate = mod_ref[i0 + 2:i0 + 3, :]
    sub = h_ref.shape[0] // n_sub
    rows_of = lambda t: slice(t * sub, (t + 1) * sub)
    hs, hids, acts = {}, {}, {}

    def up_proj(t):
        h = h_ref[rows_of(t), :]
        if mix:
            o = jnp.concatenate([oa_ref[rows_of(t), :], ob_ref[rows_of(t), :], oc_ref[rows_of(t), :]], axis=1)
            ymix = jnp.dot(o, wmix_ref[...], preferred_element_type=F32)
            h = h + mod_ref[5:6, :] * _rms(ymix, g_ref[3:4, :])
        hs[t] = h
        u = _rms(h, g_ref[gi:gi + 1, :]) * (1.0 + scale) + shift
        hids[t] = jnp.dot(u.astype(BF16), wi_ref[...], preferred_element_type=F32)

    def activate(t):
        hid = hids.pop(t)
        gt = hid[:, :ffn_dim]
        up = hid[:, ffn_dim:]
        acts[t] = (gt * jax.nn.sigmoid(gt) * up).astype(BF16)

    def down_proj(t):
        y = jnp.dot(acts.pop(t), wo_ref[...], preferred_element_type=F32)
        o_ref[rows_of(t), :] = hs.pop(t) + 0.5 * gate * _rms(y, g_ref[gi + 1:gi + 2, :])

    for t in range(n_sub):
        up_proj(t)
        activate(t)
        down_proj(t)


def _ffn(h, mods, norm_g, wi, wo, *, layer, which, mod_row, tm, n_sub=1, mix=None):
    bsz, s, d = h.shape
    ffn_dim = wo.shape[2]
    i0 = 6 * which
    gi = 4 * which
    if mod_row is None:
        mod_map = lambda b, i: (layer, b, 0, 0)
    else:
        mod_map = lambda b, i: (layer, mod_row, 0, 0)
    tile = lambda w: pl.BlockSpec((None, tm, w), lambda b, i: (b, i, 0))
    vmem = ((wi.shape[2] * wi.shape[3] + wo.shape[2] * wo.shape[3]) * 2 + tm * d * 4 * 6
            + (tm // n_sub) * ffn_dim * 24)
    in_specs = [tile(d)]
    args = [h]
    if mix is not None:
        in_specs += [tile(o.shape[2]) for o in mix[:3]]
        args += list(mix[:3])
    in_specs += [pl.BlockSpec((None, None, N_MOD, d), mod_map),
                 _resident((None, norm_g.shape[1], d), lambda b, i: (layer, 0, 0))]
    args += [mods, norm_g]
    if mix is not None:
        w_mix = mix[3]
        in_specs.append(_resident((None, w_mix.shape[1], d), lambda b, i: (layer, 0, 0)))
        args.append(w_mix)
        vmem += w_mix.shape[1] * d * 2 + tm * d * 2 * 2
    in_specs += [_resident((None, None, d, 2 * ffn_dim), lambda b, i: (layer, which, 0, 0)),
                 _resident((None, None, ffn_dim, d), lambda b, i: (layer, which, 0, 0))]
    args += [wi, wo]
    return pl.pallas_call(
        functools.partial(_ffn_kernel, i0=i0, gi=gi, ffn_dim=ffn_dim, n_sub=n_sub, mix=mix is not None),
        grid=(bsz, s // tm),
        in_specs=in_specs,
        out_specs=tile(d),
        out_shape=jax.ShapeDtypeStruct(h.shape, F32),
        compiler_params=_params(2, vmem),
        name="mix_out_ffn" if mix is not None else "ffn",
    )(*args)


def _swap_rope_partners(x):
    lane = lax.broadcasted_iota(jnp.int32, x.shape, 1)
    first = (lane % (2 * ROPE_FREQS)) < ROPE_FREQS
    return jnp.where(first,
                     pltpu.roll(x, V7X_LANES - ROPE_FREQS, axis=1),
                     pltpu.roll(x, ROPE_FREQS, axis=1))


def _proj_kernel(*refs, rope, tm, n_tiles, n_sub):
    if rope:
        (h_ref, hp_ref, hn_ref, mod_ref, g_ref, w_ref, qkg_ref, cw_ref, gm_ref, cos_ref, sin_ref,
         qa_ref, qb_ref, ka_ref, va_ref, kb_ref, vb_ref, oc_ref, z_sc, cb_sc) = refs
    else:
        (h_ref, hp_ref, hn_ref, mod_ref, g_ref, w_ref, qkg_ref, cw_ref, gm_ref,
         qa_ref, qb_ref, ka_ref, va_ref, kb_ref, vb_ref, oc_ref, z_sc, cb_sc) = refs
    i = pl.program_id(1)
    shift = mod_ref[3:4, :]
    scale = mod_ref[4:5, :]
    g2 = g_ref[2:3, :]

    def pre(x):
        return (_rms(x, g2) * (1.0 + scale) + shift).astype(BF16)

    o_qa, o_qb = 0, A_Q_W
    o_ka = o_qb + B_W
    o_va = o_ka + A_KV_W
    o_kb = o_va + A_KV_W
    o_vb = o_kb + B_W
    o_cx = o_vb + B_W
    o_cb = o_cx + C_WIDTH
    o_cc = o_cb + C_WIDTH
    q_scale = HEAD_DIM ** -0.5 * LOG2_E
    half = (A_Q_W + A_KV_W) // 2
    sub = tm // n_sub

    for t in range(n_sub):
        rows = slice(t * sub, (t + 1) * sub)
        p = jnp.dot(pre(h_ref[rows, :]), w_ref[...], preferred_element_type=F32)
        xq = jnp.concatenate([p[:, o_qa:o_qa + A_Q_W], p[:, o_ka:o_ka + A_KV_W]], axis=1)
        sq = (xq * xq).astype(BF16)
        ms = jnp.concatenate(
            [jnp.dot(sq[:, :half], gm_ref[...], preferred_element_type=F32),
             jnp.dot(sq[:, half:], gm_ref[...], preferred_element_type=F32)], axis=1)
        xn = xq * lax.rsqrt(ms + EPS) * qkg_ref[...]
        if rope:
            cos = cos_ref[rows, :]
            sin = sin_ref[rows, :]
            cols = []
            for j in range((A_Q_W + A_KV_W) // V7X_LANES):
                xc = xn[:, j * V7X_LANES:(j + 1) * V7X_LANES]
                cols.append(xc * cos + _swap_rope_partners(xc) * sin)
            xn = jnp.concatenate(cols, axis=1)
        qa_ref[rows, :] = (xn[:, :A_Q_W] * q_scale).astype(BF16)
        ka_ref[rows, :] = xn[:, A_Q_W:].astype(BF16)
        qb_ref[rows, :] = (p[:, o_qb:o_qb + B_W] * q_scale).astype(BF16)
        va_ref[rows, :] = p[:, o_va:o_va + A_KV_W].astype(BF16)
        kb_ref[rows, :] = p[:, o_kb:o_kb + B_W].astype(BF16)
        vb_ref[rows, :] = p[:, o_vb:o_vb + B_W].astype(BF16)
        z_sc[rows, :] = p[:, o_cc:o_cc + C_WIDTH] * p[:, o_cx:o_cx + C_WIDTH]
        cb_sc[rows, :] = p[:, o_cb:o_cb + C_WIDTH]

    z = z_sc[...]

    def halo(ref):
        ub = pre(ref[...])
        return (jnp.dot(ub, w_ref[:, o_cc:o_cc + C_WIDTH], preferred_element_type=F32)
                * jnp.dot(ub, w_ref[:, o_cx:o_cx + C_WIDTH], preferred_element_type=F32))

    halo_rows = hp_ref.shape[0]
    z_before = jnp.where(i > 0, halo(hp_ref)[halo_rows - 1:halo_rows, :], 0.0)
    z_after = jnp.where(i < n_tiles - 1, halo(hn_ref)[0:1, :], 0.0)
    row = lax.broadcasted_iota(jnp.int32, z.shape, 0)
    z_m1 = jnp.where(row == 0, z_before, pltpu.roll(z, 1, axis=0))
    z_p1 = jnp.where(row == tm - 1, z_after, pltpu.roll(z, tm - 1, axis=0))
    y = cw_ref[0:1, :] * z_m1 + cw_ref[1:2, :] * z + cw_ref[2:3, :] * z_p1
    oc_ref[...] = (cb_sc[...] * y).astype(BF16)


def _proj(h, mods, norm_g, w_in, qk_gain, conv_w, gmat, rope_tabs, *, layer, mod_row, tm, n_sub=1):
    bsz, s, d = h.shape
    n_tiles = s // tm
    halo_rows = 8
    hb = tm // halo_rows
    n_hblk = s // halo_rows
    rope = rope_tabs is not None
    if mod_row is None:
        mod_map = lambda b, i: (layer, b, 0, 0)
    else:
        mod_map = lambda b, i: (layer, mod_row, 0, 0)
    pw = w_in.shape[2]
    in_specs = [
        pl.BlockSpec((None, tm, d), lambda b, i: (b, i, 0)),
        pl.BlockSpec((None, halo_rows, d), lambda b, i: (b, jnp.maximum(i * hb - 1, 0), 0)),
        pl.BlockSpec((None, halo_rows, d), lambda b, i: (b, jnp.minimum((i + 1) * hb, n_hblk - 1), 0)),
        pl.BlockSpec((None, None, N_MOD, d), mod_map),
        _resident((None, norm_g.shape[1], d), lambda b, i: (layer, 0, 0)),
        _resident((None, d, pw), lambda b, i: (layer, 0, 0)),
        _resident((None, 1, A_Q_W + A_KV_W), lambda b, i: (layer, 0, 0)),
        _resident((None, conv_w.shape[1], C_WIDTH), lambda b, i: (layer, 0, 0)),
        _resident(gmat.shape, lambda b, i: (0, 0)),
    ]
    args = [h, h, h, mods, norm_g, w_in, qk_gain, conv_w, gmat]
    if rope:
        in_specs += [pl.BlockSpec((tm, V7X_LANES), lambda b, i: (i, 0))] * 2
        args += list(rope_tabs)
    widths = (A_Q_W, B_W, A_KV_W, A_KV_W, B_W, B_W, C_WIDTH)
    out_specs = [pl.BlockSpec((None, tm, w), lambda b, i: (b, i, 0)) for w in widths]
    out_shape = [jax.ShapeDtypeStruct((bsz, s, w), BF16) for w in widths]
    vmem = d * pw * 2 + tm * d * 4 * 4 + tm * pw * 12
    return pl.pallas_call(
        functools.partial(_proj_kernel, rope=rope, tm=tm, n_tiles=n_tiles, n_sub=n_sub),
        grid=(bsz, n_tiles),
        in_specs=in_specs,
        out_specs=out_specs,
        out_shape=out_shape,
        scratch_shapes=[pltpu.VMEM((tm, C_WIDTH), F32), pltpu.VMEM((tm, C_WIDTH), F32)],
        compiler_params=_params(2, vmem),
        name="mix_proj",
    )(*args)


_NT_DIMS = (((1,), (1,)), ((), ()))

SCORE_BOUND_MAX = 40.0
SCORE_BOUND_PIVOT = 16.0


def _load_q_rows(q_ref, qs_sc, heads, grp, tq):
    lane = lax.broadcasted_iota(jnp.int32, (tq, V7X_LANES), 1)
    low_half = lane < HEAD_DIM
    for j, h in enumerate(heads):
        xc = q_ref[:, (h // 2) * V7X_LANES:(h // 2 + 1) * V7X_LANES].astype(F32)
        dst_low = (h // grp) % 2 == 0
        if (h % 2 == 0) != dst_low:
            xc = pltpu.roll(xc, HEAD_DIM, axis=1)
        keep = low_half if dst_low else jnp.logical_not(low_half)
        qs_sc[j * tq:(j + 1) * tq, :] = jnp.where(keep, xc, 0.0).astype(BF16)


def _repack_heads(on_sc, hq, grp, tq):
    lane = lax.broadcasted_iota(jnp.int32, (tq, V7X_LANES), 1)
    low_half = lane < HEAD_DIM
    cols = []
    for oc in range(hq // 2):
        pieces = []
        for e in range(2):
            h = 2 * oc + e
            piece = on_sc[h * tq:(h + 1) * tq, :]
            src_low = (h // grp) % 2 == 0
            if src_low != (e == 0):
                piece = pltpu.roll(piece, HEAD_DIM, axis=1)
            pieces.append(piece)
        cols.append(jnp.where(low_half, pieces[0], pieces[1]).astype(BF16))
    return cols


def _online_softmax(qs_sc, m_sc, l_sc, acc_sc, rows, chunks, extra, *, tk, n_chunks, unroll):
    m_sc[0:rows, :] = jnp.full((rows, V7X_LANES), NEG, F32)
    l_sc[0:rows, :] = jnp.zeros((rows, V7X_LANES), F32)
    acc_sc[0:rows, :] = jnp.zeros((rows, V7X_LANES), F32)

    def load_state():
        return m_sc[0:rows, :], l_sc[0:rows, :], acc_sc[0:rows, :]

    def store_state(state):
        m_sc[0:rows, :], l_sc[0:rows, :], acc_sc[0:rows, :] = state

    def step(state, kc, vc):
        m_prev, l_prev, acc_prev = state
        s = lax.dot_general(qs_sc[0:rows, :], kc, _NT_DIMS, preferred_element_type=F32)
        m_next = jnp.maximum(m_prev, jnp.max(s, axis=1, keepdims=True))
        alpha = jnp.exp2(m_prev - m_next)
        p = jnp.exp2(s - jnp.concatenate([m_next] * (kc.shape[0] // V7X_LANES), axis=1))
        l_next = alpha * l_prev + jnp.sum(p, axis=1, keepdims=True)
        acc_next = alpha * acc_prev + jnp.dot(p.astype(BF16), vc, preferred_element_type=F32)
        return m_next, l_next, acc_next

    def body(c, carry):
        state = load_state()
        for u in range(unroll):
            state = step(state, *chunks(pl.multiple_of((c * unroll + u) * tk, tk)))
        store_state(state)
        return carry

    lax.fori_loop(0, n_chunks // unroll, body, 0)
    if extra is not None:
        store_state(step(load_state(), *extra))
    return acc_sc[0:rows, :] / l_sc[0:rows, :]


def _bounded_softmax(qs_sc, b_sc, l_sc, acc_sc, rows, chunks, extra, *, tk, n_chunks, unroll):
    l_sc[0:rows, :] = jnp.zeros((rows, V7X_LANES), F32)
    acc_sc[0:rows, :] = jnp.zeros((rows, V7X_LANES), F32)

    def step(state, kc, vc):
        l_prev, acc_prev = state
        n_cols = kc.shape[0] // V7X_LANES
        s = lax.dot_general(qs_sc[0:rows, :], kc, _NT_DIMS, preferred_element_type=F32)
        p = jnp.exp2(s - jnp.concatenate([b_sc[0:rows, :]] * n_cols, axis=1))
        l_next = l_prev
        for j in range(n_cols):
            l_next = l_next + p[:, j * V7X_LANES:(j + 1) * V7X_LANES]
        acc_next = acc_prev + jnp.dot(p.astype(BF16), vc, preferred_element_type=F32)
        return l_next, acc_next

    def body(c, carry):
        state = (l_sc[0:rows, :], acc_sc[0:rows, :])
        for u in range(unroll):
            state = step(state, *chunks(pl.multiple_of((c * unroll + u) * tk, tk)))
        l_sc[0:rows, :], acc_sc[0:rows, :] = state
        return carry

    n_trips = n_chunks // unroll
    lax.fori_loop(0, n_trips - 1, body, 0)
    state = (l_sc[0:rows, :], acc_sc[0:rows, :])
    for u in range(unroll):
        state = step(state, *chunks(((n_trips - 1) * unroll + u) * tk))
    if extra is not None:
        state = step(state, *extra)
    l_lanes, acc = state
    return acc / jnp.sum(l_lanes, axis=1, keepdims=True)


def _gqa_kernel(*refs, hq, hkv, tq, tk, n_chunks, unroll, extra_len):
    if extra_len:
        q_ref, k_ref, v_ref, kx_ref, vx_ref, o_ref, qs_sc, m_sc, l_sc, acc_sc, on_sc = refs
    else:
        q_ref, k_ref, v_ref, o_ref, qs_sc, m_sc, l_sc, acc_sc, on_sc = refs
    grp = hq // hkv
    for pc in range(hkv // 2):
        heads = [h for h in range(hq) if (h // grp) // 2 == pc]
        rows = len(heads) * tq
        col = slice(pc * V7X_LANES, (pc + 1) * V7X_LANES)
        _load_q_rows(q_ref, qs_sc, heads, grp, tq)
        chunks = lambda start: (k_ref[pl.ds(start, tk), col], v_ref[pl.ds(start, tk), col])
        extra = (kx_ref[:, col], vx_ref[:, col]) if extra_len else None
        on = _online_softmax(qs_sc, m_sc, l_sc, acc_sc, rows, chunks, extra,
                             tk=tk, n_chunks=n_chunks, unroll=unroll)
        for j, h in enumerate(heads):
            on_sc[h * tq:(h + 1) * tq, :] = on[j * tq:(j + 1) * tq, :]
    o_ref[...] = jnp.concatenate(_repack_heads(on_sc, hq, grp, tq), axis=1)


def _gqa_bounded_kernel(q_ref, k_ref, v_ref, kx_ref, vx_ref, gs_ref, h_ref, ob_ref, oc_ref, mod_ref, g_ref,
                        wmix_ref, o_ref, qs_sc, m_sc, l_sc, acc_sc, on_sc, b_sc, kmax_sc,
                        *, hq, tq, tk, n_chunks, unroll, n_q):
    hkv = 2
    grp = hq // hkv
    rows = hq * tq
    step = pl.program_id(1)

    def max_sq_norm(kc, mx):
        kf = kc.astype(F32)
        ss = lax.dot_general(gs_ref[...], (kf * kf).astype(BF16), _NT_DIMS, preferred_element_type=F32)
        return jnp.maximum(mx, jnp.max(ss, axis=1, keepdims=True))

    @pl.when(step == 0)
    def _():
        def kbody(c, mx):
            return max_sq_norm(k_ref[pl.ds(pl.multiple_of(c * tk, tk), tk), :], mx)
        mx = lax.fori_loop(0, n_chunks, kbody, jnp.zeros((V7X_LANES, 1), F32))
        kmax_sc[...] = jnp.broadcast_to(max_sq_norm(kx_ref[...], mx), kmax_sc.shape).astype(BF16)
        on_sc[...] = jnp.zeros(on_sc.shape, F32)

    o = jnp.concatenate(_repack_heads(on_sc, hq, grp, tq) + [ob_ref[...], oc_ref[...]], axis=1)
    y = jnp.dot(o, wmix_ref[...], preferred_element_type=F32)
    o_ref[...] = h_ref[...] + mod_ref[5:6, :] * _rms(y, g_ref[3:4, :])

    _load_q_rows(q_ref, qs_sc, list(range(hq)), grp, tq)
    qf = qs_sc[...].astype(F32)
    qk_sq = jnp.dot((qf * qf).astype(BF16), kmax_sc[...], preferred_element_type=F32)
    bound = (qk_sq * (0.5 / SCORE_BOUND_PIVOT) + 0.5 * SCORE_BOUND_PIVOT) * (1.0 + 2.0 ** -5)
    b_sc[...] = bound
    bounded_ok = jnp.max(bound) <= SCORE_BOUND_MAX
    has_tile = step < n_q

    chunks = lambda start: (k_ref[pl.ds(start, tk), :], v_ref[pl.ds(start, tk), :])
    extra = (kx_ref[...], vx_ref[...])

    @pl.when(jnp.logical_and(has_tile, bounded_ok))
    def _():
        on_sc[...] = _bounded_softmax(qs_sc, b_sc, l_sc, acc_sc, rows, chunks, extra,
                                      tk=tk, n_chunks=n_chunks, unroll=unroll)

    @pl.when(jnp.logical_and(has_tile, jnp.logical_not(bounded_ok)))
    def _():
        on_sc[...] = _online_softmax(qs_sc, m_sc, l_sc, acc_sc, rows, chunks, extra,
                                     tk=tk, n_chunks=n_chunks, unroll=min(unroll, GQA_FALLBACK_UNROLL))


def _gqa(q, k, v, extra, *, hq, hkv, tq, tk, unroll=1):
    bsz, s, qw = q.shape
    t = k.shape[1]
    kw = k.shape[2]
    n_chunks = t // tk
    assert n_chunks % unroll == 0
    grp = hq // hkv
    max_heads = max(sum(1 for h in range(hq) if (h // grp) // 2 == pc) for pc in range(hkv // 2))
    rows = max_heads * tq
    in_specs = [
        pl.BlockSpec((None, tq, qw), lambda b, i: (b, i, 0)),
        pl.BlockSpec((None, t, kw), lambda b, i: (b, 0, 0)),
        pl.BlockSpec((None, t, kw), lambda b, i: (b, 0, 0)),
    ]
    args = [q, k, v]
    extra_len = 0
    if extra is not None:
        extra_len = extra[0].shape[1]
        in_specs += [pl.BlockSpec((None, extra_len, kw), lambda b, i: (b, 0, 0))] * 2
        args += list(extra)
    vmem = 4 * t * kw * 2 + rows * max(tk, extra_len) * 16 + rows * V7X_LANES * 24 + 8 * tq * qw * 2
    return pl.pallas_call(
        functools.partial(_gqa_kernel, hq=hq, hkv=hkv, tq=tq, tk=tk, n_chunks=n_chunks, unroll=unroll,
                          extra_len=extra_len),
        grid=(bsz, s // tq),
        in_specs=in_specs,
        out_specs=pl.BlockSpec((None, tq, qw), lambda b, i: (b, i, 0)),
        out_shape=jax.ShapeDtypeStruct(q.shape, BF16),
        scratch_shapes=[
            pltpu.VMEM((rows, V7X_LANES), BF16),
            pltpu.VMEM((rows, V7X_LANES), F32),
            pltpu.VMEM((rows, V7X_LANES), F32),
            pltpu.VMEM((rows, V7X_LANES), F32),
            pltpu.VMEM((hq * tq, V7X_LANES), F32),
        ],
        compiler_params=_params(2, vmem),
        name="gqa_attn",
    )(*args)


def _gqa_bounded(q, k, v, kx, vx, gsum, h, ob, oc, mods, norm_g, w_o, *, layer, hq, tq, tk, unroll):
    bsz, s, qw = q.shape
    d = h.shape[2]
    t, kw = k.shape[1], k.shape[2]
    lx = kx.shape[1]
    n_q = s // tq
    assert kw == V7X_LANES and t % (tk * unroll) == 0
    rows = hq * tq
    stat = pltpu.VMEM((rows, V7X_LANES), F32)
    keys = lambda n: pl.BlockSpec((None, n, kw), lambda b, i: (b, 0, 0))
    prev = lambda w: pl.BlockSpec((None, tq, w), lambda b, i: (b, jnp.maximum(i - 1, 0), 0))
    vmem = (4 * (t + lx) * kw * 2 + rows * max(tk, lx) * 8 * min(unroll, 4) + rows * V7X_LANES * 32
            + 8 * tq * qw * 2 + w_o.shape[1] * d * 2 + 6 * tq * d * 4)
    return pl.pallas_call(
        functools.partial(_gqa_bounded_kernel, hq=hq, tq=tq, tk=tk, n_chunks=t // tk, unroll=unroll, n_q=n_q),
        grid=(bsz, n_q + 1),
        in_specs=[
            pl.BlockSpec((None, tq, qw), lambda b, i: (b, jnp.minimum(i, n_q - 1), 0)),
            keys(t), keys(t), keys(lx), keys(lx),
            _resident(gsum.shape, lambda b, i: (0, 0)),
            prev(d), prev(ob.shape[2]), prev(oc.shape[2]),
            pl.BlockSpec((None, None, N_MOD, d), lambda b, i: (layer, b, 0, 0)),
            _resident((None, norm_g.shape[1], d), lambda b, i: (layer, 0, 0)),
            _resident((None, w_o.shape[1], d), lambda b, i: (layer, 0, 0)),
        ],
        out_specs=prev(d),
        out_shape=jax.ShapeDtypeStruct(h.shape, F32),
        scratch_shapes=[
            pltpu.VMEM((rows, V7X_LANES), BF16), stat, stat, stat, stat, stat,
            pltpu.VMEM((V7X_LANES, V7X_LANES), BF16),
        ],
        compiler_params=_params(2, vmem),
        name="gqa_attn_mix_out",
    )(q, k, v, kx, vx, gsum, h, ob, oc, mods, norm_g, w_o)


def _nbr_kernel(q_ref, k_ref, v_ref, kx_ref, vx_ref, *rest, n_blocks, rows_total):
    bias_refs, o_ref = rest[:n_blocks], rest[n_blocks]
    tq = NBR_QROWS * GRID_W
    n_win = NBR_KROWS * GRID_W
    kx = kx_ref[...]
    vx = vx_ref[...]
    lane = lax.broadcasted_iota(jnp.int32, (tq, V7X_LANES), 1)
    low_half = lane < HEAD_DIM
    starts, scores, probs = {}, {}, {}

    def qk(j):
        rb = pl.program_id(2) * n_blocks + j
        krow0 = jnp.clip(rb * NBR_QROWS - WIN_R // 2, 0, rows_total - NBR_KROWS)
        starts[j] = pl.multiple_of(krow0 * GRID_W, NBR_QROWS * GRID_W)
        kw = k_ref[pl.ds(starts[j], n_win), :]
        q = q_ref[j * tq:(j + 1) * tq, :]
        zero = jnp.zeros_like(q)
        qs = jnp.concatenate([jnp.where(low_half, q, zero), jnp.where(low_half, zero, q)], axis=0)
        bias = bias_refs[j][...].reshape(2 * tq, n_win)
        scores[j] = (lax.dot_general(qs, kw, _NT_DIMS, preferred_element_type=F32) + bias,
                     lax.dot_general(qs, kx, _NT_DIMS, preferred_element_type=F32))

    def soft(j):
        s_win, s_ctx = scores.pop(j)
        m = jnp.maximum(jnp.max(s_win, axis=1, keepdims=True), jnp.max(s_ctx, axis=1, keepdims=True))
        p_win = jnp.exp2(s_win - m)
        p_ctx = jnp.exp2(s_ctx - m)
        l = jnp.sum(p_win, axis=1, keepdims=True) + jnp.sum(p_ctx, axis=1, keepdims=True)
        probs[j] = (p_win.astype(BF16), p_ctx.astype(BF16), l)

    def pv(j):
        p_win, p_ctx, l = probs.pop(j)
        vw = v_ref[pl.ds(starts[j], n_win), :]
        o = (jnp.dot(p_win, vw, preferred_element_type=F32)
             + jnp.dot(p_ctx, vx, preferred_element_type=F32)) / l
        o_ref[j * tq:(j + 1) * tq, :] = jnp.where(low_half, o[:tq], o[tq:]).astype(BF16)

    for stage in (qk, soft, pv):
        for j in range(n_blocks):
            stage(j)


def _nbr(q, k, v, kx, vx, bias, *, n_blocks=1):
    bsz, s, w = q.shape
    rows_total = s // GRID_W
    n_rblocks = rows_total // NBR_QROWS
    assert n_rblocks % n_blocks == 0
    tq = NBR_QROWS * GRID_W
    n_win = NBR_KROWS * GRID_W
    lx = kx.shape[1]
    n_pairs = w // V7X_LANES

    def bias_spec(j):
        def bias_map(b, pr, st):
            rb = st * n_blocks + j
            kind = jnp.where(rb == 0, 0, jnp.where(rb == n_rblocks - 1, 2, 1))
            return (kind, pr, 0, 0)
        return pl.BlockSpec((None, 2, tq, n_win), bias_map)

    qo_spec = pl.BlockSpec((None, n_blocks * tq, V7X_LANES), lambda b, pr, st: (b, st, pr))
    vmem = 4 * s * V7X_LANES * 2 + n_blocks * (2 * 2 * tq * n_win * 4 + tq * (n_win + lx) * 24)
    return pl.pallas_call(
        functools.partial(_nbr_kernel, n_blocks=n_blocks, rows_total=rows_total),
        grid=(bsz, n_pairs, n_rblocks // n_blocks),
        in_specs=[
            qo_spec,
            pl.BlockSpec((None, s, V7X_LANES), lambda b, pr, st: (b, 0, pr)),
            pl.BlockSpec((None, s, V7X_LANES), lambda b, pr, st: (b, 0, pr)),
            pl.BlockSpec((None, lx, V7X_LANES), lambda b, pr, st: (b, 0, pr)),
            pl.BlockSpec((None, lx, V7X_LANES), lambda b, pr, st: (b, 0, pr)),
        ] + [bias_spec(j) for j in range(n_blocks)],
        out_specs=qo_spec,
        out_shape=jax.ShapeDtypeStruct(q.shape, BF16),
        compiler_params=_params(3, vmem),
        name="nbr_attn",
    )(q, k, v, kx, vx, *([bias] * n_blocks))


def _nbr_bias_tables(rpb, rows_total):
    assert rows_total % NBR_QROWS == 0 and rows_total >= NBR_KROWS + NBR_QROWS
    wr = min(WIN_R, rows_total)
    kinds = [(0, 0), (2 * NBR_QROWS, 2 * NBR_QROWS - WIN_R // 2),
             (rows_total - NBR_QROWS, rows_total - NBR_KROWS)]
    qi = np.arange(NBR_QROWS)[:, None, None, None]
    qc = np.arange(GRID_W)[None, :, None, None]
    kj = np.arange(NBR_KROWS)[None, None, :, None]
    kc = np.arange(GRID_W)[None, None, None, :]
    n_dr, n_dc = rpb.shape[1], rpb.shape[2]
    cs = np.clip(qc - WIN_C // 2, 0, GRID_W - WIN_C)
    col_ok = (kc >= cs) & (kc < cs + WIN_C)
    col_sel = (kc - qc + (WIN_C - 1))[..., None] == np.arange(n_dc)
    col_sel = (col_sel & col_ok[..., None])[0, :, 0].astype(np.float32)
    row_sel, ok_all = [], []
    for r0, k0 in kinds:
        r = r0 + qi
        rs = np.clip(r - wr // 2, 0, rows_total - wr)
        kr = k0 + kj
        row_ok = (kr >= rs) & (kr < rs + wr)
        sel = ((kr - r + (WIN_R - 1))[..., None] == np.arange(n_dr)) & row_ok[..., None]
        row_sel.append(sel[:, 0, :, 0].astype(np.float32))
        ok_all.append(np.broadcast_to(row_ok & col_ok, (NBR_QROWS, GRID_W, NBR_KROWS, GRID_W)))
    row_sel = jnp.asarray(np.stack(row_sel))
    ok = np.stack(ok_all).reshape(3, 1, NBR_QROWS * GRID_W, NBR_KROWS * GRID_W)
    hi = lax.Precision.HIGHEST
    rows_picked = jnp.einsum('hrd,tijr->thijd', rpb, row_sel, precision=hi)
    tab = jnp.einsum('thijd,cnd->thicjn', rows_picked, jnp.asarray(col_sel), precision=hi)
    tab = tab.reshape(3, rpb.shape[0], NBR_QROWS * GRID_W, NBR_KROWS * GRID_W)
    return jnp.where(ok, tab * LOG2_E, NEG).astype(F32)


def _rope_tables(seq):
    pos = jnp.arange(seq, dtype=jnp.int32)
    row = (pos // GRID_W).astype(F32)
    col = (pos % GRID_W).astype(F32)
    freq = 1.0 / (ROPE_THETA ** (jnp.arange(ROPE_FREQS, dtype=F32) / ROPE_FREQS))
    ar = row[:, None] * freq
    ac = col[:, None] * freq
    cos = jnp.concatenate([jnp.cos(ar), jnp.cos(ar), jnp.cos(ac), jnp.cos(ac)], axis=1)
    sin = jnp.concatenate([-jnp.sin(ar), jnp.sin(ar), -jnp.sin(ac), jnp.sin(ac)], axis=1)
    reps = V7X_LANES // HEAD_DIM
    return jnp.tile(cos, (1, reps)), jnp.tile(sin, (1, reps))


def _tile_rows(n, target):
    t = min(n, target)
    assert n % t == 0
    return t


def kernel(x, c, ctx, c_ctx, w_ada, b_ada, norm_g, w_in, qk_g, rpb, conv_w, w_o, ffn_wi, ffn_wo):
    bsz, seq, d = x.shape
    ctx_len = ctx.shape[1]
    depth = w_ada.shape[0]
    ctx_row = bsz

    mod_rows = -(-(bsz + 1) // 8) * 8
    c_all = jnp.zeros((mod_rows, d), F32).at[:bsz].set(c).at[ctx_row].set(c_ctx)
    mods = _ada(c_all, w_ada, b_ada).reshape(depth, mod_rows, N_MOD, d)

    w_in_b = w_in.astype(BF16)
    w_o_b = w_o.astype(BF16)
    wi_b = ffn_wi.astype(BF16)
    wo_b = ffn_wo.astype(BF16)
    qk_gain = jnp.concatenate(
        [jnp.tile(qk_g[:, 0], (1, A_Q_HEADS)), jnp.tile(qk_g[:, 1], (1, A_KV_HEADS))], axis=1
    ).reshape(depth, 1, A_Q_W + A_KV_W)
    gsz = (A_Q_W + A_KV_W) // 2
    head_of = np.arange(gsz) // HEAD_DIM
    gmat = jnp.asarray((head_of[:, None] == head_of[None, :]) / HEAD_DIM, dtype=BF16)
    lane_head = np.arange(V7X_LANES) // HEAD_DIM
    gsum = jnp.asarray(lane_head[:, None] == lane_head[None, :], dtype=BF16)
    rope_tabs = _rope_tables(seq)

    tm_ffn = _tile_rows(seq, 512)
    tm_proj = _tile_rows(seq, 1024)
    tm_ctx = _tile_rows(ctx_len, 256)
    tq_a = _tile_rows(seq, 256)
    tk_a = _tile_rows(seq, 512)
    unroll_a = min(GQA_UNROLL, seq // tk_a)
    n_rblocks = seq // (GRID_W * NBR_QROWS)
    nbr_blocks = NBR_BLOCKS_PER_STEP if n_rblocks % NBR_BLOCKS_PER_STEP == 0 else 1

    h, hc = x, ctx
    for layer in range(depth):
        last = layer == depth - 1
        lat = dict(layer=layer, mod_row=None, n_sub=ROW_SUBTILES)
        cx = dict(layer=layer, mod_row=ctx_row, n_sub=1)
        h = _ffn(h, mods, norm_g, wi_b, wo_b, which=0, tm=tm_ffn, **lat)
        hc = _ffn(hc, mods, norm_g, wi_b, wo_b, which=0, tm=tm_ctx, **cx)
        qa, qb, ka, va, kb, vb, oc = _proj(h, mods, norm_g, w_in_b, qk_gain, conv_w, gmat, rope_tabs,
                                           tm=tm_proj, layer=layer, mod_row=None, n_sub=PROJ_SUBTILES)
        cqa, cqb, cka, cva, ckb, cvb, coc = _proj(hc, mods, norm_g, w_in_b, qk_gain, conv_w, gmat, None,
                                                  tm=tm_ctx, layer=layer, mod_row=ctx_row)
        if not last:
            coa = _gqa(cqa, cka, cva, None, hq=A_Q_HEADS, hkv=A_KV_HEADS, tq=tm_ctx, tk=ctx_len)
            cob = _gqa(cqb, ckb, cvb, None, hq=B_HEADS, hkv=B_HEADS, tq=tm_ctx, tk=ctx_len)
            hc = _ffn(hc, mods, norm_g, wi_b, wo_b, which=1, tm=tm_ctx, mix=(coa, cob, coc, w_o_b), **cx)
        ob = _nbr(qb, kb, vb, ckb, cvb, _nbr_bias_tables(rpb[layer], seq // GRID_W), n_blocks=nbr_blocks)
        h = _gqa_bounded(qa, ka, va, cka, cva, gsum, h, ob, oc, mods, norm_g, w_o_b,
                         layer=layer, hq=A_Q_HEADS, tq=tq_a, tk=tk_a, unroll=unroll_a)
        h = _ffn(h, mods, norm_g, wi_b, wo_b, which=1, tm=tm_ffn, **lat)
    return h
```

```python
import functools

import numpy as np
import jax
import jax.numpy as jnp
from jax import lax
from jax.experimental import pallas as pl
from jax.experimental.pallas import tpu as pltpu

F32 = jnp.float32
BF16 = jnp.bfloat16

HEAD_DIM = 64
GRID_W = 64
A_Q_HEADS = 6
A_KV_HEADS = 2
B_HEADS = 6
C_WIDTH = 256
A_Q_W = A_Q_HEADS * HEAD_DIM
A_KV_W = A_KV_HEADS * HEAD_DIM
B_W = B_HEADS * HEAD_DIM
WIN_R = 8
WIN_C = 16
ROPE_FREQS = HEAD_DIM // 4
ROPE_THETA = 10000.0
N_MOD = 9
EPS = 1e-6
NEG = -1e30
LOG2_E = 1.4426950408889634

V7X_LANES = 128
V7X_SCOPED_VMEM_BYTES = 60000 * 1024

NBR_QROWS = 4
NBR_KROWS = NBR_QROWS + WIN_R

GQA_UNROLL = 8
GQA_FALLBACK_UNROLL = 2
SUBTILE_ROWS = 128
NBR_BLOCKS_PER_STEP = 4


def _vmem_limit(estimate_bytes):
    return int(min(max(estimate_bytes, 16 * 1024 * 1024), V7X_SCOPED_VMEM_BYTES))


def _params(n_axes, vmem_bytes):
    return pltpu.CompilerParams(
        dimension_semantics=("arbitrary",) * n_axes,
        vmem_limit_bytes=_vmem_limit(vmem_bytes),
    )


def _rms(x, g):
    ms = jnp.mean(x * x, axis=-1, keepdims=True)
    return x * lax.rsqrt(ms + EPS) * g


def _resident(block_shape, index_map):
    return pl.BlockSpec(block_shape, index_map, pipeline_mode=pl.Buffered(1))


def _ada_kernel(c_ref, w_ref, b_ref, o_ref):
    c = c_ref[...]
    sc = c * jax.nn.sigmoid(c)
    o_ref[...] = jnp.dot(sc, w_ref[...], preferred_element_type=F32,
                         precision=lax.Precision.HIGHEST) + b_ref[...]


def _ada(c_all, w_ada, b_ada):
    depth, d, n = w_ada.shape
    rows = c_all.shape[0]
    tn = d
    return pl.pallas_call(
        _ada_kernel,
        grid=(depth, n // tn),
        in_specs=[
            pl.BlockSpec((rows, d), lambda l, j: (0, 0)),
            pl.BlockSpec((None, d, tn), lambda l, j: (l, 0, j)),
            pl.BlockSpec((None, 1, tn), lambda l, j: (l, 0, j)),
        ],
        out_specs=pl.BlockSpec((None, rows, tn), lambda l, j: (l, 0, j)),
        out_shape=jax.ShapeDtypeStruct((depth, rows, n), F32),
        compiler_params=_params(2, 4 * d * tn * 4),
        name="ada_mod",
    )(c_all, w_ada, b_ada.reshape(depth, 1, n))


def _ffn_kernel(*refs, i0, gi, ffn_dim, n_sub, mix):
    if mix:
        h_ref, oa_ref, ob_ref, oc_ref, mod_ref, g_ref, wmix_ref, wi_ref, wo_ref, o_ref = refs
    else:
        h_ref, mod_ref, g_ref, wi_ref, wo_ref, o_ref = refs
    shift = mod_ref[i0:i0 + 1, :]
    scale = mod_ref[i0 + 1:i0 + 2, :]
    gate = mod_ref[i0 + 2:i0 + 3, :]
    sub = h_ref.shape[0] // n_sub
    rows_of = lambda t: slice(t * sub, (t + 1) * sub)
    hs, hids, acts = {}, {}, {}

    def up_proj(t):
        h = h_ref[rows_of(t), :]
        if mix:
            o = jnp.concatenate([oa_ref[rows_of(t), :], ob_ref[rows_of(t), :], oc_ref[rows_of(t), :]], axis=1)
            ymix = jnp.dot(o, wmix_ref[...], preferred_element_type=F32)
            h = h + mod_ref[5:6, :] * _rms(ymix, g_ref[3:4, :])
        hs[t] = h
        u = _rms(h, g_ref[gi:gi + 1, :]) * (1.0 + scale) + shift
        hids[t] = jnp.dot(u.astype(BF16), wi_ref[...], preferred_element_type=F32)

    def activate(t):
        hid = hids.pop(t)
        gt = hid[:, :ffn_dim]
        up = hid[:, ffn_dim:]
        acts[t] = (gt * jax.nn.sigmoid(gt) * up).astype(BF16)

    def down_proj(t):
        y = jnp.dot(acts.pop(t), wo_ref[...], preferred_element_type=F32)
        o_ref[rows_of(t), :] = hs.pop(t) + 0.5 * gate * _rms(y, g_ref[gi + 1:gi + 2, :])

    for t in range(n_sub):
        up_proj(t)
        activate(t)
        down_proj(t)


def _ffn(h, mods, norm_g, wi, wo, *, layer, which, mod_row, tm, n_sub=1, mix=None):
    bsz, s, d = h.shape
    ffn_dim = wo.shape[2]
    i0 = 6 * which
    gi = 4 * which
    if mod_row is None:
        mod_map = lambda b, i: (layer, b, 0, 0)
    else:
        mod_map = lambda b, i: (layer, mod_row, 0, 0)
    tile = lambda w: pl.BlockSpec((None, tm, w), lambda b, i: (b, i, 0))
    vmem = ((wi.shape[2] * wi.shape[3] + wo.shape[2] * wo.shape[3]) * 2 + tm * d * 4 * 6
            + (tm // n_sub) * ffn_dim * 24)
    in_specs = [tile(d)]
    args = [h]
    if mix is not None:
        in_specs += [tile(o.shape[2]) for o in mix[:3]]
        args += list(mix[:3])
    in_specs += [pl.BlockSpec((None, None, N_MOD, d), mod_map),
                 _resident((None, norm_g.shape[1], d), lambda b, i: (layer, 0, 0))]
    args += [mods, norm_g]
    if mix is not None:
        w_mix = mix[3]
        in_specs.append(_resident((None, w_mix.shape[1], d), lambda b, i: (layer, 0, 0)))
        args.append(w_mix)
        vmem += w_mix.shape[1] * d * 2 + tm * d * 2 * 2
    in_specs += [_resident((None, None, d, 2 * ffn_dim), lambda b, i: (layer, which, 0, 0)),
                 _resident((None, None, ffn_dim, d), lambda b, i: (layer, which, 0, 0))]
    args += [wi, wo]
    return pl.pallas_call(
        functools.partial(_ffn_kernel, i0=i0, gi=gi, ffn_dim=ffn_dim, n_sub=n_sub, mix=mix is not None),
        grid=(bsz, s // tm),
        in_specs=in_specs,
        out_specs=tile(d),
        out_shape=jax.ShapeDtypeStruct(h.shape, F32),
        compiler_params=_params(2, vmem),
        name="mix_out_ffn" if mix is not None else "ffn",
    )(*args)


def _swap_rope_partners(x):
    lane = lax.broadcasted_iota(jnp.int32, x.shape, 1)
    first = (lane % (2 * ROPE_FREQS)) < ROPE_FREQS
    return jnp.where(first,
                     pltpu.roll(x, V7X_LANES - ROPE_FREQS, axis=1),
                     pltpu.roll(x, ROPE_FREQS, axis=1))


def _proj_kernel(*refs, rope, tm, n_tiles, n_sub):
    if rope:
        (h_ref, hp_ref, hn_ref, mod_ref, g_ref, w_ref, qkg_ref, cw_ref, gm_ref, cos_ref, sin_ref,
         qa_ref, qb_ref, ka_ref, va_ref, kb_ref, vb_ref, oc_ref, z_sc, cb_sc) = refs
    else:
        (h_ref, hp_ref, hn_ref, mod_ref, g_ref, w_ref, qkg_ref, cw_ref, gm_ref,
         qa_ref, qb_ref, ka_ref, va_ref, kb_ref, vb_ref, oc_ref, z_sc, cb_sc) = refs
    i = pl.program_id(1)
    shift = mod_ref[3:4, :]
    scale = mod_ref[4:5, :]
    g2 = g_ref[2:3, :]

    def pre(x):
        return (_rms(x, g2) * (1.0 + scale) + shift).astype(BF16)

    o_qa, o_qb = 0, A_Q_W
    o_ka = o_qb + B_W
    o_va = o_ka + A_KV_W
    o_kb = o_va + A_KV_W
    o_vb = o_kb + B_W
    o_cx = o_vb + B_W
    o_cb = o_cx + C_WIDTH
    o_cc = o_cb + C_WIDTH
    q_scale = HEAD_DIM ** -0.5 * LOG2_E
    half = (A_Q_W + A_KV_W) // 2
    sub = tm // n_sub

    for t in range(n_sub):
        rows = slice(t * sub, (t + 1) * sub)
        p = jnp.dot(pre(h_ref[rows, :]), w_ref[...], preferred_element_type=F32)
        xq = jnp.concatenate([p[:, o_qa:o_qa + A_Q_W], p[:, o_ka:o_ka + A_KV_W]], axis=1)
        sq = (xq * xq).astype(BF16)
        ms = jnp.concatenate(
            [jnp.dot(sq[:, :half], gm_ref[...], preferred_element_type=F32),
             jnp.dot(sq[:, half:], gm_ref[...], preferred_element_type=F32)], axis=1)
        xn = xq * lax.rsqrt(ms + EPS) * qkg_ref[...]
        if rope:
            cos = cos_ref[rows, :]
            sin = sin_ref[rows, :]
            cols = []
            for j in range((A_Q_W + A_KV_W) // V7X_LANES):
                xc = xn[:, j * V7X_LANES:(j + 1) * V7X_LANES]
                cols.append(xc * cos + _swap_rope_partners(xc) * sin)
            xn = jnp.concatenate(cols, axis=1)
        qa_ref[rows, :] = (xn[:, :A_Q_W] * q_scale).astype(BF16)
        ka_ref[rows, :] = xn[:, A_Q_W:].astype(BF16)
        qb_ref[rows, :] = (p[:, o_qb:o_qb + B_W] * q_scale).astype(BF16)
        va_ref[rows, :] = p[:, o_va:o_va + A_KV_W].astype(BF16)
        kb_ref[rows, :] = p[:, o_kb:o_kb + B_W].astype(BF16)
        vb_ref[rows, :] = p[:, o_vb:o_vb + B_W].astype(BF16)
        z_sc[rows, :] = p[:, o_cc:o_cc + C_WIDTH] * p[:, o_cx:o_cx + C_WIDTH]
        cb_sc[rows, :] = p[:, o_cb:o_cb + C_WIDTH]

    z = z_sc[...]

    def halo(ref):
        ub = pre(ref[...])
        return (jnp.dot(ub, w_ref[:, o_cc:o_cc + C_WIDTH], preferred_element_type=F32)
                * jnp.dot(ub, w_ref[:, o_cx:o_cx + C_WIDTH], preferred_element_type=F32))

    halo_rows = hp_ref.shape[0]
    z_before = jnp.where(i > 0, halo(hp_ref)[halo_rows - 1:halo_rows, :], 0.0)
    z_after = jnp.where(i < n_tiles - 1, halo(hn_ref)[0:1, :], 0.0)
    row = lax.broadcasted_iota(jnp.int32, z.shape, 0)
    z_m1 = jnp.where(row == 0, z_before, pltpu.roll(z, 1, axis=0))
    z_p1 = jnp.where(row == tm - 1, z_after, pltpu.roll(z, tm - 1, axis=0))
    y = cw_ref[0:1, :] * z_m1 + cw_ref[1:2, :] * z + cw_ref[2:3, :] * z_p1
    oc_ref[...] = (cb_sc[...] * y).astype(BF16)


def _proj(h, mods, norm_g, w_in, qk_gain, conv_w, gmat, rope_tabs, *, layer, mod_row, tm, n_sub=1):
    bsz, s, d = h.shape
    n_tiles = s // tm
    halo_rows = 8
    hb = tm // halo_rows
    n_hblk = s // halo_rows
    rope = rope_tabs is not None
    if mod_row is None:
        mod_map = lambda b, i: (layer, b, 0, 0)
    else:
        mod_map = lambda b, i: (layer, mod_row, 0, 0)
    pw = w_in.shape[2]
    in_specs = [
        pl.BlockSpec((None, tm, d), lambda b, i: (b, i, 0)),
        pl.BlockSpec((None, halo_rows, d), lambda b, i: (b, jnp.maximum(i * hb - 1, 0), 0)),
        pl.BlockSpec((None, halo_rows, d), lambda b, i: (b, jnp.minimum((i + 1) * hb, n_hblk - 1), 0)),
        pl.BlockSpec((None, None, N_MOD, d), mod_map),
        _resident((None, norm_g.shape[1], d), lambda b, i: (layer, 0, 0)),
        _resident((None, d, pw), lambda b, i: (layer, 0, 0)),
        _resident((None, 1, A_Q_W + A_KV_W), lambda b, i: (layer, 0, 0)),
        _resident((None, conv_w.shape[1], C_WIDTH), lambda b, i: (layer, 0, 0)),
        _resident(gmat.shape, lambda b, i: (0, 0)),
    ]
    args = [h, h, h, mods, norm_g, w_in, qk_gain, conv_w, gmat]
    if rope:
        in_specs += [pl.BlockSpec((tm, V7X_LANES), lambda b, i: (i, 0))] * 2
        args += list(rope_tabs)
    widths = (A_Q_W, B_W, A_KV_W, A_KV_W, B_W, B_W, C_WIDTH)
    out_specs = [pl.BlockSpec((None, tm, w), lambda b, i: (b, i, 0)) for w in widths]
    out_shape = [jax.ShapeDtypeStruct((bsz, s, w), BF16) for w in widths]
    vmem = d * pw * 2 + tm * d * 4 * 4 + tm * pw * 12
    return pl.pallas_call(
        functools.partial(_proj_kernel, rope=rope, tm=tm, n_tiles=n_tiles, n_sub=n_sub),
        grid=(bsz, n_tiles),
        in_specs=in_specs,
        out_specs=out_specs,
        out_shape=out_shape,
        scratch_shapes=[pltpu.VMEM((tm, C_WIDTH), F32), pltpu.VMEM((tm, C_WIDTH), F32)],
        compiler_params=_params(2, vmem),
        name="mix_proj",
    )(*args)


_NT_DIMS = (((1,), (1,)), ((), ()))

SCORE_BOUND_MAX = 40.0


def _load_q_rows(q_ref, qs_sc, heads, grp, tq):
    lane = lax.broadcasted_iota(jnp.int32, (tq, V7X_LANES), 1)
    low_half = lane < HEAD_DIM
    for j, h in enumerate(heads):
        xc = q_ref[:, (h // 2) * V7X_LANES:(h // 2 + 1) * V7X_LANES].astype(F32)
        dst_low = (h // grp) % 2 == 0
        if (h % 2 == 0) != dst_low:
            xc = pltpu.roll(xc, HEAD_DIM, axis=1)
        keep = low_half if dst_low else jnp.logical_not(low_half)
        qs_sc[j * tq:(j + 1) * tq, :] = jnp.where(keep, xc, 0.0).astype(BF16)


def _repack_heads(on_sc, hq, grp, tq):
    lane = lax.broadcasted_iota(jnp.int32, (tq, V7X_LANES), 1)
    low_half = lane < HEAD_DIM
    cols = []
    for oc in range(hq // 2):
        pieces = []
        for e in range(2):
            h = 2 * oc + e
            piece = on_sc[h * tq:(h + 1) * tq, :]
            src_low = (h // grp) % 2 == 0
            if src_low != (e == 0):
                piece = pltpu.roll(piece, HEAD_DIM, axis=1)
            pieces.append(piece)
        cols.append(jnp.where(low_half, pieces[0], pieces[1]).astype(BF16))
    return cols


def _online_softmax(qs_sc, m_sc, l_sc, acc_sc, rows, chunks, extra, *, tk, n_chunks, unroll):
    m_sc[0:rows, :] = jnp.full((rows, V7X_LANES), NEG, F32)
    l_sc[0:rows, :] = jnp.zeros((rows, V7X_LANES), F32)
    acc_sc[0:rows, :] = jnp.zeros((rows, V7X_LANES), F32)

    def load_state():
        return m_sc[0:rows, :], l_sc[0:rows, :], acc_sc[0:rows, :]

    def store_state(state):
        m_sc[0:rows, :], l_sc[0:rows, :], acc_sc[0:rows, :] = state

    def step(state, kc, vc):
        m_prev, l_prev, acc_prev = state
        s = lax.dot_general(qs_sc[0:rows, :], kc, _NT_DIMS, preferred_element_type=F32)
        m_next = jnp.maximum(m_prev, jnp.max(s, axis=1, keepdims=True))
        alpha = jnp.exp2(m_prev - m_next)
        p = jnp.exp2(s - jnp.concatenate([m_next] * (kc.shape[0] // V7X_LANES), axis=1))
        l_next = alpha * l_prev + jnp.sum(p, axis=1, keepdims=True)
        acc_next = alpha * acc_prev + jnp.dot(p.astype(BF16), vc, preferred_element_type=F32)
        return m_next, l_next, acc_next

    def body(c, carry):
        state = load_state()
        for u in range(unroll):
            state = step(state, *chunks(pl.multiple_of((c * unroll + u) * tk, tk)))
        store_state(state)
        return carry

    lax.fori_loop(0, n_chunks // unroll, body, 0)
    if extra is not None:
        store_state(step(load_state(), *extra))
    return acc_sc[0:rows, :] / l_sc[0:rows, :]


def _bounded_softmax(qs_sc, l_sc, acc_sc, rows, chunks, extra, *, tk, n_chunks, unroll):
    l_sc[0:rows, :] = jnp.zeros((rows, V7X_LANES), F32)
    acc_sc[0:rows, :] = jnp.zeros((rows, V7X_LANES), F32)

    def step(state, kc, vc):
        l_prev, acc_prev = state
        n_cols = kc.shape[0] // V7X_LANES
        s = lax.dot_general(qs_sc[0:rows, :], kc, _NT_DIMS, preferred_element_type=F32)
        p = jnp.exp2(s)
        l_next = l_prev
        for j in range(n_cols):
            l_next = l_next + p[:, j * V7X_LANES:(j + 1) * V7X_LANES]
        acc_next = acc_prev + jnp.dot(p.astype(BF16), vc, preferred_element_type=F32)
        return l_next, acc_next

    def body(c, carry):
        state = (l_sc[0:rows, :], acc_sc[0:rows, :])
        for u in range(unroll):
            state = step(state, *chunks(pl.multiple_of((c * unroll + u) * tk, tk)))
        l_sc[0:rows, :], acc_sc[0:rows, :] = state
        return carry

    n_trips = n_chunks // unroll
    lax.fori_loop(0, n_trips - 1, body, 0)
    state = (l_sc[0:rows, :], acc_sc[0:rows, :])
    for u in range(unroll):
        state = step(state, *chunks(((n_trips - 1) * unroll + u) * tk))
    if extra is not None:
        state = step(state, *extra)
    l_lanes, acc = state
    return acc / jnp.sum(l_lanes, axis=1, keepdims=True)


def _gqa_kernel(*refs, hq, hkv, tq, tk, n_chunks, unroll, extra_len):
    if extra_len:
        q_ref, k_ref, v_ref, kx_ref, vx_ref, o_ref, qs_sc, m_sc, l_sc, acc_sc, on_sc = refs
    else:
        q_ref, k_ref, v_ref, o_ref, qs_sc, m_sc, l_sc, acc_sc, on_sc = refs
    grp = hq // hkv
    for pc in range(hkv // 2):
        heads = [h for h in range(hq) if (h // grp) // 2 == pc]
        rows = len(heads) * tq
        col = slice(pc * V7X_LANES, (pc + 1) * V7X_LANES)
        _load_q_rows(q_ref, qs_sc, heads, grp, tq)
        chunks = lambda start: (k_ref[pl.ds(start, tk), col], v_ref[pl.ds(start, tk), col])
        extra = (kx_ref[:, col], vx_ref[:, col]) if extra_len else None
        on = _online_softmax(qs_sc, m_sc, l_sc, acc_sc, rows, chunks, extra,
                             tk=tk, n_chunks=n_chunks, unroll=unroll)
        for j, h in enumerate(heads):
            on_sc[h * tq:(h + 1) * tq, :] = on[j * tq:(j + 1) * tq, :]
    o_ref[...] = jnp.concatenate(_repack_heads(on_sc, hq, grp, tq), axis=1)


def _gqa_bounded_kernel(qsq_ref, q_ref, k_ref, v_ref, kx_ref, vx_ref, gs_ref, h_ref, ob_ref, oc_ref, mod_ref,
                        g_ref, wmix_ref, o_ref, qs_sc, m_sc, l_sc, acc_sc, on_sc, ok_sc,
                        *, layer, hq, tq, tk, n_chunks, unroll, n_q):
    hkv = 2
    grp = hq // hkv
    rows = hq * tq
    step = pl.program_id(1)

    def max_sq_norm(kc, mx):
        kf = kc.astype(F32)
        ss = lax.dot_general(gs_ref[...], (kf * kf).astype(BF16), _NT_DIMS, preferred_element_type=F32)
        return jnp.maximum(mx, jnp.max(ss, axis=1, keepdims=True))

    @pl.when(step == 0)
    def _():
        def kbody(c, mx):
            return max_sq_norm(k_ref[pl.ds(pl.multiple_of(c * tk, tk), tk), :], mx)
        mx = lax.fori_loop(0, n_chunks, kbody, jnp.zeros((V7X_LANES, 1), F32))
        k_sq = jnp.max(max_sq_norm(kx_ref[...], mx))
        score_sq = qsq_ref[layer] * k_sq * (1.0 + 2.0 ** -4)
        ok_sc[0] = (score_sq <= SCORE_BOUND_MAX * SCORE_BOUND_MAX).astype(jnp.int32)
        on_sc[...] = jnp.zeros(on_sc.shape, F32)

    o = jnp.concatenate(_repack_heads(on_sc, hq, grp, tq) + [ob_ref[...], oc_ref[...]], axis=1)
    y = jnp.dot(o, wmix_ref[...], preferred_element_type=F32)
    o_ref[...] = h_ref[...] + mod_ref[5:6, :] * _rms(y, g_ref[3:4, :])

    _load_q_rows(q_ref, qs_sc, list(range(hq)), grp, tq)
    bounded_ok = ok_sc[0] == 1
    has_tile = step < n_q

    chunks = lambda start: (k_ref[pl.ds(start, tk), :], v_ref[pl.ds(start, tk), :])
    extra = (kx_ref[...], vx_ref[...])

    @pl.when(jnp.logical_and(has_tile, bounded_ok))
    def _():
        on_sc[...] = _bounded_softmax(qs_sc, l_sc, acc_sc, rows, chunks, extra,
                                      tk=tk, n_chunks=n_chunks, unroll=unroll)

    @pl.when(jnp.logical_and(has_tile, jnp.logical_not(bounded_ok)))
    def _():
        on_sc[...] = _online_softmax(qs_sc, m_sc, l_sc, acc_sc, rows, chunks, extra,
                                     tk=tk, n_chunks=n_chunks, unroll=min(unroll, GQA_FALLBACK_UNROLL))


def _gqa(q, k, v, extra, *, hq, hkv, tq, tk, unroll=1):
    bsz, s, qw = q.shape
    t = k.shape[1]
    kw = k.shape[2]
    n_chunks = t // tk
    assert n_chunks % unroll == 0
    grp = hq // hkv
    max_heads = max(sum(1 for h in range(hq) if (h // grp) // 2 == pc) for pc in range(hkv // 2))
    rows = max_heads * tq
    in_specs = [
        pl.BlockSpec((None, tq, qw), lambda b, i: (b, i, 0)),
        pl.BlockSpec((None, t, kw), lambda b, i: (b, 0, 0)),
        pl.BlockSpec((None, t, kw), lambda b, i: (b, 0, 0)),
    ]
    args = [q, k, v]
    extra_len = 0
    if extra is not None:
        extra_len = extra[0].shape[1]
        in_specs += [pl.BlockSpec((None, extra_len, kw), lambda b, i: (b, 0, 0))] * 2
        args += list(extra)
    vmem = 4 * t * kw * 2 + rows * max(tk, extra_len) * 16 + rows * V7X_LANES * 24 + 8 * tq * qw * 2
    return pl.pallas_call(
        functools.partial(_gqa_kernel, hq=hq, hkv=hkv, tq=tq, tk=tk, n_chunks=n_chunks, unroll=unroll,
                          extra_len=extra_len),
        grid=(bsz, s // tq),
        in_specs=in_specs,
        out_specs=pl.BlockSpec((None, tq, qw), lambda b, i: (b, i, 0)),
        out_shape=jax.ShapeDtypeStruct(q.shape, BF16),
        scratch_shapes=[
            pltpu.VMEM((rows, V7X_LANES), BF16),
            pltpu.VMEM((rows, V7X_LANES), F32),
            pltpu.VMEM((rows, V7X_LANES), F32),
            pltpu.VMEM((rows, V7X_LANES), F32),
            pltpu.VMEM((hq * tq, V7X_LANES), F32),
        ],
        compiler_params=_params(2, vmem),
        name="gqa_attn",
    )(*args)


def _gqa_bounded(q_sq_max, q, k, v, kx, vx, gsum, h, ob, oc, mods, norm_g, w_o, *, layer, hq, tq, tk, unroll):
    bsz, s, qw = q.shape
    d = h.shape[2]
    t, kw = k.shape[1], k.shape[2]
    lx = kx.shape[1]
    n_q = s // tq
    assert kw == V7X_LANES and t % (tk * unroll) == 0
    rows = hq * tq
    stat = pltpu.VMEM((rows, V7X_LANES), F32)
    keys = lambda n: pl.BlockSpec((None, n, kw), lambda b, i: (b, 0, 0))
    prev = lambda w: pl.BlockSpec((None, tq, w), lambda b, i: (b, jnp.maximum(i - 1, 0), 0))
    vmem = (4 * (t + lx) * kw * 2 + rows * max(tk, lx) * 8 * min(unroll, 4) + rows * V7X_LANES * 32
            + 8 * tq * qw * 2 + w_o.shape[1] * d * 2 + 6 * tq * d * 4)
    return pl.pallas_call(
        functools.partial(_gqa_bounded_kernel, layer=layer, hq=hq, tq=tq, tk=tk, n_chunks=t // tk, unroll=unroll,
                          n_q=n_q),
        grid=(bsz, n_q + 1),
        in_specs=[
            pl.BlockSpec(memory_space=pltpu.SMEM),
            pl.BlockSpec((None, tq, qw), lambda b, i: (b, jnp.minimum(i, n_q - 1), 0)),
            keys(t), keys(t), keys(lx), keys(lx),
            _resident(gsum.shape, lambda b, i: (0, 0)),
            prev(d), prev(ob.shape[2]), prev(oc.shape[2]),
            pl.BlockSpec((None, None, N_MOD, d), lambda b, i: (layer, b, 0, 0)),
            _resident((None, norm_g.shape[1], d), lambda b, i: (layer, 0, 0)),
            _resident((None, w_o.shape[1], d), lambda b, i: (layer, 0, 0)),
        ],
        out_specs=prev(d),
        out_shape=jax.ShapeDtypeStruct(h.shape, F32),
        scratch_shapes=[
            pltpu.VMEM((rows, V7X_LANES), BF16), stat, stat, stat, stat,
            pltpu.SMEM((1,), jnp.int32),
        ],
        compiler_params=_params(2, vmem),
        name="gqa_attn_mix_out",
    )(q_sq_max, q, k, v, kx, vx, gsum, h, ob, oc, mods, norm_g, w_o)


def _nbr_kernel(q_ref, k_ref, v_ref, kx_ref, vx_ref, *rest, n_blocks, rows_total):
    bias_refs, o_ref = rest[:n_blocks], rest[n_blocks]
    tq = NBR_QROWS * GRID_W
    n_win = NBR_KROWS * GRID_W
    kx = kx_ref[...]
    vx = vx_ref[...]
    lane = lax.broadcasted_iota(jnp.int32, (tq, V7X_LANES), 1)
    low_half = lane < HEAD_DIM
    starts, scores, probs = {}, {}, {}

    def qk(j):
        rb = pl.program_id(2) * n_blocks + j
        krow0 = jnp.clip(rb * NBR_QROWS - WIN_R // 2, 0, rows_total - NBR_KROWS)
        starts[j] = pl.multiple_of(krow0 * GRID_W, NBR_QROWS * GRID_W)
        kw = k_ref[pl.ds(starts[j], n_win), :]
        q = q_ref[j * tq:(j + 1) * tq, :]
        zero = jnp.zeros_like(q)
        qs = jnp.concatenate([jnp.where(low_half, q, zero), jnp.where(low_half, zero, q)], axis=0)
        bias = bias_refs[j][...].reshape(2 * tq, n_win)
        scores[j] = (lax.dot_general(qs, kw, _NT_DIMS, preferred_element_type=F32) + bias,
                     lax.dot_general(qs, kx, _NT_DIMS, preferred_element_type=F32))

    def soft(j):
        s_win, s_ctx = scores.pop(j)
        m = jnp.maximum(jnp.max(s_win, axis=1, keepdims=True), jnp.max(s_ctx, axis=1, keepdims=True))
        p_win = jnp.exp2(s_win - m)
        p_ctx = jnp.exp2(s_ctx - m)
        l = jnp.sum(p_win, axis=1, keepdims=True) + jnp.sum(p_ctx, axis=1, keepdims=True)
        probs[j] = (p_win.astype(BF16), p_ctx.astype(BF16), l)

    def pv(j):
        p_win, p_ctx, l = probs.pop(j)
        vw = v_ref[pl.ds(starts[j], n_win), :]
        o = (jnp.dot(p_win, vw, preferred_element_type=F32)
             + jnp.dot(p_ctx, vx, preferred_element_type=F32)) / l
        o_ref[j * tq:(j + 1) * tq, :] = jnp.where(low_half, o[:tq], o[tq:]).astype(BF16)

    for stage in (qk, soft, pv):
        for j in range(n_blocks):
            stage(j)


def _nbr(q, k, v, kx, vx, bias, *, n_blocks=1):
    bsz, s, w = q.shape
    rows_total = s // GRID_W
    n_rblocks = rows_total // NBR_QROWS
    assert n_rblocks % n_blocks == 0
    tq = NBR_QROWS * GRID_W
    n_win = NBR_KROWS * GRID_W
    lx = kx.shape[1]
    n_pairs = w // V7X_LANES

    def bias_spec(j):
        def bias_map(b, pr, st):
            rb = st * n_blocks + j
            kind = jnp.where(rb == 0, 0, jnp.where(rb == n_rblocks - 1, 2, 1))
            return (kind, pr, 0, 0)
        return pl.BlockSpec((None, 2, tq, n_win), bias_map)

    qo_spec = pl.BlockSpec((None, n_blocks * tq, V7X_LANES), lambda b, pr, st: (b, st, pr))
    vmem = 4 * s * V7X_LANES * 2 + n_blocks * (2 * 2 * tq * n_win * 4 + tq * (n_win + lx) * 24)
    return pl.pallas_call(
        functools.partial(_nbr_kernel, n_blocks=n_blocks, rows_total=rows_total),
        grid=(bsz, n_pairs, n_rblocks // n_blocks),
        in_specs=[
            qo_spec,
            pl.BlockSpec((None, s, V7X_LANES), lambda b, pr, st: (b, 0, pr)),
            pl.BlockSpec((None, s, V7X_LANES), lambda b, pr, st: (b, 0, pr)),
            pl.BlockSpec((None, lx, V7X_LANES), lambda b, pr, st: (b, 0, pr)),
            pl.BlockSpec((None, lx, V7X_LANES), lambda b, pr, st: (b, 0, pr)),
        ] + [bias_spec(j) for j in range(n_blocks)],
        out_specs=qo_spec,
        out_shape=jax.ShapeDtypeStruct(q.shape, BF16),
        compiler_params=_params(3, vmem),
        name="nbr_attn",
    )(q, k, v, kx, vx, *([bias] * n_blocks))


def _nbr_bias_tables(rpb, rows_total):
    assert rows_total % NBR_QROWS == 0 and rows_total >= NBR_KROWS + NBR_QROWS
    wr = min(WIN_R, rows_total)
    kinds = [(0, 0), (2 * NBR_QROWS, 2 * NBR_QROWS - WIN_R // 2),
             (rows_total - NBR_QROWS, rows_total - NBR_KROWS)]
    qi = np.arange(NBR_QROWS)[:, None, None, None]
    qc = np.arange(GRID_W)[None, :, None, None]
    kj = np.arange(NBR_KROWS)[None, None, :, None]
    kc = np.arange(GRID_W)[None, None, None, :]
    n_dr, n_dc = rpb.shape[1], rpb.shape[2]
    cs = np.clip(qc - WIN_C // 2, 0, GRID_W - WIN_C)
    col_ok = (kc >= cs) & (kc < cs + WIN_C)
    col_sel = (kc - qc + (WIN_C - 1))[..., None] == np.arange(n_dc)
    col_sel = (col_sel & col_ok[..., None])[0, :, 0].astype(np.float32)
    row_sel, ok_all = [], []
    for r0, k0 in kinds:
        r = r0 + qi
        rs = np.clip(r - wr // 2, 0, rows_total - wr)
        kr = k0 + kj
        row_ok = (kr >= rs) & (kr < rs + wr)
        sel = ((kr - r + (WIN_R - 1))[..., None] == np.arange(n_dr)) & row_ok[..., None]
        row_sel.append(sel[:, 0, :, 0].astype(np.float32))
        ok_all.append(np.broadcast_to(row_ok & col_ok, (NBR_QROWS, GRID_W, NBR_KROWS, GRID_W)))
    row_sel = jnp.asarray(np.stack(row_sel))
    ok = np.stack(ok_all).reshape(3, 1, NBR_QROWS * GRID_W, NBR_KROWS * GRID_W)
    hi = lax.Precision.HIGHEST
    rows_picked = jnp.einsum('hrd,tijr->thijd', rpb, row_sel, precision=hi)
    tab = jnp.einsum('thijd,cnd->thicjn', rows_picked, jnp.asarray(col_sel), precision=hi)
    tab = tab.reshape(3, rpb.shape[0], NBR_QROWS * GRID_W, NBR_KROWS * GRID_W)
    return jnp.where(ok, tab * LOG2_E, NEG).astype(F32)


def _rope_tables(seq):
    pos = jnp.arange(seq, dtype=jnp.int32)
    row = (pos // GRID_W).astype(F32)
    col = (pos % GRID_W).astype(F32)
    freq = 1.0 / (ROPE_THETA ** (jnp.arange(ROPE_FREQS, dtype=F32) / ROPE_FREQS))
    ar = row[:, None] * freq
    ac = col[:, None] * freq
    cos = jnp.concatenate([jnp.cos(ar), jnp.cos(ar), jnp.cos(ac), jnp.cos(ac)], axis=1)
    sin = jnp.concatenate([-jnp.sin(ar), jnp.sin(ar), -jnp.sin(ac), jnp.sin(ac)], axis=1)
    reps = V7X_LANES // HEAD_DIM
    return jnp.tile(cos, (1, reps)), jnp.tile(sin, (1, reps))


def _tile_rows(n, target):
    t = min(n, target)
    assert n % t == 0
    return t


def kernel(x, c, ctx, c_ctx, w_ada, b_ada, norm_g, w_in, qk_g, rpb, conv_w, w_o, ffn_wi, ffn_wo):
    bsz, seq, d = x.shape
    ctx_len = ctx.shape[1]
    depth = w_ada.shape[0]
    ctx_row = bsz

    mod_rows = -(-(bsz + 1) // 8) * 8
    c_all = jnp.zeros((mod_rows, d), F32).at[:bsz].set(c).at[ctx_row].set(c_ctx)
    mods = _ada(c_all, w_ada, b_ada).reshape(depth, mod_rows, N_MOD, d)

    w_in_b = w_in.astype(BF16)
    w_o_b = w_o.astype(BF16)
    wi_b = ffn_wi.astype(BF16)
    wo_b = ffn_wo.astype(BF16)
    qk_gain = jnp.concatenate(
        [jnp.tile(qk_g[:, 0], (1, A_Q_HEADS)), jnp.tile(qk_g[:, 1], (1, A_KV_HEADS))], axis=1
    ).reshape(depth, 1, A_Q_W + A_KV_W)
    gsz = (A_Q_W + A_KV_W) // 2
    head_of = np.arange(gsz) // HEAD_DIM
    gmat = jnp.asarray((head_of[:, None] == head_of[None, :]) / HEAD_DIM, dtype=BF16)
    lane_head = np.arange(V7X_LANES) // HEAD_DIM
    gsum = jnp.asarray(lane_head[:, None] == lane_head[None, :], dtype=BF16)
    rope_tabs = _rope_tables(seq)
    q_sq_max = (LOG2_E ** 2) * jnp.max(jnp.square(qk_g[:, 0]), axis=-1).astype(F32)

    tm_ffn = _tile_rows(seq, 512)
    tm_proj = _tile_rows(seq, 2048)
    tm_ctx = _tile_rows(ctx_len, 256)
    tq_a = _tile_rows(seq, 256)
    tk_a = _tile_rows(seq, 512)
    unroll_a = min(GQA_UNROLL, seq // tk_a)
    n_rblocks = seq // (GRID_W * NBR_QROWS)
    nbr_blocks = NBR_BLOCKS_PER_STEP if n_rblocks % NBR_BLOCKS_PER_STEP == 0 else 1

    h, hc = x, ctx
    for layer in range(depth):
        last = layer == depth - 1
        lat = dict(layer=layer, mod_row=None, n_sub=max(tm_ffn // SUBTILE_ROWS, 1))
        cx = dict(layer=layer, mod_row=ctx_row, n_sub=1)
        h = _ffn(h, mods, norm_g, wi_b, wo_b, which=0, tm=tm_ffn, **lat)
        hc = _ffn(hc, mods, norm_g, wi_b, wo_b, which=0, tm=tm_ctx, **cx)
        qa, qb, ka, va, kb, vb, oc = _proj(h, mods, norm_g, w_in_b, qk_gain, conv_w, gmat, rope_tabs,
                                           tm=tm_proj, layer=layer, mod_row=None,
                                           n_sub=max(tm_proj // SUBTILE_ROWS, 1))
        cqa, cqb, cka, cva, ckb, cvb, coc = _proj(hc, mods, norm_g, w_in_b, qk_gain, conv_w, gmat, None,
                                                  tm=tm_ctx, layer=layer, mod_row=ctx_row)
        if not last:
            coa = _gqa(cqa, cka, cva, None, hq=A_Q_HEADS, hkv=A_KV_HEADS, tq=tm_ctx, tk=ctx_len)
            cob = _gqa(cqb, ckb, cvb, None, hq=B_HEADS, hkv=B_HEADS, tq=tm_ctx, tk=ctx_len)
            hc = _ffn(hc, mods, norm_g, wi_b, wo_b, which=1, tm=tm_ctx, mix=(coa, cob, coc, w_o_b), **cx)
        ob = _nbr(qb, kb, vb, ckb, cvb, _nbr_bias_tables(rpb[layer], seq // GRID_W), n_blocks=nbr_blocks)
        h = _gqa_bounded(q_sq_max, qa, ka, va, cka, cva, gsum, h, ob, oc, mods, norm_g, w_o_b,
                         layer=layer, hq=A_Q_HEADS, tq=tq_a, tk=tk_a, unroll=unroll_a)
        h = _ffn(h, mods, norm_g, wi_b, wo_b, which=1, tm=tm_ffn, **lat)
    return h
```

```python
import functools

import numpy as np
import jax
import jax.numpy as jnp
from jax import lax
from jax.experimental import pallas as pl
from jax.experimental.pallas import tpu as pltpu

F32 = jnp.float32
BF16 = jnp.bfloat16

HEAD_DIM = 64
GRID_W = 64
A_Q_HEADS = 6
A_KV_HEADS = 2
B_HEADS = 6
C_WIDTH = 256
A_Q_W = A_Q_HEADS * HEAD_DIM
A_KV_W = A_KV_HEADS * HEAD_DIM
B_W = B_HEADS * HEAD_DIM
WIN_R = 8
WIN_C = 16
ROPE_FREQS = HEAD_DIM // 4
ROPE_THETA = 10000.0
N_MOD = 9
EPS = 1e-6
NEG = -1e30
LOG2_E = 1.4426950408889634

V7X_LANES = 128
V7X_SCOPED_VMEM_BYTES = 60000 * 1024

NBR_QROWS = 4
NBR_KROWS = NBR_QROWS + WIN_R

GQA_UNROLL = 8
GQA_FALLBACK_UNROLL = 2
SUBTILE_ROWS = 128
NBR_BLOCKS_PER_STEP = 4


def _vmem_limit(estimate_bytes):
    return int(min(max(estimate_bytes, 16 * 1024 * 1024), V7X_SCOPED_VMEM_BYTES))


def _params(n_axes, vmem_bytes, fusible=None):
    return pltpu.CompilerParams(
        dimension_semantics=("arbitrary",) * n_axes,
        vmem_limit_bytes=_vmem_limit(vmem_bytes),
        allow_input_fusion=fusible,
    )


def _rms(x, g):
    ms = jnp.mean(x * x, axis=-1, keepdims=True)
    return x * lax.rsqrt(ms + EPS) * g


def _resident(block_shape, index_map):
    return pl.BlockSpec(block_shape, index_map, pipeline_mode=pl.Buffered(1))


def _ada_kernel(c_ref, w_ref, b_ref, o_ref):
    c = c_ref[...]
    sc = c * jax.nn.sigmoid(c)
    o_ref[...] = jnp.dot(sc, w_ref[...], preferred_element_type=F32,
                         precision=lax.Precision.HIGHEST) + b_ref[...]


def _ada(c_all, w_ada, b_ada):
    depth, d, n = w_ada.shape
    rows = c_all.shape[0]
    tn = d
    return pl.pallas_call(
        _ada_kernel,
        grid=(depth, n // tn),
        in_specs=[
            pl.BlockSpec((rows, d), lambda l, j: (0, 0)),
            pl.BlockSpec((None, d, tn), lambda l, j: (l, 0, j)),
            pl.BlockSpec((None, 1, tn), lambda l, j: (l, 0, j)),
        ],
        out_specs=pl.BlockSpec((None, rows, tn), lambda l, j: (l, 0, j)),
        out_shape=jax.ShapeDtypeStruct((depth, rows, n), F32),
        compiler_params=_params(2, 4 * d * tn * 4),
        name="ada_mod",
    )(c_all, w_ada, b_ada.reshape(depth, 1, n))


def _ffn_kernel(*refs, i0, gi, ffn_dim, n_sub, mix):
    if mix:
        h_ref, oa_ref, ob_ref, oc_ref, mod_ref, g_ref, wmix_ref, wi_ref, wo_ref, o_ref = refs
    else:
        h_ref, mod_ref, g_ref, wi_ref, wo_ref, o_ref = refs
    shift = mod_ref[i0:i0 + 1, :]
    scale = mod_ref[i0 + 1:i0 + 2, :]
    gate = mod_ref[i0 + 2:i0 + 3, :]
    sub = h_ref.shape[0] // n_sub
    rows_of = lambda t: slice(t * sub, (t + 1) * sub)
    hs, hids, acts = {}, {}, {}

    def up_proj(t):
        h = h_ref[rows_of(t), :]
        if mix:
            o = jnp.concatenate([oa_ref[rows_of(t), :], ob_ref[rows_of(t), :], oc_ref[rows_of(t), :]], axis=1)
            ymix = jnp.dot(o, wmix_ref[...], preferred_element_type=F32)
            h = h + mod_ref[5:6, :] * _rms(ymix, g_ref[3:4, :])
        hs[t] = h
        u = _rms(h, g_ref[gi:gi + 1, :]) * (1.0 + scale) + shift
        hids[t] = jnp.dot(u.astype(BF16), wi_ref[...], preferred_element_type=F32)

    def activate(t):
        hid = hids.pop(t)
        gt = hid[:, :ffn_dim]
        up = hid[:, ffn_dim:]
        acts[t] = (gt * jax.nn.sigmoid(gt) * up).astype(BF16)

    def down_proj(t):
        y = jnp.dot(acts.pop(t), wo_ref[...], preferred_element_type=F32)
        o_ref[rows_of(t), :] = hs.pop(t) + 0.5 * gate * _rms(y, g_ref[gi + 1:gi + 2, :])

    for t in range(n_sub):
        up_proj(t)
        activate(t)
        down_proj(t)


def _ffn(h, mods, norm_g, wi, wo, *, layer, which, mod_row, tm, n_sub=1, mix=None):
    bsz, s, d = h.shape
    ffn_dim = wo.shape[2]
    i0 = 6 * which
    gi = 4 * which
    if mod_row is None:
        mod_map = lambda b, i: (layer, b, 0, 0)
    else:
        mod_map = lambda b, i: (layer, mod_row, 0, 0)
    tile = lambda w: pl.BlockSpec((None, tm, w), lambda b, i: (b, i, 0))
    vmem = ((wi.shape[2] * wi.shape[3] + wo.shape[2] * wo.shape[3]) * 2 + tm * d * 4 * 6
            + (tm // n_sub) * ffn_dim * 24)
    in_specs = [tile(d)]
    args = [h]
    if mix is not None:
        in_specs += [tile(o.shape[2]) for o in mix[:3]]
        args += list(mix[:3])
    in_specs += [pl.BlockSpec((None, None, N_MOD, d), mod_map),
                 _resident((None, norm_g.shape[1], d), lambda b, i: (layer, 0, 0))]
    args += [mods, norm_g]
    if mix is not None:
        w_mix = mix[3]
        in_specs.append(_resident((None, w_mix.shape[1], d), lambda b, i: (layer, 0, 0)))
        args.append(w_mix)
        vmem += w_mix.shape[1] * d * 2 + tm * d * 2 * 2
    in_specs += [_resident((None, None, d, 2 * ffn_dim), lambda b, i: (layer, which, 0, 0)),
                 _resident((None, None, ffn_dim, d), lambda b, i: (layer, which, 0, 0))]
    args += [wi, wo]
    return pl.pallas_call(
        functools.partial(_ffn_kernel, i0=i0, gi=gi, ffn_dim=ffn_dim, n_sub=n_sub, mix=mix is not None),
        grid=(bsz, s // tm),
        in_specs=in_specs,
        out_specs=tile(d),
        out_shape=jax.ShapeDtypeStruct(h.shape, F32),
        compiler_params=_params(2, vmem, fusible=[False] * (len(args) - 2) + [True, True]),
        name="mix_out_ffn" if mix is not None else "ffn",
    )(*args)


def _swap_rope_partners(x):
    lane = lax.broadcasted_iota(jnp.int32, x.shape, 1)
    first = (lane % (2 * ROPE_FREQS)) < ROPE_FREQS
    return jnp.where(first,
                     pltpu.roll(x, V7X_LANES - ROPE_FREQS, axis=1),
                     pltpu.roll(x, ROPE_FREQS, axis=1))


def _proj_kernel(*refs, rope, tm, n_tiles, n_sub):
    if rope:
        (h_ref, hp_ref, hn_ref, mod_ref, g_ref, w_ref, qkg_ref, cw_ref, gm_ref, cos_ref, sin_ref,
         qa_ref, qb_ref, ka_ref, va_ref, kb_ref, vb_ref, oc_ref, z_sc, cb_sc) = refs
    else:
        (h_ref, hp_ref, hn_ref, mod_ref, g_ref, w_ref, qkg_ref, cw_ref, gm_ref,
         qa_ref, qb_ref, ka_ref, va_ref, kb_ref, vb_ref, oc_ref, z_sc, cb_sc) = refs
    i = pl.program_id(1)
    shift = mod_ref[3:4, :]
    scale = mod_ref[4:5, :]
    g2 = g_ref[2:3, :]

    def pre(x):
        return (_rms(x, g2) * (1.0 + scale) + shift).astype(BF16)

    o_qa, o_qb = 0, A_Q_W
    o_ka = o_qb + B_W
    o_va = o_ka + A_KV_W
    o_kb = o_va + A_KV_W
    o_vb = o_kb + B_W
    o_cx = o_vb + B_W
    o_cb = o_cx + C_WIDTH
    o_cc = o_cb + C_WIDTH
    q_scale = HEAD_DIM ** -0.5 * LOG2_E
    half = (A_Q_W + A_KV_W) // 2
    sub = tm // n_sub

    for t in range(n_sub):
        rows = slice(t * sub, (t + 1) * sub)
        p = jnp.dot(pre(h_ref[rows, :]), w_ref[...], preferred_element_type=F32)
        xq = jnp.concatenate([p[:, o_qa:o_qa + A_Q_W], p[:, o_ka:o_ka + A_KV_W]], axis=1)
        sq = (xq * xq).astype(BF16)
        ms = jnp.concatenate(
            [jnp.dot(sq[:, :half], gm_ref[...], preferred_element_type=F32),
             jnp.dot(sq[:, half:], gm_ref[...], preferred_element_type=F32)], axis=1)
        xn = xq * lax.rsqrt(ms + EPS) * qkg_ref[...]
        if rope:
            cos = cos_ref[rows, :]
            sin = sin_ref[rows, :]
            cols = []
            for j in range((A_Q_W + A_KV_W) // V7X_LANES):
                xc = xn[:, j * V7X_LANES:(j + 1) * V7X_LANES]
                cols.append(xc * cos + _swap_rope_partners(xc) * sin)
            xn = jnp.concatenate(cols, axis=1)
        qa_ref[rows, :] = (xn[:, :A_Q_W] * q_scale).astype(BF16)
        ka_ref[rows, :] = xn[:, A_Q_W:].astype(BF16)
        qb_ref[rows, :] = (p[:, o_qb:o_qb + B_W] * q_scale).astype(BF16)
        va_ref[rows, :] = p[:, o_va:o_va + A_KV_W].astype(BF16)
        kb_ref[rows, :] = p[:, o_kb:o_kb + B_W].astype(BF16)
        vb_ref[rows, :] = p[:, o_vb:o_vb + B_W].astype(BF16)
        z_sc[rows, :] = p[:, o_cc:o_cc + C_WIDTH] * p[:, o_cx:o_cx + C_WIDTH]
        cb_sc[rows, :] = p[:, o_cb:o_cb + C_WIDTH]

    z = z_sc[...]

    def halo(ref):
        ub = pre(ref[...])
        return (jnp.dot(ub, w_ref[:, o_cc:o_cc + C_WIDTH], preferred_element_type=F32)
                * jnp.dot(ub, w_ref[:, o_cx:o_cx + C_WIDTH], preferred_element_type=F32))

    halo_rows = hp_ref.shape[0]
    z_before = jnp.where(i > 0, halo(hp_ref)[halo_rows - 1:halo_rows, :], 0.0)
    z_after = jnp.where(i < n_tiles - 1, halo(hn_ref)[0:1, :], 0.0)
    row = lax.broadcasted_iota(jnp.int32, z.shape, 0)
    z_m1 = jnp.where(row == 0, z_before, pltpu.roll(z, 1, axis=0))
    z_p1 = jnp.where(row == tm - 1, z_after, pltpu.roll(z, tm - 1, axis=0))
    y = cw_ref[0:1, :] * z_m1 + cw_ref[1:2, :] * z + cw_ref[2:3, :] * z_p1
    oc_ref[...] = (cb_sc[...] * y).astype(BF16)


def _proj(h, mods, norm_g, w_in, qk_gain, conv_w, gmat, rope_tabs, *, layer, mod_row, tm, n_sub=1):
    bsz, s, d = h.shape
    n_tiles = s // tm
    halo_rows = 8
    hb = tm // halo_rows
    n_hblk = s // halo_rows
    rope = rope_tabs is not None
    if mod_row is None:
        mod_map = lambda b, i: (layer, b, 0, 0)
    else:
        mod_map = lambda b, i: (layer, mod_row, 0, 0)
    pw = w_in.shape[2]
    in_specs = [
        pl.BlockSpec((None, tm, d), lambda b, i: (b, i, 0)),
        pl.BlockSpec((None, halo_rows, d), lambda b, i: (b, jnp.maximum(i * hb - 1, 0), 0)),
        pl.BlockSpec((None, halo_rows, d), lambda b, i: (b, jnp.minimum((i + 1) * hb, n_hblk - 1), 0)),
        pl.BlockSpec((None, None, N_MOD, d), mod_map),
        _resident((None, norm_g.shape[1], d), lambda b, i: (layer, 0, 0)),
        _resident((None, d, pw), lambda b, i: (layer, 0, 0)),
        _resident((None, 1, A_Q_W + A_KV_W), lambda b, i: (layer, 0, 0)),
        _resident((None, conv_w.shape[1], C_WIDTH), lambda b, i: (layer, 0, 0)),
        _resident(gmat.shape, lambda b, i: (0, 0)),
    ]
    args = [h, h, h, mods, norm_g, w_in, qk_gain, conv_w, gmat]
    if rope:
        in_specs += [pl.BlockSpec((tm, V7X_LANES), lambda b, i: (i, 0))] * 2
        args += list(rope_tabs)
    widths = (A_Q_W, B_W, A_KV_W, A_KV_W, B_W, B_W, C_WIDTH)
    out_specs = [pl.BlockSpec((None, tm, w), lambda b, i: (b, i, 0)) for w in widths]
    out_shape = [jax.ShapeDtypeStruct((bsz, s, w), BF16) for w in widths]
    vmem = d * pw * 2 + tm * d * 4 * 4 + tm * pw * 12
    return pl.pallas_call(
        functools.partial(_proj_kernel, rope=rope, tm=tm, n_tiles=n_tiles, n_sub=n_sub),
        grid=(bsz, n_tiles),
        in_specs=in_specs,
        out_specs=out_specs,
        out_shape=out_shape,
        scratch_shapes=[pltpu.VMEM((tm, C_WIDTH), F32), pltpu.VMEM((tm, C_WIDTH), F32)],
        compiler_params=_params(2, vmem, fusible=[a is w_in for a in args]),
        name="mix_proj",
    )(*args)


_NT_DIMS = (((1,), (1,)), ((), ()))

SCORE_BOUND_MAX = 40.0


def _load_q_rows(q_ref, qs_sc, heads, grp, tq):
    lane = lax.broadcasted_iota(jnp.int32, (tq, V7X_LANES), 1)
    low_half = lane < HEAD_DIM
    for j, h in enumerate(heads):
        xc = q_ref[:, (h // 2) * V7X_LANES:(h // 2 + 1) * V7X_LANES].astype(F32)
        dst_low = (h // grp) % 2 == 0
        if (h % 2 == 0) != dst_low:
            xc = pltpu.roll(xc, HEAD_DIM, axis=1)
        keep = low_half if dst_low else jnp.logical_not(low_half)
        qs_sc[j * tq:(j + 1) * tq, :] = jnp.where(keep, xc, 0.0).astype(BF16)


def _repack_heads(on_sc, hq, grp, tq):
    lane = lax.broadcasted_iota(jnp.int32, (tq, V7X_LANES), 1)
    low_half = lane < HEAD_DIM
    cols = []
    for oc in range(hq // 2):
        pieces = []
        for e in range(2):
            h = 2 * oc + e
            piece = on_sc[h * tq:(h + 1) * tq, :]
            src_low = (h // grp) % 2 == 0
            if src_low != (e == 0):
                piece = pltpu.roll(piece, HEAD_DIM, axis=1)
            pieces.append(piece)
        cols.append(jnp.where(low_half, pieces[0], pieces[1]).astype(BF16))
    return cols


def _online_softmax(qs_sc, m_sc, l_sc, acc_sc, rows, chunks, extra, *, tk, n_chunks, unroll):
    m_sc[0:rows, :] = jnp.full((rows, V7X_LANES), NEG, F32)
    l_sc[0:rows, :] = jnp.zeros((rows, V7X_LANES), F32)
    acc_sc[0:rows, :] = jnp.zeros((rows, V7X_LANES), F32)

    def load_state():
        return m_sc[0:rows, :], l_sc[0:rows, :], acc_sc[0:rows, :]

    def store_state(state):
        m_sc[0:rows, :], l_sc[0:rows, :], acc_sc[0:rows, :] = state

    def step(state, kc, vc):
        m_prev, l_prev, acc_prev = state
        s = lax.dot_general(qs_sc[0:rows, :], kc, _NT_DIMS, preferred_element_type=F32)
        m_next = jnp.maximum(m_prev, jnp.max(s, axis=1, keepdims=True))
        alpha = jnp.exp2(m_prev - m_next)
        p = jnp.exp2(s - jnp.concatenate([m_next] * (kc.shape[0] // V7X_LANES), axis=1))
        l_next = alpha * l_prev + jnp.sum(p, axis=1, keepdims=True)
        acc_next = alpha * acc_prev + jnp.dot(p.astype(BF16), vc, preferred_element_type=F32)
        return m_next, l_next, acc_next

    def body(c, carry):
        state = load_state()
        for u in range(unroll):
            state = step(state, *chunks(pl.multiple_of((c * unroll + u) * tk, tk)))
        store_state(state)
        return carry

    lax.fori_loop(0, n_chunks // unroll, body, 0)
    if extra is not None:
        store_state(step(load_state(), *extra))
    return acc_sc[0:rows, :] / l_sc[0:rows, :]


def _bounded_softmax(qs_sc, l_sc, acc_sc, rows, chunks, extra, *, tk, n_chunks, unroll):
    l_sc[0:rows, :] = jnp.zeros((rows, V7X_LANES), F32)
    acc_sc[0:rows, :] = jnp.zeros((rows, V7X_LANES), F32)

    def step(state, kc, vc):
        l_prev, acc_prev = state
        n_cols = kc.shape[0] // V7X_LANES
        s = lax.dot_general(qs_sc[0:rows, :], kc, _NT_DIMS, preferred_element_type=F32)
        p = jnp.exp2(s)
        l_next = l_prev
        for j in range(n_cols):
            l_next = l_next + p[:, j * V7X_LANES:(j + 1) * V7X_LANES]
        acc_next = acc_prev + jnp.dot(p.astype(BF16), vc, preferred_element_type=F32)
        return l_next, acc_next

    def body(c, carry):
        state = (l_sc[0:rows, :], acc_sc[0:rows, :])
        for u in range(unroll):
            state = step(state, *chunks(pl.multiple_of((c * unroll + u) * tk, tk)))
        l_sc[0:rows, :], acc_sc[0:rows, :] = state
        return carry

    n_trips = n_chunks // unroll
    lax.fori_loop(0, n_trips - 1, body, 0)
    state = (l_sc[0:rows, :], acc_sc[0:rows, :])
    for u in range(unroll):
        state = step(state, *chunks(((n_trips - 1) * unroll + u) * tk))
    if extra is not None:
        state = step(state, *extra)
    l_lanes, acc = state
    return acc / jnp.sum(l_lanes, axis=1, keepdims=True)


def _gqa_kernel(*refs, hq, hkv, tq, tk, n_chunks, unroll, extra_len):
    if extra_len:
        q_ref, k_ref, v_ref, kx_ref, vx_ref, o_ref, qs_sc, m_sc, l_sc, acc_sc, on_sc = refs
    else:
        q_ref, k_ref, v_ref, o_ref, qs_sc, m_sc, l_sc, acc_sc, on_sc = refs
    grp = hq // hkv
    for pc in range(hkv // 2):
        heads = [h for h in range(hq) if (h // grp) // 2 == pc]
        rows = len(heads) * tq
        col = slice(pc * V7X_LANES, (pc + 1) * V7X_LANES)
        _load_q_rows(q_ref, qs_sc, heads, grp, tq)
        chunks = lambda start: (k_ref[pl.ds(start, tk), col], v_ref[pl.ds(start, tk), col])
        extra = (kx_ref[:, col], vx_ref[:, col]) if extra_len else None
        on = _online_softmax(qs_sc, m_sc, l_sc, acc_sc, rows, chunks, extra,
                             tk=tk, n_chunks=n_chunks, unroll=unroll)
        for j, h in enumerate(heads):
            on_sc[h * tq:(h + 1) * tq, :] = on[j * tq:(j + 1) * tq, :]
    o_ref[...] = jnp.concatenate(_repack_heads(on_sc, hq, grp, tq), axis=1)


def _gqa_bounded_kernel(qsq_ref, q_ref, k_ref, v_ref, kx_ref, vx_ref, gs_ref, h_ref, ob_ref, oc_ref, mod_ref,
                        g_ref, wmix_ref, o_ref, qs_sc, m_sc, l_sc, acc_sc, on_sc, ok_sc,
                        *, layer, hq, tq, tk, n_chunks, unroll, n_q):
    hkv = 2
    grp = hq // hkv
    rows = hq * tq
    step = pl.program_id(1)

    def max_sq_norm(kc, mx):
        kf = kc.astype(F32)
        ss = lax.dot_general(gs_ref[...], (kf * kf).astype(BF16), _NT_DIMS, preferred_element_type=F32)
        return jnp.maximum(mx, jnp.max(ss, axis=1, keepdims=True))

    @pl.when(step == 0)
    def _():
        def kbody(c, mx):
            return max_sq_norm(k_ref[pl.ds(pl.multiple_of(c * tk, tk), tk), :], mx)
        mx = lax.fori_loop(0, n_chunks, kbody, jnp.zeros((V7X_LANES, 1), F32))
        k_sq = jnp.max(max_sq_norm(kx_ref[...], mx))
        score_sq = qsq_ref[layer] * k_sq * (1.0 + 2.0 ** -4)
        ok_sc[0] = (score_sq <= SCORE_BOUND_MAX * SCORE_BOUND_MAX).astype(jnp.int32)
        on_sc[...] = jnp.zeros(on_sc.shape, F32)

    o = jnp.concatenate(_repack_heads(on_sc, hq, grp, tq) + [ob_ref[...], oc_ref[...]], axis=1)
    y = jnp.dot(o, wmix_ref[...], preferred_element_type=F32)
    o_ref[...] = h_ref[...] + mod_ref[5:6, :] * _rms(y, g_ref[3:4, :])

    _load_q_rows(q_ref, qs_sc, list(range(hq)), grp, tq)
    bounded_ok = ok_sc[0] == 1
    has_tile = step < n_q

    chunks = lambda start: (k_ref[pl.ds(start, tk), :], v_ref[pl.ds(start, tk), :])
    extra = (kx_ref[...], vx_ref[...])

    @pl.when(jnp.logical_and(has_tile, bounded_ok))
    def _():
        on_sc[...] = _bounded_softmax(qs_sc, l_sc, acc_sc, rows, chunks, extra,
                                      tk=tk, n_chunks=n_chunks, unroll=unroll)

    @pl.when(jnp.logical_and(has_tile, jnp.logical_not(bounded_ok)))
    def _():
        on_sc[...] = _online_softmax(qs_sc, m_sc, l_sc, acc_sc, rows, chunks, extra,
                                     tk=tk, n_chunks=n_chunks, unroll=min(unroll, GQA_FALLBACK_UNROLL))


def _gqa(q, k, v, extra, *, hq, hkv, tq, tk, unroll=1):
    bsz, s, qw = q.shape
    t = k.shape[1]
    kw = k.shape[2]
    n_chunks = t // tk
    assert n_chunks % unroll == 0
    grp = hq // hkv
    max_heads = max(sum(1 for h in range(hq) if (h // grp) // 2 == pc) for pc in range(hkv // 2))
    rows = max_heads * tq
    in_specs = [
        pl.BlockSpec((None, tq, qw), lambda b, i: (b, i, 0)),
        pl.BlockSpec((None, t, kw), lambda b, i: (b, 0, 0)),
        pl.BlockSpec((None, t, kw), lambda b, i: (b, 0, 0)),
    ]
    args = [q, k, v]
    extra_len = 0
    if extra is not None:
        extra_len = extra[0].shape[1]
        in_specs += [pl.BlockSpec((None, extra_len, kw), lambda b, i: (b, 0, 0))] * 2
        args += list(extra)
    vmem = 4 * t * kw * 2 + rows * max(tk, extra_len) * 16 + rows * V7X_LANES * 24 + 8 * tq * qw * 2
    return pl.pallas_call(
        functools.partial(_gqa_kernel, hq=hq, hkv=hkv, tq=tq, tk=tk, n_chunks=n_chunks, unroll=unroll,
                          extra_len=extra_len),
        grid=(bsz, s // tq),
        in_specs=in_specs,
        out_specs=pl.BlockSpec((None, tq, qw), lambda b, i: (b, i, 0)),
        out_shape=jax.ShapeDtypeStruct(q.shape, BF16),
        scratch_shapes=[
            pltpu.VMEM((rows, V7X_LANES), BF16),
            pltpu.VMEM((rows, V7X_LANES), F32),
            pltpu.VMEM((rows, V7X_LANES), F32),
            pltpu.VMEM((rows, V7X_LANES), F32),
            pltpu.VMEM((hq * tq, V7X_LANES), F32),
        ],
        compiler_params=_params(2, vmem),
        name="gqa_attn",
    )(*args)


def _gqa_bounded(q_sq_max, q, k, v, kx, vx, gsum, h, ob, oc, mods, norm_g, w_o, *, layer, hq, tq, tk, unroll):
    bsz, s, qw = q.shape
    d = h.shape[2]
    t, kw = k.shape[1], k.shape[2]
    lx = kx.shape[1]
    n_q = s // tq
    assert kw == V7X_LANES and t % (tk * unroll) == 0
    rows = hq * tq
    stat = pltpu.VMEM((rows, V7X_LANES), F32)
    keys = lambda n: pl.BlockSpec((None, n, kw), lambda b, i: (b, 0, 0))
    prev = lambda w: pl.BlockSpec((None, tq, w), lambda b, i: (b, jnp.maximum(i - 1, 0), 0))
    vmem = (4 * (t + lx) * kw * 2 + rows * max(tk, lx) * 8 * min(unroll, 4) + rows * V7X_LANES * 32
            + 8 * tq * qw * 2 + w_o.shape[1] * d * 2 + 6 * tq * d * 4)
    return pl.pallas_call(
        functools.partial(_gqa_bounded_kernel, layer=layer, hq=hq, tq=tq, tk=tk, n_chunks=t // tk, unroll=unroll,
                          n_q=n_q),
        grid=(bsz, n_q + 1),
        in_specs=[
            pl.BlockSpec(memory_space=pltpu.SMEM),
            pl.BlockSpec((None, tq, qw), lambda b, i: (b, jnp.minimum(i, n_q - 1), 0)),
            keys(t), keys(t), keys(lx), keys(lx),
            _resident(gsum.shape, lambda b, i: (0, 0)),
            prev(d), prev(ob.shape[2]), prev(oc.shape[2]),
            pl.BlockSpec((None, None, N_MOD, d), lambda b, i: (layer, b, 0, 0)),
            _resident((None, norm_g.shape[1], d), lambda b, i: (layer, 0, 0)),
            _resident((None, w_o.shape[1], d), lambda b, i: (layer, 0, 0)),
        ],
        out_specs=prev(d),
        out_shape=jax.ShapeDtypeStruct(h.shape, F32),
        scratch_shapes=[
            pltpu.VMEM((rows, V7X_LANES), BF16), stat, stat, stat, stat,
            pltpu.SMEM((1,), jnp.int32),
        ],
        compiler_params=_params(2, vmem),
        name="gqa_attn_mix_out",
    )(q_sq_max, q, k, v, kx, vx, gsum, h, ob, oc, mods, norm_g, w_o)


def _nbr_kernel(q_ref, k_ref, v_ref, kx_ref, vx_ref, *rest, n_blocks, rows_total):
    bias_refs, o_ref = rest[:n_blocks], rest[n_blocks]
    tq = NBR_QROWS * GRID_W
    n_win = NBR_KROWS * GRID_W
    kx = kx_ref[...]
    vx = vx_ref[...]
    lane = lax.broadcasted_iota(jnp.int32, (tq, V7X_LANES), 1)
    low_half = lane < HEAD_DIM
    starts, scores, probs = {}, {}, {}

    def qk(j):
        rb = pl.program_id(2) * n_blocks + j
        krow0 = jnp.clip(rb * NBR_QROWS - WIN_R // 2, 0, rows_total - NBR_KROWS)
        starts[j] = pl.multiple_of(krow0 * GRID_W, NBR_QROWS * GRID_W)
        kw = k_ref[pl.ds(starts[j], n_win), :]
        q = q_ref[j * tq:(j + 1) * tq, :]
        zero = jnp.zeros_like(q)
        qs = jnp.concatenate([jnp.where(low_half, q, zero), jnp.where(low_half, zero, q)], axis=0)
        bias = bias_refs[j][...].reshape(2 * tq, n_win)
        scores[j] = (lax.dot_general(qs, kw, _NT_DIMS, preferred_element_type=F32) + bias,
                     lax.dot_general(qs, kx, _NT_DIMS, preferred_element_type=F32))

    def soft(j):
        s_win, s_ctx = scores.pop(j)
        m = jnp.maximum(jnp.max(s_win, axis=1, keepdims=True), jnp.max(s_ctx, axis=1, keepdims=True))
        p_win = jnp.exp2(s_win - m)
        p_ctx = jnp.exp2(s_ctx - m)
        l = jnp.sum(p_win, axis=1, keepdims=True) + jnp.sum(p_ctx, axis=1, keepdims=True)
        probs[j] = (p_win.astype(BF16), p_ctx.astype(BF16), l)

    def pv(j):
        p_win, p_ctx, l = probs.pop(j)
        vw = v_ref[pl.ds(starts[j], n_win), :]
        o = (jnp.dot(p_win, vw, preferred_element_type=F32)
             + jnp.dot(p_ctx, vx, preferred_element_type=F32)) / l
        o_ref[j * tq:(j + 1) * tq, :] = jnp.where(low_half, o[:tq], o[tq:]).astype(BF16)

    for stage in (qk, soft, pv):
        for j in range(n_blocks):
            stage(j)


def _nbr(q, k, v, kx, vx, bias, *, n_blocks=1):
    bsz, s, w = q.shape
    rows_total = s // GRID_W
    n_rblocks = rows_total // NBR_QROWS
    assert n_rblocks % n_blocks == 0
    tq = NBR_QROWS * GRID_W
    n_win = NBR_KROWS * GRID_W
    lx = kx.shape[1]
    n_pairs = w // V7X_LANES

    def bias_spec(j):
        def bias_map(b, pr, st):
            rb = st * n_blocks + j
            kind = jnp.where(rb == 0, 0, jnp.where(rb == n_rblocks - 1, 2, 1))
            return (kind, pr, 0, 0)
        return pl.BlockSpec((None, 2, tq, n_win), bias_map)

    qo_spec = pl.BlockSpec((None, n_blocks * tq, V7X_LANES), lambda b, pr, st: (b, st, pr))
    vmem = 4 * s * V7X_LANES * 2 + n_blocks * (2 * 2 * tq * n_win * 4 + tq * (n_win + lx) * 24)
    return pl.pallas_call(
        functools.partial(_nbr_kernel, n_blocks=n_blocks, rows_total=rows_total),
        grid=(bsz, n_pairs, n_rblocks // n_blocks),
        in_specs=[
            qo_spec,
            pl.BlockSpec((None, s, V7X_LANES), lambda b, pr, st: (b, 0, pr)),
            pl.BlockSpec((None, s, V7X_LANES), lambda b, pr, st: (b, 0, pr)),
            pl.BlockSpec((None, lx, V7X_LANES), lambda b, pr, st: (b, 0, pr)),
            pl.BlockSpec((None, lx, V7X_LANES), lambda b, pr, st: (b, 0, pr)),
        ] + [bias_spec(j) for j in range(n_blocks)],
        out_specs=qo_spec,
        out_shape=jax.ShapeDtypeStruct(q.shape, BF16),
        compiler_params=_params(3, vmem),
        name="nbr_attn",
    )(q, k, v, kx, vx, *([bias] * n_blocks))


def _nbr_bias_tables(rpb, rows_total):
    assert rows_total % NBR_QROWS == 0 and rows_total >= NBR_KROWS + NBR_QROWS
    wr = min(WIN_R, rows_total)
    kinds = [(0, 0), (2 * NBR_QROWS, 2 * NBR_QROWS - WIN_R // 2),
             (rows_total - NBR_QROWS, rows_total - NBR_KROWS)]
    qi = np.arange(NBR_QROWS)[:, None, None, None]
    qc = np.arange(GRID_W)[None, :, None, None]
    kj = np.arange(NBR_KROWS)[None, None, :, None]
    kc = np.arange(GRID_W)[None, None, None, :]
    n_dr, n_dc = rpb.shape[1], rpb.shape[2]
    cs = np.clip(qc - WIN_C // 2, 0, GRID_W - WIN_C)
    col_ok = (kc >= cs) & (kc < cs + WIN_C)
    col_sel = (kc - qc + (WIN_C - 1))[..., None] == np.arange(n_dc)
    col_sel = (col_sel & col_ok[..., None])[0, :, 0].astype(np.float32)
    row_sel, ok_all = [], []
    for r0, k0 in kinds:
        r = r0 + qi
        rs = np.clip(r - wr // 2, 0, rows_total - wr)
        kr = k0 + kj
        row_ok = (kr >= rs) & (kr < rs + wr)
        sel = ((kr - r + (WIN_R - 1))[..., None] == np.arange(n_dr)) & row_ok[..., None]
        row_sel.append(sel[:, 0, :, 0].astype(np.float32))
        ok_all.append(np.broadcast_to(row_ok & col_ok, (NBR_QROWS, GRID_W, NBR_KROWS, GRID_W)))
    row_sel = jnp.asarray(np.stack(row_sel))
    ok = np.stack(ok_all).reshape(3, 1, NBR_QROWS * GRID_W, NBR_KROWS * GRID_W)
    hi = lax.Precision.HIGHEST
    rows_picked = jnp.einsum('hrd,tijr->thijd', rpb, row_sel, precision=hi)
    tab = jnp.einsum('thijd,cnd->thicjn', rows_picked, jnp.asarray(col_sel), precision=hi)
    tab = tab.reshape(3, rpb.shape[0], NBR_QROWS * GRID_W, NBR_KROWS * GRID_W)
    return jnp.where(ok, tab * LOG2_E, NEG).astype(F32)


def _rope_tables(seq):
    pos = jnp.arange(seq, dtype=jnp.int32)
    row = (pos // GRID_W).astype(F32)
    col = (pos % GRID_W).astype(F32)
    freq = 1.0 / (ROPE_THETA ** (jnp.arange(ROPE_FREQS, dtype=F32) / ROPE_FREQS))
    ar = row[:, None] * freq
    ac = col[:, None] * freq
    cos = jnp.concatenate([jnp.cos(ar), jnp.cos(ar), jnp.cos(ac), jnp.cos(ac)], axis=1)
    sin = jnp.concatenate([-jnp.sin(ar), jnp.sin(ar), -jnp.sin(ac), jnp.sin(ac)], axis=1)
    reps = V7X_LANES // HEAD_DIM
    return jnp.tile(cos, (1, reps)), jnp.tile(sin, (1, reps))


def _tile_rows(n, target):
    t = min(n, target)
    assert n % t == 0
    return t


def kernel(x, c, ctx, c_ctx, w_ada, b_ada, norm_g, w_in, qk_g, rpb, conv_w, w_o, ffn_wi, ffn_wo):
    bsz, seq, d = x.shape
    ctx_len = ctx.shape[1]
    depth = w_ada.shape[0]
    ctx_row = bsz

    mod_rows = -(-(bsz + 1) // 8) * 8
    c_all = jnp.zeros((mod_rows, d), F32).at[:bsz].set(c).at[ctx_row].set(c_ctx)
    mods = _ada(c_all, w_ada, b_ada).reshape(depth, mod_rows, N_MOD, d)

    w_in_b = w_in.astype(BF16)
    w_o_b = w_o.astype(BF16)
    wi_b = ffn_wi.astype(BF16)
    wo_b = ffn_wo.astype(BF16)
    qk_gain = jnp.concatenate(
        [jnp.tile(qk_g[:, 0], (1, A_Q_HEADS)), jnp.tile(qk_g[:, 1], (1, A_KV_HEADS))], axis=1
    ).reshape(depth, 1, A_Q_W + A_KV_W)
    gsz = (A_Q_W + A_KV_W) // 2
    head_of = np.arange(gsz) // HEAD_DIM
    gmat = jnp.asarray((head_of[:, None] == head_of[None, :]) / HEAD_DIM, dtype=BF16)
    lane_head = np.arange(V7X_LANES) // HEAD_DIM
    gsum = jnp.asarray(lane_head[:, None] == lane_head[None, :], dtype=BF16)
    rope_tabs = _rope_tables(seq)
    q_sq_max = (LOG2_E ** 2) * jnp.max(jnp.square(qk_g[:, 0]), axis=-1).astype(F32)

    tm_ffn = _tile_rows(seq, 512)
    tm_proj = _tile_rows(seq, 2048)
    tm_ctx = _tile_rows(ctx_len, 256)
    tq_a = _tile_rows(seq, 256)
    tk_a = _tile_rows(seq, 512)
    unroll_a = min(GQA_UNROLL, seq // tk_a)
    n_rblocks = seq // (GRID_W * NBR_QROWS)
    nbr_blocks = NBR_BLOCKS_PER_STEP if n_rblocks % NBR_BLOCKS_PER_STEP == 0 else 1

    h, hc = x, ctx
    for layer in range(depth):
        last = layer == depth - 1
        lat = dict(layer=layer, mod_row=None, n_sub=max(tm_ffn // SUBTILE_ROWS, 1))
        cx = dict(layer=layer, mod_row=ctx_row, n_sub=1)
        h = _ffn(h, mods, norm_g, wi_b, wo_b, which=0, tm=tm_ffn, **lat)
        hc = _ffn(hc, mods, norm_g, wi_b, wo_b, which=0, tm=tm_ctx, **cx)
        qa, qb, ka, va, kb, vb, oc = _proj(h, mods, norm_g, w_in_b, qk_gain, conv_w, gmat, rope_tabs,
                                           tm=tm_proj, layer=layer, mod_row=None,
                                           n_sub=max(tm_proj // SUBTILE_ROWS, 1))
        cqa, cqb, cka, cva, ckb, cvb, coc = _proj(hc, mods, norm_g, w_in_b, qk_gain, conv_w, gmat, None,
                                                  tm=tm_ctx, layer=layer, mod_row=ctx_row)
        if not last:
            coa = _gqa(cqa, cka, cva, None, hq=A_Q_HEADS, hkv=A_KV_HEADS, tq=tm_ctx, tk=ctx_len)
            cob = _gqa(cqb, ckb, cvb, None, hq=B_HEADS, hkv=B_HEADS, tq=tm_ctx, tk=ctx_len)
            hc = _ffn(hc, mods, norm_g, wi_b, wo_b, which=1, tm=tm_ctx, mix=(coa, cob, coc, w_o_b), **cx)
        ob = _nbr(qb, kb, vb, ckb, cvb, _nbr_bias_tables(rpb[layer], seq // GRID_W), n_blocks=nbr_blocks)
        h = _gqa_bounded(q_sq_max, qa, ka, va, cka, cva, gsum, h, ob, oc, mods, norm_g, w_o_b,
                         layer=layer, hq=A_Q_HEADS, tq=tq_a, tk=tk_a, unroll=unroll_a)
        h = _ffn(h, mods, norm_g, wi_b, wo_b, which=1, tm=tm_ffn, **lat)
    return h
```
